```python
import math, functools
import jax, jax.numpy as jnp
from jax import lax
import numpy as np

D_MODEL = 1024
BATCH = 4
SEQ = 4096
DEPTH = 1
DEC_BATCH = 32
DEC_SEQ = 1
PAST_LEN = 8192
PAGE_SIZE = 128

A_HEADS = 4
A_HEAD_DIM = 64
A_QK_DIM = 2 * A_HEAD_DIM
A_V_DIM = 2 * A_HEAD_DIM
A_WIDTH = A_HEADS * A_V_DIM
R_HEAD = 64
R_HEADS = 8
R_WIDTH = R_HEADS * R_HEAD
R_LORA_W = 64
R_LORA_A = 64
R_LORA_G = 128
R_SHIFT_W = 3 * R_WIDTH + R_LORA_W + R_LORA_A + R_LORA_G
D_FF = -(-8 * D_MODEL // (3 * 256)) * 256
OFF_Q = 0
OFF_K = OFF_Q + A_HEADS * A_QK_DIM
OFF_V = OFF_K + A_HEADS * A_QK_DIM
OFF_R = OFF_V + A_WIDTH
OFF_G = OFF_R + R_SHIFT_W
IN_WIDTH = OFF_G + 2 * D_MODEL
Q_BLOCK = 128
NORM_EPS = 1e-6
GN_EPS = 64e-5

kernel_name = "griffin_diffattn_rwkv7_swiglu_step"


def _rms_norm(x, g, eps=NORM_EPS):
    xf = x.astype(jnp.float32)
    y = xf * lax.rsqrt(jnp.mean(xf * xf, axis=-1, keepdims=True) + eps)
    return (y * g.astype(jnp.float32)).astype(x.dtype)


def _alibi_slopes():
    return jnp.exp2(-8.0 * jnp.arange(1, A_HEADS + 1, dtype=jnp.float32) / A_HEADS)


def _diff_attn_core(q, k, v, q_pos, k_pos, lam):
    s = jnp.einsum("bqhcd,bkhcd->bhcqk", q, k, preferred_element_type=jnp.float32) * (A_HEAD_DIM ** -0.5)
    dist = q_pos[:, None] - k_pos[None, :]
    bias = -_alibi_slopes()[:, None, None] * dist.astype(jnp.float32)
    s = jnp.where(dist >= 0, s + bias[None, :, None], -jnp.inf)
    p = jax.nn.softmax(s, axis=-1)
    pd = p[:, :, 0] - lam * p[:, :, 1]
    return jnp.einsum("bhqk,bkhd->bqhd", pd.astype(v.dtype), v)


def _prompt_attend(q, k, v, lam):
    b, s = q.shape[:2]
    nb = s // Q_BLOCK
    qb = jnp.moveaxis(q.reshape(b, nb, Q_BLOCK, A_HEADS, 2, A_HEAD_DIM), 1, 0)
    k_pos = jnp.arange(s)

    def one_block(args):
        q_blk, i = args
        q_pos = i * Q_BLOCK + jnp.arange(Q_BLOCK)
        return _diff_attn_core(q_blk, k, v, q_pos, k_pos, lam)

    o = lax.map(one_block, (qb, jnp.arange(nb)))
    return jnp.moveaxis(o, 0, 1).reshape(b, s, A_HEADS, A_V_DIM)


def _sample_attend(q, k, v, lam, pool_k, pool_v, page_table):
    db, t = q.shape[:2]
    past = page_table.shape[1] * PAGE_SIZE
    past_k = pool_k[page_table].reshape(db, past, A_HEADS, 2, A_HEAD_DIM)
    past_v = pool_v[page_table].reshape(db, past, A_HEADS, A_V_DIM)
    k_all = jnp.concatenate([past_k.astype(k.dtype), k], axis=1)
    v_all = jnp.concatenate([past_v.astype(v.dtype), v], axis=1)
    q_pos = past + jnp.arange(t)
    k_pos = jnp.arange(past + t)
    return _diff_attn_core(q, k_all, v_all, q_pos, k_pos, lam)


def _rwkv7(p_cols, shift_prev, wkv0, p):
    b, t, _ = p_cols.shape
    f32 = jnp.float32
    prev = jnp.concatenate([shift_prev[:, None].astype(p_cols.dtype), p_cols[:, :-1]], axis=1)
    m = (p_cols + (prev - p_cols) * p["shift_mu"].astype(p_cols.dtype)).astype(f32)
    r, k, v, xw, xa, xg = jnp.split(
        m, [R_WIDTH, 2 * R_WIDTH, 3 * R_WIDTH, 3 * R_WIDTH + R_LORA_W,
            3 * R_WIDTH + R_LORA_W + R_LORA_A], axis=-1)
    w_raw = -jax.nn.softplus(-(p["w0"].astype(f32) + jnp.tanh(xw) @ p["w2"].astype(f32))) - 0.5
    decay = jnp.exp(-jnp.exp(w_raw))
    a = jax.nn.sigmoid(p["a0"].astype(f32) + xa @ p["a2"].astype(f32))
    g = jax.nn.sigmoid(xg) @ p["g2"].astype(f32)

    def hs(z):
        return z.reshape(b, t, R_HEADS, R_HEAD)

    kk = hs(k * p["k_k"].astype(f32))
    kk = kk * lax.rsqrt(jnp.maximum(jnp.sum(kk * kk, axis=-1, keepdims=True), 1e-24))
    k = k * (1.0 + (a - 1.0) * p["k_a"].astype(f32))
    r, decay, k, v, a = hs(r), hs(decay), hs(k), hs(v), hs(a)

    def step(S, inp):
        r_t, w_t, k_t, v_t, kk_t, a_t = inp
        sa = jnp.einsum("bhij,bhj->bhi", S, -kk_t)
        S = (S * w_t[:, :, None, :] + sa[..., None] * (kk_t * a_t)[:, :, None, :]
             + v_t[..., None] * k_t[:, :, None, :])
        return S, jnp.einsum("bhij,bhj->bhi", S, r_t)

    seq = tuple(jnp.moveaxis(z, 1, 0) for z in (r, decay, k, v, kk, a))
    S_fin, y = lax.scan(step, wkv0.astype(f32), seq)
    y = jnp.moveaxis(y, 0, 1)
    mu = jnp.mean(y, axis=-1, keepdims=True)
    var = jnp.mean(jnp.square(y - mu), axis=-1, keepdims=True)
    yn = ((y - mu) * lax.rsqrt(var + GN_EPS) * p["gn_w"].astype(f32).reshape(R_HEADS, R_HEAD)
          + p["gn_b"].astype(f32).reshape(R_HEADS, R_HEAD))
    bonus = jnp.sum(r * k * p["r_k"].astype(f32), axis=-1, keepdims=True) * v
    out = ((yn + bonus).reshape(b, t, R_WIDTH) * g).astype(p_cols.dtype) @ p["w_pb"]
    return out, S_fin.astype(wkv0.dtype), p_cols[:, -1].astype(shift_prev.dtype)


def _layer(x, shift_prev, wkv0, attend, l, p):
    b, t, _ = x.shape
    lambda_init = 0.8 - 0.6 * math.exp(-0.3 * l)
    h = _rms_norm(x, p["norm_mix"])
    proj = h @ p["w_in"]
    q = _rms_norm(proj[..., OFF_Q:OFF_K].reshape(b, t, A_HEADS, 2, A_HEAD_DIM), p["q_gain"])
    k = _rms_norm(proj[..., OFF_K:OFF_V].reshape(b, t, A_HEADS, 2, A_HEAD_DIM), p["k_gain"])
    v = proj[..., OFF_V:OFF_R].reshape(b, t, A_HEADS, A_V_DIM)
    f32 = jnp.float32
    lam = (jnp.exp(jnp.sum(p["lambda_q1"].astype(f32) * p["lambda_k1"].astype(f32)))
           - jnp.exp(jnp.sum(p["lambda_q2"].astype(f32) * p["lambda_k2"].astype(f32)))
           + lambda_init)
    o = attend(q, k, v, lam)
    o = _rms_norm(o, p["attn_out_gain"]) * (1.0 - lambda_init)
    y_a = o.reshape(b, t, A_WIDTH).astype(x.dtype) @ p["w_pa"]
    y_b, wkv_new, shift_new = _rwkv7(proj[..., OFF_R:OFF_G], shift_prev, wkv0, p)
    gates = jax.nn.sigmoid(proj[..., OFF_G:].astype(f32))
    merged = (gates[..., :D_MODEL] * y_a.astype(f32) + gates[..., D_MODEL:] * y_b.astype(f32)).astype(x.dtype)
    x = x + merged @ p["w_out"]
    hf = _rms_norm(x, p["norm_ffn"])
    x = x + (jax.nn.silu(hf @ p["w_gate"]) * (hf @ p["w_up"])) @ p["w_down"]
    return x, k.reshape(b, t, A_HEADS, A_QK_DIM), v, wkv_new, shift_new


def setup_inputs(seed: int = 0) -> dict:
    key = jax.random.key(seed)
    ks = iter(jax.random.split(key, 48))
    f32 = jnp.float32

    def nrm(shape, scale):
        return jax.random.normal(next(ks), shape, f32) * scale

    def gain(shape):
        return 1.0 + nrm(shape, 0.02)

    n_pages = PAST_LEN // PAGE_SIZE
    n_pool = -(-5 * DEC_BATCH * n_pages // 4)
    d = {}
    d["x_prompt"] = nrm((BATCH, SEQ, D_MODEL), 1.0)
    d["x_sample"] = nrm((DEC_BATCH, DEC_SEQ, D_MODEL), 1.0)
    d["cache_k"] = nrm((DEPTH, n_pool, PAGE_SIZE, A_HEADS, A_QK_DIM), 1.0)
    d["cache_v"] = nrm((DEPTH, n_pool, PAGE_SIZE, A_HEADS, A_V_DIM), 1.0)
    d["page_table"] = jax.random.permutation(next(ks), n_pool)[:DEC_BATCH * n_pages].reshape(
        DEC_BATCH, n_pages).astype(jnp.int32)
    d["state_wkv"] = nrm((DEPTH, DEC_BATCH, R_HEADS, R_HEAD, R_HEAD), 0.5)
    d["state_shift"] = nrm((DEPTH, DEC_BATCH, R_SHIFT_W), 1.0)
    d["norm_mix"] = gain((DEPTH, D_MODEL))
    d["w_in"] = nrm((DEPTH, D_MODEL, IN_WIDTH), D_MODEL ** -0.5)
    d["q_gain"] = gain((DEPTH, A_HEAD_DIM))
    d["k_gain"] = gain((DEPTH, A_HEAD_DIM))
    d["lambda_q1"] = nrm((DEPTH, A_HEAD_DIM), 0.1)
    d["lambda_k1"] = nrm((DEPTH, A_HEAD_DIM), 0.1)
    d["lambda_q2"] = nrm((DEPTH, A_HEAD_DIM), 0.1)
    d["lambda_k2"] = nrm((DEPTH, A_HEAD_DIM), 0.1)
    d["attn_out_gain"] = gain((DEPTH, A_V_DIM))
    d["w_pa"] = nrm((DEPTH, A_WIDTH, D_MODEL), A_WIDTH ** -0.5)
    d["shift_mu"] = jax.random.uniform(next(ks), (DEPTH, R_SHIFT_W), f32)
    d["w0"] = jax.random.uniform(next(ks), (DEPTH, R_WIDTH), f32, -5.0, 1.0)
    d["w2"] = nrm((DEPTH, R_LORA_W, R_WIDTH), 0.5 * R_LORA_W ** -0.5)
    d["a0"] = nrm((DEPTH, R_WIDTH), 0.1)
    d["a2"] = nrm((DEPTH, R_LORA_A, R_WIDTH), R_LORA_A ** -0.5)
    d["g2"] = nrm((DEPTH, R_LORA_G, R_WIDTH), R_LORA_G ** -0.5)
    d["k_k"] = 0.85 + nrm((DEPTH, R_WIDTH), 0.02)
    d["k_a"] = gain((DEPTH, R_WIDTH))
    d["r_k"] = nrm((DEPTH, R_HEADS, R_HEAD), 0.1)
    d["gn_w"] = gain((DEPTH, R_WIDTH))
    d["gn_b"] = nrm((DEPTH, R_WIDTH), 0.01)
    d["w_pb"] = nrm((DEPTH, R_WIDTH, D_MODEL), R_WIDTH ** -0.5)
    d["w_out"] = nrm((DEPTH, D_MODEL, D_MODEL), D_MODEL ** -0.5)
    d["norm_ffn"] = gain((DEPTH, D_MODEL))
    d["w_gate"] = nrm((DEPTH, D_MODEL, D_FF), D_MODEL ** -0.5)
    d["w_up"] = nrm((DEPTH, D_MODEL, D_FF), D_MODEL ** -0.5)
    d["w_down"] = nrm((DEPTH, D_FF, D_MODEL), D_FF ** -0.5)
    return d


def reference(x_prompt, x_sample, cache_k, cache_v, page_table, state_wkv, state_shift,
              norm_mix, w_in, q_gain, k_gain, lambda_q1, lambda_k1, lambda_q2, lambda_k2,
              attn_out_gain, w_pa, shift_mu, w0, w2, a0, a2, g2, k_k, k_a, r_k, gn_w, gn_b,
              w_pb, w_out, norm_ffn, w_gate, w_up, w_down):
    b = x_prompt.shape[0]
    xp, xs = x_prompt, x_sample
    kp_l, vp_l, wp_l, sp_l = [], [], [], []
    ks_l, vs_l, ws_l, ss_l = [], [], [], []
    for l in range(DEPTH):
        p = {"norm_mix": norm_mix[l], "w_in": w_in[l], "q_gain": q_gain[l], "k_gain": k_gain[l],
             "lambda_q1": lambda_q1[l], "lambda_k1": lambda_k1[l], "lambda_q2": lambda_q2[l],
             "lambda_k2": lambda_k2[l], "attn_out_gain": attn_out_gain[l], "w_pa": w_pa[l],
             "shift_mu": shift_mu[l], "w0": w0[l], "w2": w2[l], "a0": a0[l], "a2": a2[l],
             "g2": g2[l], "k_k": k_k[l], "k_a": k_a[l], "r_k": r_k[l], "gn_w": gn_w[l],
             "gn_b": gn_b[l], "w_pb": w_pb[l], "w_out": w_out[l], "norm_ffn": norm_ffn[l],
             "w_gate": w_gate[l], "w_up": w_up[l], "w_down": w_down[l]}
        shift0 = jnp.zeros((b, R_SHIFT_W), xp.dtype)
        wkv_init = jnp.zeros((b, R_HEADS, R_HEAD, R_HEAD), xp.dtype)
        xp, kp, vp, wp, sp = _layer(xp, shift0, wkv_init, _prompt_attend, l, p)
        attend_s = functools.partial(_sample_attend, pool_k=cache_k[l], pool_v=cache_v[l],
                                     page_table=page_table)
        xs, ks_, vs_, ws_, ss_ = _layer(xs, state_shift[l], state_wkv[l], attend_s, l, p)
        kp_l.append(kp); vp_l.append(vp); wp_l.append(wp); sp_l.append(sp)
        ks_l.append(ks_); vs_l.append(vs_); ws_l.append(ws_); ss_l.append(ss_)
    return (xp, xs, jnp.stack(kp_l), jnp.stack(vp_l), jnp.stack(wp_l), jnp.stack(sp_l),
            jnp.stack(ks_l), jnp.stack(vs_l), jnp.stack(ws_l), jnp.stack(ss_l))
```

```python
import functools
import math

import jax
import jax.numpy as jnp
from jax import lax
from jax.experimental import pallas as pl
from jax.experimental.pallas import tpu as pltpu

F32 = jnp.float32
BF16 = jnp.bfloat16

D_MODEL = 1024
PAGE_SIZE = 128
A_HEADS = 4
A_HEAD_DIM = 64
A_QK_DIM = 2 * A_HEAD_DIM
A_V_DIM = 2 * A_HEAD_DIM
A_WIDTH = A_HEADS * A_V_DIM
R_HEAD = 64
R_HEADS = 8
R_WIDTH = R_HEADS * R_HEAD
R_LORA_W = 64
R_LORA_A = 64
R_LORA_G = 128
R_SHIFT_W = 3 * R_WIDTH + R_LORA_W + R_LORA_A + R_LORA_G
OFF_Q = 0
OFF_K = OFF_Q + A_HEADS * A_QK_DIM
OFF_V = OFF_K + A_HEADS * A_QK_DIM
OFF_R = OFF_V + A_WIDTH
OFF_G = OFF_R + R_SHIFT_W
IN_WIDTH = OFF_G + 2 * D_MODEL
NORM_EPS = 1e-6
GN_EPS = 64e-5
NEG_BIG = -1e30

V7X_VMEM_LIMIT = 56 * 1024 * 1024
LANES = 128
R_PAIRS = R_WIDTH // LANES
RWKV_CHUNK = 64


def _dot(a, b):
    return jnp.dot(a, b, preferred_element_type=F32)


def _dot_nt(a, b):
    return lax.dot_general(a, b, (((1,), (1,)), ((), ())), preferred_element_type=F32)


def _dot_tn(a, b):
    return lax.dot_general(a, b, (((0,), (0,)), ((), ())), preferred_element_type=F32)


def _split2(x):
    hi = x.astype(BF16)
    lo = (x - hi.astype(F32)).astype(BF16)
    return hi, lo


def _mm3(a, b, dot=_dot):
    ah, al = _split2(a)
    bh, bl = _split2(b)
    return dot(ah, bh) + dot(ah, bl) + dot(al, bh)


def _sigmoid(x):
    return 1.0 / (1.0 + jnp.exp(-x))


def _iota(shape, dim):
    return lax.broadcasted_iota(jnp.int32, shape, dim)


def _group_sum(z, ones_bf):
    hi, lo = _split2(z)
    return _dot(hi, ones_bf) + _dot(lo, ones_bf)


def _rms_rows(x, gain_row):
    ms = jnp.mean(x * x, axis=-1, keepdims=True)
    return x * lax.rsqrt(ms + NORM_EPS) * gain_row


def _proj_kernel(x_ref, nm_ref, w_ref, qg_ref, kg_ref, ones_ref,
                 q_ref, k_ref, kb_ref, v_ref, vb_ref, r_ref, g_ref):
    h = _rms_rows(x_ref[...], nm_ref[...]).astype(BF16)
    ones = ones_ref[...]

    def head_norm(z, gain_row):
        ss = _group_sum(z * z, ones)
        return z * lax.rsqrt(ss * (1.0 / A_HEAD_DIM) + NORM_EPS) * gain_row

    q = head_norm(_dot(h, w_ref[:, OFF_Q:OFF_K]), qg_ref[...])
    q_ref[...] = (q * (A_HEAD_DIM ** -0.5)).astype(BF16)
    k = head_norm(_dot(h, w_ref[:, OFF_K:OFF_V]), kg_ref[...])
    k_ref[...] = k
    kb_ref[...] = k.astype(BF16)
    v = _dot(h, w_ref[:, OFF_V:OFF_R])
    v_ref[...] = v
    vb_ref[...] = v.astype(BF16)
    step = 512
    for c in range(0, R_SHIFT_W, step):
        n = min(step, R_SHIFT_W - c)
        r_ref[:, c:c + n] = _dot(h, w_ref[:, OFF_R + c:OFF_R + c + n])
    for c in range(0, 2 * D_MODEL, step):
        g_ref[:, c:c + step] = _sigmoid(_dot(h, w_ref[:, OFF_G + c:OFF_G + c + step]))


def _proj(x2, nm, w_in_bf, qg, kg, ones_bf, tm):
    m = x2.shape[0]
    row = lambda i: (i, 0)
    const = lambda i: (0, 0)
    out_shape = (
        jax.ShapeDtypeStruct((m, A_WIDTH), BF16),
        jax.ShapeDtypeStruct((m, A_WIDTH), F32),
        jax.ShapeDtypeStruct((m, A_WIDTH), BF16),
        jax.ShapeDtypeStruct((m, A_WIDTH), F32),
        jax.ShapeDtypeStruct((m, A_WIDTH), BF16),
        jax.ShapeDtypeStruct((m, R_SHIFT_W), F32),
        jax.ShapeDtypeStruct((m, 2 * D_MODEL), F32),
    )
    return pl.pallas_call(
        _proj_kernel,
        grid=(m // tm,),
        in_specs=[
            pl.BlockSpec((tm, D_MODEL), row),
            pl.BlockSpec((1, D_MODEL), const),
            pl.BlockSpec((D_MODEL, IN_WIDTH), const),
            pl.BlockSpec((1, A_WIDTH), const),
            pl.BlockSpec((1, A_WIDTH), const),
            pl.BlockSpec((A_WIDTH, A_WIDTH), const),
        ],
        out_specs=[
            pl.BlockSpec((tm, A_WIDTH), row),
            pl.BlockSpec((tm, A_WIDTH), row),
            pl.BlockSpec((tm, A_WIDTH), row),
            pl.BlockSpec((tm, A_WIDTH), row),
            pl.BlockSpec((tm, A_WIDTH), row),
            pl.BlockSpec((tm, R_SHIFT_W), row),
            pl.BlockSpec((tm, 2 * D_MODEL), row),
        ],
        out_shape=out_shape,
        compiler_params=pltpu.CompilerParams(
            dimension_semantics=("arbitrary",), vmem_limit_bytes=V7X_VMEM_LIMIT),
        name="proj",
    )(x2, nm, w_in_bf, qg, kg, ones_bf)


def _lambda_full(lq1, lk1, lq2, lk2, lambda_init):
    s1 = jnp.sum(lq1 * lk1, axis=-1, keepdims=True)
    s2 = jnp.sum(lq2 * lk2, axis=-1, keepdims=True)
    return jnp.exp(s1) - jnp.exp(s2) + lambda_init


def _softmax_update(s, m, l, acc, vs):
    m_new = jnp.maximum(m, jnp.max(s, axis=-1, keepdims=True))
    alpha = jnp.exp(m - m_new)
    p = jnp.exp(s - m_new)
    l = alpha * l + jnp.sum(p, axis=-1, keepdims=True)
    acc = alpha * acc + _dot(p.astype(BF16), vs)
    return m_new, l, acc


def _prompt_attn_kernel(q_ref, k_ref, v_ref, lq1_ref, lk1_ref, lq2_ref, lk2_ref, gain_ref,
                        o_ref, *, tq, lambda_init):
    h = pl.program_id(1)
    i = pl.program_id(2)
    slope = jnp.exp2(jnp.full((1, 1), -8.0 / A_HEADS, F32) * (h + 1).astype(F32))
    q = q_ref[...]
    lane = _iota((tq, A_QK_DIM), 1)
    zero = jnp.zeros_like(q)
    qc = (jnp.where(lane < A_HEAD_DIM, q, zero), jnp.where(lane >= A_HEAD_DIM, q, zero))
    row = _iota((tq, tq), 0)
    col = _iota((tq, tq), 1)
    rel_bias = slope * (col - row).astype(F32)

    def tile(j, carry, masked):
        ks = k_ref[pl.ds(pl.multiple_of(j * tq, tq), tq), :]
        vs = v_ref[pl.ds(pl.multiple_of(j * tq, tq), tq), :]
        off = slope * ((j - i) * tq).astype(F32)
        out = []
        for c in range(2):
            m, l, acc = carry[c]
            s = _dot_nt(qc[c], ks) + rel_bias
            if masked:
                s = jnp.where(row >= col, s, NEG_BIG)
            m_new, l, acc = _softmax_update(s, m - off, l, acc, vs)
            out.append((m_new + off, l, acc))
        return tuple(out)

    init = tuple((jnp.full((tq, 1), NEG_BIG, F32), jnp.zeros((tq, 1), F32),
                  jnp.zeros((tq, A_V_DIM), F32)) for _ in range(2))
    carry = lax.fori_loop(0, i, lambda j, c: tile(j, c, False), init)
    (_, l0, a0), (_, l1, a1) = tile(i, carry, True)
    lam = _lambda_full(lq1_ref[...], lk1_ref[...], lq2_ref[...], lk2_ref[...], lambda_init)
    o = a0 / l0 - lam * (a1 / l1)
    o_ref[...] = (_rms_rows(o, gain_ref[...]) * (1.0 - lambda_init)).astype(BF16)


def _prompt_attn(qb, kb, vb, lq1, lk1, lq2, lk2, gain, b, t, tq, lambda_init):
    nq = t // tq
    vec = lambda bi, h, i: (0, 0)
    return pl.pallas_call(
        functools.partial(_prompt_attn_kernel, tq=tq, lambda_init=lambda_init),
        grid=(b, A_HEADS, nq),
        in_specs=[
            pl.BlockSpec((tq, A_QK_DIM), lambda bi, h, i: (bi * nq + i, h)),
            pl.BlockSpec((t, A_QK_DIM), lambda bi, h, i: (bi, h)),
            pl.BlockSpec((t, A_V_DIM), lambda bi, h, i: (bi, h)),
            pl.BlockSpec((1, A_HEAD_DIM), vec),
            pl.BlockSpec((1, A_HEAD_DIM), vec),
            pl.BlockSpec((1, A_HEAD_DIM), vec),
            pl.BlockSpec((1, A_HEAD_DIM), vec),
            pl.BlockSpec((1, A_V_DIM), vec),
        ],
        out_specs=pl.BlockSpec((tq, A_V_DIM), lambda bi, h, i: (bi * nq + i, h)),
        out_shape=jax.ShapeDtypeStruct((b * t, A_WIDTH), BF16),
        compiler_params=pltpu.CompilerParams(
            dimension_semantics=("arbitrary", "arbitrary", "arbitrary"),
            vmem_limit_bytes=V7X_VMEM_LIMIT),
        name="prompt_attn",
    )(qb, kb, vb, lq1, lk1, lq2, lk2, gain)


def _decode_attn_kernel(pt_ref, q_ref, kn_ref, vn_ref, lq1_ref, lk1_ref, lq2_ref, lk2_ref,
                        gain_ref, *refs, pps, n_pages, lambda_init):
    k_refs = refs[:pps]
    v_refs = refs[pps:2 * pps]
    o_ref = refs[2 * pps]
    m_ref, l_ref, acc_ref = refs[2 * pps + 1:]
    j = pl.program_id(1)
    rows = 2 * A_HEADS
    row_w = _iota((rows, A_WIDTH), 0)
    lane_w = _iota((rows, A_WIDTH), 1)
    qmat = jnp.where((lane_w >> 6) == row_w, jnp.broadcast_to(q_ref[0], (rows, A_WIDTH)),
                     0.0).astype(BF16)
    slope = jnp.exp2((-8.0 / A_HEADS) * ((_iota((rows, 1), 0) >> 1) + 1).astype(F32))
    tok = _iota((rows, PAGE_SIZE), 1)

    @pl.when(j == 0)
    def _():
        m_ref[...] = jnp.full((rows, 1), NEG_BIG, F32)
        l_ref[...] = jnp.zeros((rows, 1), F32)
        acc_ref[...] = jnp.zeros((rows, A_WIDTH), F32)

    m, l, acc = m_ref[...], l_ref[...], acc_ref[...]
    for u in range(pps):
        kp = k_refs[u][0].astype(BF16)
        vp = v_refs[u][0].astype(BF16)
        dist = (n_pages - (j * pps + u)) * PAGE_SIZE - tok
        s = _dot_nt(qmat, kp) - slope * dist.astype(F32)
        m, l, acc = _softmax_update(s, m, l, acc, vp)
    m_ref[...], l_ref[...], acc_ref[...] = m, l, acc

    @pl.when(j == pl.num_programs(1) - 1)
    def _():
        s_new = jnp.sum(qmat.astype(F32) * kn_ref[0], axis=-1, keepdims=True)
        m_new = jnp.maximum(m, s_new)
        alpha = jnp.exp(m - m_new)
        p_new = jnp.exp(s_new - m_new)
        l_fin = alpha * l + p_new
        acc_fin = alpha * acc + p_new.astype(BF16).astype(F32) * vn_ref[0]
        lam = _lambda_full(lq1_ref[...], lk1_ref[...], lq2_ref[...], lk2_ref[...], lambda_init)
        comp = _iota((rows, 1), 0) & 1
        coef = jnp.where(comp == 0, 1.0, -lam) / l_fin
        keep = (lane_w >> 7) == (row_w >> 1)
        o_row = jnp.sum(jnp.where(keep, acc_fin * coef, 0.0), axis=0, keepdims=True)
        for hd in range(A_HEADS):
            sl = slice(hd * A_V_DIM, (hd + 1) * A_V_DIM)
            o_ref[0, :, sl] = _rms_rows(o_row[:, sl], gain_ref[...]) * (1.0 - lambda_init)


def _decode_attn(page_table, qb, kb, vb, cache_k, cache_v, lq1, lk1, lq2, lk2, gain,
                 pps, lambda_init):
    db, n_pages = page_table.shape
    n_pool = cache_k.shape[0]
    ck = cache_k.reshape(n_pool, PAGE_SIZE, A_WIDTH)
    cv = cache_v.reshape(n_pool, PAGE_SIZE, A_WIDTH)
    pt = page_table.reshape(db * n_pages)
    tok3 = lambda b, j, pt_ref: (b, 0, 0)
    vec = lambda b, j, pt_ref: (0, 0)

    def page_spec(u):
        return pl.BlockSpec((1, PAGE_SIZE, A_WIDTH),
                            lambda b, j, pt_ref: (pt_ref[b * n_pages + j * pps + u], 0, 0))

    rows = 2 * A_HEADS
    grid_spec = pltpu.PrefetchScalarGridSpec(
        num_scalar_prefetch=1,
        grid=(db, n_pages // pps),
        in_specs=[
            pl.BlockSpec((1, 1, A_WIDTH), tok3),
            pl.BlockSpec((1, 1, A_WIDTH), tok3),
            pl.BlockSpec((1, 1, A_WIDTH), tok3),
            pl.BlockSpec((1, A_HEAD_DIM), vec),
            pl.BlockSpec((1, A_HEAD_DIM), vec),
            pl.BlockSpec((1, A_HEAD_DIM), vec),
            pl.BlockSpec((1, A_HEAD_DIM), vec),
            pl.BlockSpec((1, A_V_DIM), vec),
        ] + [page_spec(u) for u in range(pps)] + [page_spec(u) for u in range(pps)],
        out_specs=pl.BlockSpec((1, 1, A_WIDTH), tok3),
        scratch_shapes=[pltpu.VMEM((rows, 1), F32), pltpu.VMEM((rows, 1), F32),
                        pltpu.VMEM((rows, A_WIDTH), F32)],
    )
    out = pl.pallas_call(
        functools.partial(_decode_attn_kernel, pps=pps, n_pages=n_pages, lambda_init=lambda_init),
        grid_spec=grid_spec,
        out_shape=jax.ShapeDtypeStruct((db, 1, A_WIDTH), F32),
        compiler_params=pltpu.CompilerParams(
            dimension_semantics=("arbitrary", "arbitrary"), vmem_limit_bytes=V7X_VMEM_LIMIT),
        name="decode_attn",
    )(pt, *(z.astype(F32).reshape(db, 1, A_WIDTH) for z in (qb, kb, vb)),
      lq1, lk1, lq2, lk2, gain, *([ck] * pps), *([cv] * pps))
    return out.reshape(db, A_WIDTH)


def _rwkv_prep(x, prev, mu, w0, w2e, a0, a2e, g2, k_k, k_a, r_k, ones):
    m = x + (prev - x) * mu
    r = m[:, 0:R_WIDTH]
    k = m[:, R_WIDTH:2 * R_WIDTH]
    v = m[:, 2 * R_WIDTH:3 * R_WIDTH]
    xwa = m[:, 3 * R_WIDTH:3 * R_WIDTH + R_LORA_W + R_LORA_A]
    xg = m[:, 3 * R_WIDTH + R_LORA_W + R_LORA_A:]
    z = w0 + _dot(jnp.tanh(xwa).astype(BF16), w2e)
    lw = (-math.exp(-0.5)) * _sigmoid(z)
    a = _sigmoid(a0 + _dot(xwa.astype(BF16), a2e))
    g = _dot(_sigmoid(xg).astype(BF16), g2)
    kk = k * k_k
    kk = kk * lax.rsqrt(jnp.maximum(_group_sum(kk * kk, ones), 1e-24))
    k = k * (1.0 + (a - 1.0) * k_a)
    bonus = _group_sum(r * k * r_k, ones) * v
    return r, lw, k, v, kk, a, g, bonus


def _rwkv_finish(y, bonus, g, gn_w, gn_b, ones):
    mu = _group_sum(y, ones) * (1.0 / R_HEAD)
    d = y - mu
    var = _group_sum(d * d, ones) * (1.0 / R_HEAD)
    yn = d * lax.rsqrt(var + GN_EPS) * gn_w + gn_b
    return ((yn + bonus) * g).astype(BF16)


def _lane_halves(x, width):
    lane = _iota(x.shape, 1)
    zero = jnp.zeros_like(x)
    return jnp.concatenate([jnp.where(lane < width, x, zero), jnp.where(lane >= width, x, zero)],
                           axis=0)


def _rwkv_chunk_kernel(x_ref, mu_ref, w0_ref, w2e_ref, a0_ref, a2e_ref, g2_ref, kk_ref, ka_ref,
                       rk_ref, gnw_ref, gnb_ref, ones_ref,
                       y_ref, st_ref, sh_ref, prev_ref, *, chunk):
    c = chunk
    j = pl.program_id(1)

    @pl.when(j == 0)
    def _():
        prev_ref[...] = jnp.zeros_like(prev_ref)
        st_ref[...] = jnp.zeros_like(st_ref)

    x = x_ref[0]
    rows = _iota((c, R_SHIFT_W), 0)
    prev = jnp.where(rows == 0, prev_ref[...], pltpu.roll(x, shift=1, axis=0))
    prev_ref[...] = x[c - 1:c, :]
    sh_ref[0] = x[c - 1:c, :]
    ones = ones_ref[...]
    r, lw, k, v, kk, a, g, bonus = _rwkv_prep(
        x, prev, mu_ref[...], w0_ref[...], w2e_ref[...], a0_ref[...], a2e_ref[...], g2_ref[...],
        kk_ref[...], ka_ref[...], rk_ref[...], ones)

    tri = (_iota((c, c), 0) >= _iota((c, c), 1)).astype(BF16)
    l_hi = lw.astype(BF16)
    l_r1 = lw - l_hi.astype(F32)
    l_mid = l_r1.astype(BF16)
    l_lo = (l_r1 - l_mid.astype(F32)).astype(BF16)
    cum = _dot(tri, l_hi) + _dot(tri, l_mid) + _dot(tri, l_lo)
    cum_end = cum[c - 1:c, :]
    e_neg = jnp.exp(-cum)
    e_end = jnp.exp(cum_end - cum)
    kka = kk * a
    ah = -kk * jnp.exp(cum - lw)
    rh = r * jnp.exp(cum)
    bt = kka * e_neg
    kt = k * e_neg
    bb = kka * e_end
    kb = k * e_end
    p_end = jnp.exp(cum_end)

    t_idx = _iota((c, 2 * c), 0)
    s_idx = _iota((c, 2 * c), 1) & (c - 1)
    strict = t_idx > s_idx
    incl = t_idx >= s_idx
    eye_pack = (t_idx == s_idx).astype(F32)
    bd_mask = (_iota((LANES, LANES), 0) >> 6) == (_iota((LANES, LANES), 1) >> 6)
    eye_l = _iota((LANES, LANES), 0) == _iota((LANES, LANES), 1)
    zeros_c = jnp.zeros((c, LANES), F32)

    ys = []
    for p in range(R_PAIRS):
        sl = slice(p * LANES, (p + 1) * LANES)
        ah_p, rh_p, v_p = ah[:, sl], rh[:, sl], v[:, sl]
        lhs = jnp.concatenate([ah_p, rh_p], axis=0)
        rhs = jnp.concatenate([_lane_halves(bt[:, sl], R_HEAD), _lane_halves(kt[:, sl], R_HEAD)],
                              axis=0)
        a_all = _mm3(lhs, rhs, _dot_nt)
        a_ab = jnp.where(strict, a_all[0:c, 0:2 * c], 0.0)
        a_ak = jnp.where(strict, a_all[0:c, 2 * c:4 * c], 0.0)
        a_rb = jnp.where(incl, a_all[c:2 * c, 0:2 * c], 0.0)
        a_rk = jnp.where(incl, a_all[c:2 * c, 2 * c:4 * c], 0.0)
        inv = eye_pack + a_ab
        pw = a_ab
        bd = _lane_halves(pw, c)
        for _ in range(int(math.log2(c)) - 1):
            pw = _mm3(pw, bd)
            bd = _lane_halves(pw, c)
            inv = inv + _mm3(inv, bd)
        w1 = _mm3(a_ak, _lane_halves(v_p, R_HEAD))
        t4 = _mm3(inv, jnp.concatenate([_lane_halves(w1, R_HEAD), _lane_halves(ah_p, R_HEAD)],
                                       axis=1))
        wu, ah2 = t4[:, 0:LANES], t4[:, LANES:]
        rhs5 = jnp.concatenate([
            jnp.concatenate([_lane_halves(wu, R_HEAD), _lane_halves(ah2, R_HEAD)], axis=1),
            jnp.concatenate([_lane_halves(v_p, R_HEAD), jnp.zeros((2 * c, LANES), F32)], axis=1),
        ], axis=0)
        t5 = _mm3(jnp.concatenate([a_rb, a_rk], axis=1), rhs5)
        y0, rp = t5[:, 0:LANES], rh_p + t5[:, LANES:]
        t6 = _mm3(jnp.concatenate([bb[:, sl], kb[:, sl]], axis=0),
                  jnp.concatenate([jnp.concatenate([wu, ah2], axis=1),
                                   jnp.concatenate([v_p, zeros_c], axis=1)], axis=0), _dot_tn)
        h_new = jnp.where(bd_mask, t6[:, 0:LANES], 0.0)
        g_x = jnp.where(bd_mask, t6[:, LANES:], 0.0)
        st = st_ref[0, p]
        ys.append(_mm3(rp, st) + y0)
        p_col = jnp.sum(jnp.where(eye_l, p_end[:, sl], 0.0), axis=1, keepdims=True)
        st_ref[0, p] = p_col * st + _mm3(g_x, st) + h_new

    y = jnp.concatenate(ys, axis=1)
    y_ref[0] = _rwkv_finish(y, bonus, g, gnw_ref[...], gnb_ref[...], ones)


def _rwkv_prompt(rc3, params, chunk):
    b, t, _ = rc3.shape
    const = lambda bi, j: (0, 0)
    specs = [pl.BlockSpec(p.shape, const) for p in params]
    return pl.pallas_call(
        functools.partial(_rwkv_chunk_kernel, chunk=chunk),
        grid=(b, t // chunk),
        in_specs=[pl.BlockSpec((1, chunk, R_SHIFT_W), lambda bi, j: (bi, j, 0))] + specs,
        out_specs=[
            pl.BlockSpec((1, chunk, R_WIDTH), lambda bi, j: (bi, j, 0)),
            pl.BlockSpec((1, R_PAIRS, LANES, LANES), lambda bi, j: (bi, 0, 0, 0)),
            pl.BlockSpec((1, 1, R_SHIFT_W), lambda bi, j: (bi, 0, 0)),
        ],
        out_shape=(
            jax.ShapeDtypeStruct((b, t, R_WIDTH), BF16),
            jax.ShapeDtypeStruct((b, R_PAIRS, LANES, LANES), F32),
            jax.ShapeDtypeStruct((b, 1, R_SHIFT_W), F32),
        ),
        scratch_shapes=[pltpu.VMEM((1, R_SHIFT_W), F32)],
        compiler_params=pltpu.CompilerParams(
            dimension_semantics=("arbitrary", "arbitrary"), vmem_limit_bytes=V7X_VMEM_LIMIT),
        name="rwkv_chunk",
    )(rc3, *params)


def _rwkv_step_kernel(x_ref, prev_ref, st_in_ref, mu_ref, w0_ref, w2e_ref, a0_ref, a2e_ref,
                      g2_ref, kk_ref, ka_ref, rk_ref, gnw_ref, gnb_ref, ones_ref,
                      y_ref, st_ref, yrow_ref, *, rows):
    ones = ones_ref[...]
    r, lw, k, v, kk, a, g, bonus = _rwkv_prep(
        x_ref[...], prev_ref[...], mu_ref[...], w0_ref[...], w2e_ref[...], a0_ref[...],
        a2e_ref[...], g2_ref[...], kk_ref[...], ka_ref[...], rk_ref[...], ones)
    vecs = (jnp.exp(lw), kk * a, k, v, kk, r)
    bd_mask = (_iota((LANES, LANES), 0) >> 6) == (_iota((LANES, LANES), 1) >> 6)
    eye_l = _iota((LANES, LANES), 0) == _iota((LANES, LANES), 1)

    def col(row_vec):
        return jnp.sum(jnp.where(eye_l, row_vec, 0.0), axis=1, keepdims=True)

    for bi in range(rows):
        for p in range(R_PAIRS):
            sl = slice(p * LANES, (p + 1) * LANES)
            w_r, b_r, k_r, v_r, kk_r, r_r = (z[bi:bi + 1, sl] for z in vecs)
            st = st_in_ref[bi, p]
            sa = -jnp.sum(col(kk_r) * st, axis=0, keepdims=True)
            upd = jnp.where(bd_mask, col(b_r) * sa + col(k_r) * v_r, 0.0)
            st_new = col(w_r) * st + upd
            st_ref[bi, p] = st_new
            yrow_ref[bi:bi + 1, sl] = jnp.sum(col(r_r) * st_new, axis=0, keepdims=True)
    y_ref[...] = _rwkv_finish(yrow_ref[...], bonus, g, gnw_ref[...], gnb_ref[...], ones)


def _rwkv_step(rc, shift_prev, st_pairs, params, rows):
    db = rc.shape[0]
    const = lambda i: (0, 0)
    specs = [pl.BlockSpec(p.shape, const) for p in params]
    return pl.pallas_call(
        functools.partial(_rwkv_step_kernel, rows=rows),
        grid=(db // rows,),
        in_specs=[
            pl.BlockSpec((rows, R_SHIFT_W), lambda i: (i, 0)),
            pl.BlockSpec((rows, R_SHIFT_W), lambda i: (i, 0)),
            pl.BlockSpec((rows, R_PAIRS, LANES, LANES), lambda i: (i, 0, 0, 0)),
        ] + specs,
        out_specs=[
            pl.BlockSpec((rows, R_WIDTH), lambda i: (i, 0)),
            pl.BlockSpec((rows, R_PAIRS, LANES, LANES), lambda i: (i, 0, 0, 0)),
        ],
        out_shape=(
            jax.ShapeDtypeStruct((db, R_WIDTH), BF16),
            jax.ShapeDtypeStruct((db, R_PAIRS, LANES, LANES), F32),
        ),
        scratch_shapes=[pltpu.VMEM((rows, R_WIDTH), F32)],
        compiler_params=pltpu.CompilerParams(
            dimension_semantics=("arbitrary",), vmem_limit_bytes=V7X_VMEM_LIMIT),
        name="rwkv_step",
    )(rc, shift_prev, st_pairs, *params)


def _state_to_pairs(s):
    b = s.shape[0]
    st = jnp.swapaxes(s, -1, -2).reshape(b, R_PAIRS, 2, R_HEAD, R_HEAD)
    eye2 = jnp.eye(2, dtype=s.dtype)
    return jnp.einsum("bpgij,gh->bpgihj", st, eye2).reshape(b, R_PAIRS, LANES, LANES)


def _pairs_to_state(st):
    b = st.shape[0]
    blocks = st.reshape(b, R_PAIRS, 2, R_HEAD, 2, R_HEAD)
    diag = jnp.stack([blocks[:, :, 0, :, 0, :], blocks[:, :, 1, :, 1, :]], axis=2)
    return jnp.swapaxes(diag.reshape(b, R_HEADS, R_HEAD, R_HEAD), -1, -2)


def _merge_ffn_kernel(x_ref, o_ref, rw_ref, g_ref, wpa_ref, wpb_ref, wout_ref, nf_ref,
                      wg_ref, wu_ref, wd_ref, y_ref, *, d_ff):
    ya = _dot(o_ref[...].astype(BF16), wpa_ref[...])
    yb = _dot(rw_ref[...], wpb_ref[...])
    merged = (g_ref[:, 0:D_MODEL] * ya + g_ref[:, D_MODEL:] * yb).astype(BF16)
    x1 = x_ref[...] + _dot(merged, wout_ref[...])
    hf = _rms_rows(x1, nf_ref[...]).astype(BF16)
    acc = x1
    step = 512
    for c in range(0, d_ff, step):
        n = min(step, d_ff - c)
        gate = _dot(hf, wg_ref[:, c:c + n])
        up = _dot(hf, wu_ref[:, c:c + n])
        act = (gate * _sigmoid(gate) * up).astype(BF16)
        acc = acc + _dot(act, wd_ref[c:c + n, :])
    y_ref[...] = acc


def _merge_ffn(x2, o, rw, gates, wpa, wpb, wout, nf, wg, wu, wd, tm):
    m = x2.shape[0]
    d_ff = wg.shape[1]
    row = lambda i: (i, 0)
    const = lambda i: (0, 0)
    return pl.pallas_call(
        functools.partial(_merge_ffn_kernel, d_ff=d_ff),
        grid=(m // tm,),
        in_specs=[
            pl.BlockSpec((tm, D_MODEL), row),
            pl.BlockSpec((tm, A_WIDTH), row),
            pl.BlockSpec((tm, R_WIDTH), row),
            pl.BlockSpec((tm, 2 * D_MODEL), row),
            pl.BlockSpec(wpa.shape, const),
            pl.BlockSpec(wpb.shape, const),
            pl.BlockSpec(wout.shape, const),
            pl.BlockSpec((1, D_MODEL), const),
            pl.BlockSpec(wg.shape, const),
            pl.BlockSpec(wu.shape, const),
            pl.BlockSpec(wd.shape, const),
        ],
        out_specs=pl.BlockSpec((tm, D_MODEL), row),
        out_shape=jax.ShapeDtypeStruct((m, D_MODEL), F32),
        compiler_params=pltpu.CompilerParams(
            dimension_semantics=("arbitrary",), vmem_limit_bytes=V7X_VMEM_LIMIT),
        name="merge_ffn",
    )(x2, o, rw, gates, wpa, wpb, wout, nf, wg, wu, wd)


def _row(v):
    return v.reshape(1, -1).astype(F32)


def _pick_tile(m, target):
    t = min(m, target)
    while m % t:
        t //= 2
    return t


def kernel(x_prompt, x_sample, cache_k, cache_v, page_table, state_wkv, state_shift, norm_mix, w_in, q_gain, k_gain, lambda_q1, lambda_k1, lambda_q2, lambda_k2, attn_out_gain, w_pa, shift_mu, w0, w2, a0, a2, g2, k_k, k_a, r_k, gn_w, gn_b, w_pb, w_out, norm_ffn, w_gate, w_up, w_down):
    b, t, _ = x_prompt.shape
    db, ds, _ = x_sample.shape
    assert ds == 1, "sample group carries one new token per sequence"
    depth = w_in.shape[0]
    n_pages = page_table.shape[1]
    xp = x_prompt.reshape(b * t, D_MODEL)
    xs = x_sample.reshape(db, D_MODEL)
    head_ones = (jnp.arange(A_WIDTH)[:, None] // A_HEAD_DIM
                 == jnp.arange(A_WIDTH)[None, :] // A_HEAD_DIM).astype(BF16)
    outs = [[] for _ in range(8)]
    tm_p = _pick_tile(b * t, 256)
    tm_s = _pick_tile(db, 256)
    tq = _pick_tile(t, 256)
    chunk = _pick_tile(t, RWKV_CHUNK)
    pps = _pick_tile(n_pages, 4)
    step_rows = _pick_tile(db, 16)
    for l in range(depth):
        lambda_init = 0.8 - 0.6 * math.exp(-0.3 * l)
        w_in_bf = w_in[l].astype(BF16)
        qg = _row(jnp.tile(q_gain[l], A_WIDTH // A_HEAD_DIM))
        kg = _row(jnp.tile(k_gain[l], A_WIDTH // A_HEAD_DIM))
        nm = _row(norm_mix[l])
        lam_vecs = (_row(lambda_q1[l]), _row(lambda_k1[l]), _row(lambda_q2[l]), _row(lambda_k2[l]))
        og = _row(attn_out_gain[l])
        zeros_lora = jnp.zeros((R_LORA_W, R_WIDTH), F32)
        rwkv_params = (
            _row(shift_mu[l]), _row(w0[l]),
            jnp.concatenate([w2[l], zeros_lora], axis=0).astype(BF16),
            _row(a0[l]),
            jnp.concatenate([zeros_lora, a2[l]], axis=0).astype(BF16),
            g2[l].astype(BF16), _row(k_k[l]), _row(k_a[l]), _row(r_k[l]),
            _row(gn_w[l]), _row(gn_b[l]), head_ones)
        ffn_w = (w_pa[l].astype(BF16), w_pb[l].astype(BF16), w_out[l].astype(BF16),
                 _row(norm_ffn[l]), w_gate[l].astype(BF16), w_up[l].astype(BF16),
                 w_down[l].astype(BF16))

        qb, k32, kb, v32, vb, rc, gates = _proj(xp, nm, w_in_bf, qg, kg, head_ones, tm_p)
        o = _prompt_attn(qb, kb, vb, *lam_vecs, og, b, t, tq, lambda_init)
        rw, st_p, sh_p = _rwkv_prompt(rc.reshape(b, t, R_SHIFT_W), rwkv_params, chunk)
        xp = _merge_ffn(xp, o, rw.reshape(b * t, R_WIDTH), gates, *ffn_w, tm_p)
        outs[0].append(k32.reshape(b, t, A_HEADS, A_QK_DIM))
        outs[1].append(v32.reshape(b, t, A_HEADS, A_V_DIM))
        outs[2].append(_pairs_to_state(st_p))
        outs[3].append(sh_p.reshape(b, R_SHIFT_W))

        qb, k32, kb, v32, vb, rc, gates = _proj(xs, nm, w_in_bf, qg, kg, head_ones, tm_s)
        o = _decode_attn(page_table, qb, kb, vb, cache_k[l], cache_v[l], *lam_vecs, og,
                         pps, lambda_init)
        rw, st_s = _rwkv_step(rc, state_shift[l], _state_to_pairs(state_wkv[l]), rwkv_params,
                              step_rows)
        xs = _merge_ffn(xs, o, rw, gates, *ffn_w, tm_s)
        outs[4].append(k32.reshape(db, 1, A_HEADS, A_QK_DIM))
        outs[5].append(v32.reshape(db, 1, A_HEADS, A_V_DIM))
        outs[6].append(_pairs_to_state(st_s))
        outs[7].append(rc)
    return (xp.reshape(b, t, D_MODEL), xs.reshape(db, 1, D_MODEL),
            *(jnp.stack(o) for o in outs))
```

```python
import functools
import math

import jax
import jax.numpy as jnp
from jax import lax
from jax.experimental import pallas as pl
from jax.experimental.pallas import tpu as pltpu

F32 = jnp.float32
BF16 = jnp.bfloat16

D_MODEL = 1024
PAGE_SIZE = 128
A_HEADS = 4
A_HEAD_DIM = 64
A_QK_DIM = 2 * A_HEAD_DIM
A_V_DIM = 2 * A_HEAD_DIM
A_WIDTH = A_HEADS * A_V_DIM
R_HEAD = 64
R_HEADS = 8
R_WIDTH = R_HEADS * R_HEAD
R_LORA_W = 64
R_LORA_A = 64
R_LORA_G = 128
R_SHIFT_W = 3 * R_WIDTH + R_LORA_W + R_LORA_A + R_LORA_G
OFF_Q = 0
OFF_K = OFF_Q + A_HEADS * A_QK_DIM
OFF_V = OFF_K + A_HEADS * A_QK_DIM
OFF_R = OFF_V + A_WIDTH
OFF_G = OFF_R + R_SHIFT_W
IN_WIDTH = OFF_G + 2 * D_MODEL
NORM_EPS = 1e-6
GN_EPS = 64e-5
NEG_BIG = -1e30

V7X_VMEM_LIMIT = 56 * 1024 * 1024
LANES = 128
R_PAIRS = R_WIDTH // LANES
RWKV_CHUNK = 64


def _dot(a, b):
    return jnp.dot(a, b, preferred_element_type=F32)


def _dot_nt(a, b):
    return lax.dot_general(a, b, (((1,), (1,)), ((), ())), preferred_element_type=F32)


def _dot_tn(a, b):
    return lax.dot_general(a, b, (((0,), (0,)), ((), ())), preferred_element_type=F32)


def _split2(x):
    hi = x.astype(BF16)
    lo = (x - hi.astype(F32)).astype(BF16)
    return hi, lo


def _mm(a, b, dot=_dot):
    return dot(a.astype(BF16), b.astype(BF16))


def _sigmoid(x):
    return 1.0 / (1.0 + jnp.exp(-x))


def _iota(shape, dim):
    return lax.broadcasted_iota(jnp.int32, shape, dim)


def _group_sum(z, ones_bf):
    hi, lo = _split2(z)
    return _dot(hi, ones_bf) + _dot(lo, ones_bf)


def _rms_rows(x, gain_row):
    ms = jnp.mean(x * x, axis=-1, keepdims=True)
    return x * lax.rsqrt(ms + NORM_EPS) * gain_row


def _proj_kernel(x_ref, nm_ref, w_ref, qg_ref, kg_ref, ones_ref,
                 q_ref, k_ref, kb_ref, v_ref, vb_ref, r_ref, g_ref):
    h = _rms_rows(x_ref[...], nm_ref[...]).astype(BF16)
    ones = ones_ref[...]

    def head_norm(z, gain_row):
        ss = _group_sum(z * z, ones)
        return z * lax.rsqrt(ss * (1.0 / A_HEAD_DIM) + NORM_EPS) * gain_row

    q = head_norm(_dot(h, w_ref[:, OFF_Q:OFF_K]), qg_ref[...])
    q_ref[...] = (q * (A_HEAD_DIM ** -0.5)).astype(BF16)
    k = head_norm(_dot(h, w_ref[:, OFF_K:OFF_V]), kg_ref[...])
    k_ref[...] = k
    kb_ref[...] = k.astype(BF16)
    v = _dot(h, w_ref[:, OFF_V:OFF_R])
    v_ref[...] = v
    vb_ref[...] = v.astype(BF16)
    step = 512
    for c in range(0, R_SHIFT_W, step):
        n = min(step, R_SHIFT_W - c)
        r_ref[:, c:c + n] = _dot(h, w_ref[:, OFF_R + c:OFF_R + c + n])
    for c in range(0, 2 * D_MODEL, step):
        g_ref[:, c:c + step] = _sigmoid(_dot(h, w_ref[:, OFF_G + c:OFF_G + c + step]))


def _proj(x2, nm, w_in_bf, qg, kg, ones_bf, tm):
    m = x2.shape[0]
    row = lambda i: (i, 0)
    const = lambda i: (0, 0)
    out_shape = (
        jax.ShapeDtypeStruct((m, A_WIDTH), BF16),
        jax.ShapeDtypeStruct((m, A_WIDTH), F32),
        jax.ShapeDtypeStruct((m, A_WIDTH), BF16),
        jax.ShapeDtypeStruct((m, A_WIDTH), F32),
        jax.ShapeDtypeStruct((m, A_WIDTH), BF16),
        jax.ShapeDtypeStruct((m, R_SHIFT_W), F32),
        jax.ShapeDtypeStruct((m, 2 * D_MODEL), F32),
    )
    return pl.pallas_call(
        _proj_kernel,
        grid=(m // tm,),
        in_specs=[
            pl.BlockSpec((tm, D_MODEL), row),
            pl.BlockSpec((1, D_MODEL), const),
            pl.BlockSpec((D_MODEL, IN_WIDTH), const),
            pl.BlockSpec((1, A_WIDTH), const),
            pl.BlockSpec((1, A_WIDTH), const),
            pl.BlockSpec((A_WIDTH, A_WIDTH), const),
        ],
        out_specs=[
            pl.BlockSpec((tm, A_WIDTH), row),
            pl.BlockSpec((tm, A_WIDTH), row),
            pl.BlockSpec((tm, A_WIDTH), row),
            pl.BlockSpec((tm, A_WIDTH), row),
            pl.BlockSpec((tm, A_WIDTH), row),
            pl.BlockSpec((tm, R_SHIFT_W), row),
            pl.BlockSpec((tm, 2 * D_MODEL), row),
        ],
        out_shape=out_shape,
        compiler_params=pltpu.CompilerParams(
            dimension_semantics=("arbitrary",), vmem_limit_bytes=V7X_VMEM_LIMIT),
        name="proj",
    )(x2, nm, w_in_bf, qg, kg, ones_bf)


def _lambda_full(lq1, lk1, lq2, lk2, lambda_init):
    s1 = jnp.sum(lq1 * lk1, axis=-1, keepdims=True)
    s2 = jnp.sum(lq2 * lk2, axis=-1, keepdims=True)
    return jnp.exp(s1) - jnp.exp(s2) + lambda_init


def _softmax_update(s, m, l, acc, vs):
    m_new = jnp.maximum(m, jnp.max(s, axis=-1, keepdims=True))
    alpha = jnp.exp(m - m_new)
    p = jnp.exp(s - m_new)
    l = alpha * l + jnp.sum(p, axis=-1, keepdims=True)
    acc = alpha * acc + _dot(p.astype(BF16), vs)
    return m_new, l, acc


def _prompt_attn_kernel(q_ref, k_ref, v_ref, lq1_ref, lk1_ref, lq2_ref, lk2_ref, gain_ref,
                        o_ref, *, tq, lambda_init):
    h = pl.program_id(1)
    i = pl.program_id(2)
    slope = jnp.exp2(jnp.full((1, 1), -8.0 / A_HEADS, F32) * (h + 1).astype(F32))
    q2 = _lane_halves(q_ref[...], A_HEAD_DIM)
    row = _iota((2 * tq, tq), 0) & (tq - 1)
    col = _iota((2 * tq, tq), 1)
    rel_bias = slope * (col - row).astype(F32)

    def tile(j, carry, masked):
        m, l, acc = carry
        ks = k_ref[pl.ds(pl.multiple_of(j * tq, tq), tq), :]
        vs = v_ref[pl.ds(pl.multiple_of(j * tq, tq), tq), :]
        off = slope * ((j - i) * tq).astype(F32)
        s = _dot_nt(q2, ks) + rel_bias
        if masked:
            s = jnp.where(row >= col, s, NEG_BIG)
        m_new, l, acc = _softmax_update(s, m - off, l, acc, vs)
        return m_new + off, l, acc

    init = (jnp.full((2 * tq, 1), NEG_BIG, F32), jnp.zeros((2 * tq, 1), F32),
            jnp.zeros((2 * tq, A_V_DIM), F32))
    carry = lax.fori_loop(0, i, lambda j, c: tile(j, c, False), init)
    _, l, acc = tile(i, carry, True)
    lam = _lambda_full(lq1_ref[...], lk1_ref[...], lq2_ref[...], lk2_ref[...], lambda_init)
    an = acc / l
    o = an[0:tq] - lam * an[tq:2 * tq]
    o_ref[...] = (_rms_rows(o, gain_ref[...]) * (1.0 - lambda_init)).astype(BF16)


def _prompt_attn(qb, kb, vb, lq1, lk1, lq2, lk2, gain, b, t, tq, lambda_init):
    nq = t // tq
    vec = lambda bi, h, i: (0, 0)
    return pl.pallas_call(
        functools.partial(_prompt_attn_kernel, tq=tq, lambda_init=lambda_init),
        grid=(b, A_HEADS, nq),
        in_specs=[
            pl.BlockSpec((tq, A_QK_DIM), lambda bi, h, i: (bi * nq + i, h)),
            pl.BlockSpec((t, A_QK_DIM), lambda bi, h, i: (bi, h)),
            pl.BlockSpec((t, A_V_DIM), lambda bi, h, i: (bi, h)),
            pl.BlockSpec((1, A_HEAD_DIM), vec),
            pl.BlockSpec((1, A_HEAD_DIM), vec),
            pl.BlockSpec((1, A_HEAD_DIM), vec),
            pl.BlockSpec((1, A_HEAD_DIM), vec),
            pl.BlockSpec((1, A_V_DIM), vec),
        ],
        out_specs=pl.BlockSpec((tq, A_V_DIM), lambda bi, h, i: (bi * nq + i, h)),
        out_shape=jax.ShapeDtypeStruct((b * t, A_WIDTH), BF16),
        compiler_params=pltpu.CompilerParams(
            dimension_semantics=("arbitrary", "arbitrary", "arbitrary"),
            vmem_limit_bytes=V7X_VMEM_LIMIT),
        name="prompt_attn",
    )(qb, kb, vb, lq1, lk1, lq2, lk2, gain)


def _decode_attn_kernel(pt_ref, q_ref, kn_ref, vn_ref, lq1_ref, lk1_ref, lq2_ref, lk2_ref,
                        gain_ref, *refs, pps, n_pages, lambda_init):
    k_refs = refs[:pps]
    v_refs = refs[pps:2 * pps]
    o_ref = refs[2 * pps]
    m_ref, l_ref, acc_ref = refs[2 * pps + 1:]
    j = pl.program_id(1)
    rows = 2 * A_HEADS
    page_rows = PAGE_SIZE * A_HEADS
    row_q = _iota((rows, A_QK_DIM), 0)
    lane_q = _iota((rows, A_QK_DIM), 1)
    qm = jnp.where((lane_q >> 6) == (row_q & 1), q_ref[0], 0.0).astype(BF16)
    slope = jnp.exp2((-8.0 / A_HEADS) * ((_iota((rows, 1), 0) >> 1) + 1).astype(F32))
    col = _iota((rows, page_rows), 1)
    own_head = (col & (A_HEADS - 1)) == (_iota((rows, page_rows), 0) >> 1)
    tok = col >> 2

    @pl.when(j == 0)
    def _():
        m_ref[...] = jnp.full((rows, 1), NEG_BIG, F32)
        l_ref[...] = jnp.zeros((rows, 1), F32)
        acc_ref[...] = jnp.zeros((rows, A_V_DIM), F32)

    m, l, acc = m_ref[...], l_ref[...], acc_ref[...]
    for u in range(pps):
        kp = k_refs[u][...].astype(BF16)
        vp = v_refs[u][...].astype(BF16)
        dist = (n_pages - (j * pps + u)) * PAGE_SIZE - tok
        s = jnp.where(own_head, _dot_nt(qm, kp) - slope * dist.astype(F32), NEG_BIG)
        m, l, acc = _softmax_update(s, m, l, acc, vp)
    m_ref[...], l_ref[...], acc_ref[...] = m, l, acc

    @pl.when(j == pl.num_programs(1) - 1)
    def _():
        s_new = jnp.sum(qm.astype(F32) * kn_ref[0], axis=-1, keepdims=True)
        m_new = jnp.maximum(m, s_new)
        alpha = jnp.exp(m - m_new)
        p_new = jnp.exp(s_new - m_new)
        l_fin = alpha * l + p_new
        acc_fin = alpha * acc + p_new.astype(BF16).astype(F32) * vn_ref[0]
        lam = _lambda_full(lq1_ref[...], lk1_ref[...], lq2_ref[...], lk2_ref[...], lambda_init)
        comp = _iota((rows, 1), 0) & 1
        x = acc_fin * (jnp.where(comp == 0, 1.0, -lam) / l_fin)
        o = x + pltpu.roll(x, shift=rows - 1, axis=0)
        o_ref[0] = _rms_rows(o, gain_ref[...]) * (1.0 - lambda_init)


def _decode_attn(page_table, qb, kb, vb, cache_k, cache_v, layer, lq1, lk1, lq2, lk2, gain,
                 pps, lambda_init):
    db, n_pages = page_table.shape
    n_pool = cache_k.shape[1]
    page_rows = PAGE_SIZE * A_HEADS
    ck = cache_k.reshape(-1, A_QK_DIM)
    cv = cache_v.reshape(-1, A_V_DIM)
    pt = page_table.reshape(db * n_pages)
    rows = 2 * A_HEADS
    tok3 = lambda b, j, pt_ref: (b, 0, 0)
    vec = lambda b, j, pt_ref: (0, 0)

    def page_spec(u):
        return pl.BlockSpec(
            (page_rows, A_QK_DIM),
            lambda b, j, pt_ref: (layer * n_pool + pt_ref[b * n_pages + j * pps + u], 0))

    def per_row(z):
        return jnp.repeat(z.astype(F32).reshape(db, A_HEADS, A_QK_DIM), 2, axis=1)

    grid_spec = pltpu.PrefetchScalarGridSpec(
        num_scalar_prefetch=1,
        grid=(db, n_pages // pps),
        in_specs=[
            pl.BlockSpec((1, rows, A_QK_DIM), tok3),
            pl.BlockSpec((1, rows, A_QK_DIM), tok3),
            pl.BlockSpec((1, rows, A_V_DIM), tok3),
            pl.BlockSpec((1, A_HEAD_DIM), vec),
            pl.BlockSpec((1, A_HEAD_DIM), vec),
            pl.BlockSpec((1, A_HEAD_DIM), vec),
            pl.BlockSpec((1, A_HEAD_DIM), vec),
            pl.BlockSpec((1, A_V_DIM), vec),
        ] + [page_spec(u) for u in range(pps)] + [page_spec(u) for u in range(pps)],
        out_specs=pl.BlockSpec((1, rows, A_V_DIM), tok3),
        scratch_shapes=[pltpu.VMEM((rows, 1), F32), pltpu.VMEM((rows, 1), F32),
                        pltpu.VMEM((rows, A_V_DIM), F32)],
    )
    out = pl.pallas_call(
        functools.partial(_decode_attn_kernel, pps=pps, n_pages=n_pages, lambda_init=lambda_init),
        grid_spec=grid_spec,
        out_shape=jax.ShapeDtypeStruct((db, rows, A_V_DIM), F32),
        compiler_params=pltpu.CompilerParams(
            dimension_semantics=("arbitrary", "arbitrary"), vmem_limit_bytes=V7X_VMEM_LIMIT),
        name="decode_attn",
    )(pt, per_row(qb), per_row(kb), per_row(vb),
      lq1, lk1, lq2, lk2, gain, *([ck] * pps), *([cv] * pps))
    return out[:, 0::2, :].reshape(db, A_WIDTH)


def _rwkv_prep(x, prev, mu, w0, w2e, a0, a2e, g2, k_k, k_a, r_k, ones):
    m = x + (prev - x) * mu
    r = m[:, 0:R_WIDTH]
    k = m[:, R_WIDTH:2 * R_WIDTH]
    v = m[:, 2 * R_WIDTH:3 * R_WIDTH]
    xwa = m[:, 3 * R_WIDTH:3 * R_WIDTH + R_LORA_W + R_LORA_A]
    xg = m[:, 3 * R_WIDTH + R_LORA_W + R_LORA_A:]
    z = w0 + _dot(jnp.tanh(xwa).astype(BF16), w2e)
    lw = (-math.exp(-0.5)) * _sigmoid(z)
    a = _sigmoid(a0 + _dot(xwa.astype(BF16), a2e))
    g = _dot(_sigmoid(xg).astype(BF16), g2)
    kk = k * k_k
    kk = kk * lax.rsqrt(jnp.maximum(_group_sum(kk * kk, ones), 1e-24))
    k = k * (1.0 + (a - 1.0) * k_a)
    bonus = _group_sum(r * k * r_k, ones) * v
    return r, lw, k, v, kk, a, g, bonus


def _rwkv_finish(y, bonus, g, gn_w, gn_b, ones):
    mu = _group_sum(y, ones) * (1.0 / R_HEAD)
    d = y - mu
    var = _group_sum(d * d, ones) * (1.0 / R_HEAD)
    yn = d * lax.rsqrt(var + GN_EPS) * gn_w + gn_b
    return ((yn + bonus) * g).astype(BF16)


def _lane_halves(x, width):
    lane = _iota(x.shape, 1)
    zero = jnp.zeros_like(x)
    return jnp.concatenate([jnp.where(lane < width, x, zero), jnp.where(lane >= width, x, zero)],
                           axis=0)


def _rwkv_chunk_kernel(x_ref, mu_ref, w0_ref, w2e_ref, a0_ref, a2e_ref, g2_ref, kk_ref, ka_ref,
                       rk_ref, gnw_ref, gnb_ref, ones_ref,
                       y_ref, st_ref, sh_ref, prev_ref, *, chunk, nb):
    c = chunk
    j = pl.program_id(1)

    @pl.when(j == 0)
    def _():
        prev_ref[...] = jnp.zeros_like(prev_ref)
        st_ref[...] = jnp.zeros_like(st_ref)

    ones = ones_ref[...]
    rows = _iota((c, R_SHIFT_W), 0)
    tri = (_iota((c, c), 0) >= _iota((c, c), 1)).astype(BF16)
    sls = [slice(p * LANES, (p + 1) * LANES) for p in range(R_PAIRS)]
    ah_p, rh_p, v_p, bt_p, kt_p, bb_p, kb_p, pe_p, st_idx, fin = ([] for _ in range(10))
    for bi in range(nb):
        x = x_ref[bi]
        prev = jnp.where(rows == 0, prev_ref[bi], pltpu.roll(x, shift=1, axis=0))
        prev_ref[bi] = x[c - 1:c, :]
        sh_ref[bi] = x[c - 1:c, :]
        r, lw, k, v, kk, a, g, bonus = _rwkv_prep(
            x, prev, mu_ref[...], w0_ref[...], w2e_ref[...], a0_ref[...], a2e_ref[...],
            g2_ref[...], kk_ref[...], ka_ref[...], rk_ref[...], ones)
        fin.append((bonus, g))
        l_hi = lw.astype(BF16)
        l_r1 = lw - l_hi.astype(F32)
        l_mid = l_r1.astype(BF16)
        l_lo = (l_r1 - l_mid.astype(F32)).astype(BF16)
        cum = _dot(tri, l_hi) + _dot(tri, l_mid) + _dot(tri, l_lo)
        cum_end = cum[c - 1:c, :]
        e_neg = jnp.exp(-cum)
        e_end = jnp.exp(cum_end - cum)
        kka = kk * a
        ah = -kk * jnp.exp(cum - lw)
        rh = r * jnp.exp(cum)
        bt = kka * e_neg
        kt = k * e_neg
        bb = kka * e_end
        kb = k * e_end
        p_end = jnp.exp(cum_end)
        for p, sl in enumerate(sls):
            for lst, z in ((ah_p, ah), (rh_p, rh), (v_p, v), (bt_p, bt), (kt_p, kt),
                           (bb_p, bb), (kb_p, kb), (pe_p, p_end)):
                lst.append(z[:, sl])
            st_idx.append((bi, p))

    t_idx = _iota((c, 2 * c), 0)
    s_idx = _iota((c, 2 * c), 1) & (c - 1)
    strict = t_idx > s_idx
    incl = t_idx >= s_idx
    eye_pack = (t_idx == s_idx).astype(F32)
    bd_mask = (_iota((LANES, LANES), 0) >> 6) == (_iota((LANES, LANES), 1) >> 6)
    eye_l = _iota((LANES, LANES), 0) == _iota((LANES, LANES), 1)
    zeros_c = jnp.zeros((c, LANES), F32)

    units = range(len(st_idx))
    v_blk = [_lane_halves(z, R_HEAD) for z in v_p]
    a_all = [_mm(jnp.concatenate([ah_p[u], rh_p[u]], axis=0),
                 jnp.concatenate([_lane_halves(bt_p[u], R_HEAD), _lane_halves(kt_p[u], R_HEAD)],
                                 axis=0), _dot_nt) for u in units]
    a_ab = [jnp.where(strict, z[0:c, 0:2 * c], 0.0) for z in a_all]
    a_ak = [jnp.where(strict, z[0:c, 2 * c:4 * c], 0.0) for z in a_all]
    a_rb = [jnp.where(incl, z[c:2 * c, 0:2 * c], 0.0) for z in a_all]
    a_rk = [jnp.where(incl, z[c:2 * c, 2 * c:4 * c], 0.0) for z in a_all]
    w1 = [_mm(a_ak[u], v_blk[u]) for u in units]
    inv = [eye_pack + z for z in a_ab]
    pw = a_ab
    bd = [_lane_halves(z, c) for z in pw]
    for _ in range(int(math.log2(c)) - 1):
        pw = [_mm(pw[u], bd[u]) for u in units]
        bd = [_lane_halves(z, c) for z in pw]
        inv = [inv[u] + _mm(inv[u], bd[u]) for u in units]
    t4 = [_mm(inv[u], jnp.concatenate([_lane_halves(w1[u], R_HEAD),
                                       _lane_halves(ah_p[u], R_HEAD)], axis=1)) for u in units]
    wu = [z[:, 0:LANES] for z in t4]
    ah2 = [z[:, LANES:] for z in t4]
    zeros_blk = jnp.zeros((2 * c, LANES), F32)
    t5 = [_mm(jnp.concatenate([a_rb[u], a_rk[u]], axis=1),
              jnp.concatenate([
                  jnp.concatenate([_lane_halves(wu[u], R_HEAD), _lane_halves(ah2[u], R_HEAD)],
                                  axis=1),
                  jnp.concatenate([v_blk[u], zeros_blk], axis=1)], axis=0)) for u in units]
    t6 = [_mm(jnp.concatenate([bb_p[u], kb_p[u]], axis=0),
              jnp.concatenate([jnp.concatenate([wu[u], ah2[u]], axis=1),
                               jnp.concatenate([v_p[u], zeros_c], axis=1)], axis=0), _dot_tn)
          for u in units]
    st = [st_ref[bi, p] for bi, p in st_idx]
    ys = [_mm(rh_p[u] + t5[u][:, LANES:], st[u]) + t5[u][:, 0:LANES] for u in units]
    for u in units:
        h_new = jnp.where(bd_mask, t6[u][:, 0:LANES], 0.0)
        g_x = jnp.where(bd_mask, t6[u][:, LANES:], 0.0)
        p_col = jnp.sum(jnp.where(eye_l, pe_p[u], 0.0), axis=1, keepdims=True)
        st_ref[st_idx[u]] = p_col * st[u] + _mm(g_x, st[u]) + h_new

    for bi in range(nb):
        y = jnp.concatenate(ys[bi * R_PAIRS:(bi + 1) * R_PAIRS], axis=1)
        bonus, g = fin[bi]
        y_ref[bi] = _rwkv_finish(y, bonus, g, gnw_ref[...], gnb_ref[...], ones)


def _rwkv_prompt(rc3, params, chunk, nb):
    b, t, _ = rc3.shape
    const = lambda bi, j: (0, 0)
    specs = [pl.BlockSpec(p.shape, const) for p in params]
    return pl.pallas_call(
        functools.partial(_rwkv_chunk_kernel, chunk=chunk, nb=nb),
        grid=(b // nb, t // chunk),
        in_specs=[pl.BlockSpec((nb, chunk, R_SHIFT_W), lambda bi, j: (bi, j, 0))] + specs,
        out_specs=[
            pl.BlockSpec((nb, chunk, R_WIDTH), lambda bi, j: (bi, j, 0)),
            pl.BlockSpec((nb, R_PAIRS, LANES, LANES), lambda bi, j: (bi, 0, 0, 0)),
            pl.BlockSpec((nb, 1, R_SHIFT_W), lambda bi, j: (bi, 0, 0)),
        ],
        out_shape=(
            jax.ShapeDtypeStruct((b, t, R_WIDTH), BF16),
            jax.ShapeDtypeStruct((b, R_PAIRS, LANES, LANES), F32),
            jax.ShapeDtypeStruct((b, 1, R_SHIFT_W), F32),
        ),
        scratch_shapes=[pltpu.VMEM((nb, 1, R_SHIFT_W), F32)],
        compiler_params=pltpu.CompilerParams(
            dimension_semantics=("arbitrary", "arbitrary"), vmem_limit_bytes=V7X_VMEM_LIMIT),
        name="rwkv_chunk",
    )(rc3, *params)


def _rwkv_step_kernel(x_ref, prev_ref, st_in_ref, mu_ref, w0_ref, w2e_ref, a0_ref, a2e_ref,
                      g2_ref, kk_ref, ka_ref, rk_ref, gnw_ref, gnb_ref, ones_ref,
                      y_ref, st_ref, yrow_ref, *, rows):
    ones = ones_ref[...]
    r, lw, k, v, kk, a, g, bonus = _rwkv_prep(
        x_ref[...], prev_ref[...], mu_ref[...], w0_ref[...], w2e_ref[...], a0_ref[...],
        a2e_ref[...], g2_ref[...], kk_ref[...], ka_ref[...], rk_ref[...], ones)
    vecs = (jnp.exp(lw), kk * a, k, v, kk, r)
    bd_mask = (_iota((LANES, LANES), 0) >> 6) == (_iota((LANES, LANES), 1) >> 6)
    eye_l = _iota((LANES, LANES), 0) == _iota((LANES, LANES), 1)

    def col(row_vec):
        return jnp.sum(jnp.where(eye_l, row_vec, 0.0), axis=1, keepdims=True)

    for bi in range(rows):
        for p in range(R_PAIRS):
            sl = slice(p * LANES, (p + 1) * LANES)
            w_r, b_r, k_r, v_r, kk_r, r_r = (z[bi:bi + 1, sl] for z in vecs)
            st = st_in_ref[bi, p]
            sa = -jnp.sum(col(kk_r) * st, axis=0, keepdims=True)
            upd = jnp.where(bd_mask, col(b_r) * sa + col(k_r) * v_r, 0.0)
            st_new = col(w_r) * st + upd
            st_ref[bi, p] = st_new
            yrow_ref[bi:bi + 1, sl] = jnp.sum(col(r_r) * st_new, axis=0, keepdims=True)
    y_ref[...] = _rwkv_finish(yrow_ref[...], bonus, g, gnw_ref[...], gnb_ref[...], ones)


def _rwkv_step(rc, shift_prev, st_pairs, params, rows):
    db = rc.shape[0]
    const = lambda i: (0, 0)
    specs = [pl.BlockSpec(p.shape, const) for p in params]
    return pl.pallas_call(
        functools.partial(_rwkv_step_kernel, rows=rows),
        grid=(db // rows,),
        in_specs=[
            pl.BlockSpec((rows, R_SHIFT_W), lambda i: (i, 0)),
            pl.BlockSpec((rows, R_SHIFT_W), lambda i: (i, 0)),
            pl.BlockSpec((rows, R_PAIRS, LANES, LANES), lambda i: (i, 0, 0, 0)),
        ] + specs,
        out_specs=[
            pl.BlockSpec((rows, R_WIDTH), lambda i: (i, 0)),
            pl.BlockSpec((rows, R_PAIRS, LANES, LANES), lambda i: (i, 0, 0, 0)),
        ],
        out_shape=(
            jax.ShapeDtypeStruct((db, R_WIDTH), BF16),
            jax.ShapeDtypeStruct((db, R_PAIRS, LANES, LANES), F32),
        ),
        scratch_shapes=[pltpu.VMEM((rows, R_WIDTH), F32)],
        compiler_params=pltpu.CompilerParams(
            dimension_semantics=("arbitrary",), vmem_limit_bytes=V7X_VMEM_LIMIT),
        name="rwkv_step",
    )(rc, shift_prev, st_pairs, *params)


def _state_to_pairs(s):
    b = s.shape[0]
    st = jnp.swapaxes(s, -1, -2).reshape(b, R_PAIRS, 2, R_HEAD, R_HEAD)
    eye2 = jnp.eye(2, dtype=s.dtype)
    return jnp.einsum("bpgij,gh->bpgihj", st, eye2).reshape(b, R_PAIRS, LANES, LANES)


def _pairs_to_state(st):
    b = st.shape[0]
    blocks = st.reshape(b, R_PAIRS, 2, R_HEAD, 2, R_HEAD)
    diag = jnp.stack([blocks[:, :, 0, :, 0, :], blocks[:, :, 1, :, 1, :]], axis=2)
    return jnp.swapaxes(diag.reshape(b, R_HEADS, R_HEAD, R_HEAD), -1, -2)


def _merge_ffn_kernel(x_ref, o_ref, rw_ref, g_ref, wpa_ref, wpb_ref, wout_ref, nf_ref,
                      wg_ref, wu_ref, wd_ref, y_ref, *, d_ff):
    ya = _dot(o_ref[...].astype(BF16), wpa_ref[...])
    yb = _dot(rw_ref[...], wpb_ref[...])
    merged = (g_ref[:, 0:D_MODEL] * ya + g_ref[:, D_MODEL:] * yb).astype(BF16)
    x1 = x_ref[...] + _dot(merged, wout_ref[...])
    hf = _rms_rows(x1, nf_ref[...]).astype(BF16)
    acc = x1
    step = 512
    for c in range(0, d_ff, step):
        n = min(step, d_ff - c)
        gate = _dot(hf, wg_ref[:, c:c + n])
        up = _dot(hf, wu_ref[:, c:c + n])
        act = (gate * _sigmoid(gate) * up).astype(BF16)
        acc = acc + _dot(act, wd_ref[c:c + n, :])
    y_ref[...] = acc


def _merge_ffn(x2, o, rw, gates, wpa, wpb, wout, nf, wg, wu, wd, tm):
    m = x2.shape[0]
    d_ff = wg.shape[1]
    row = lambda i: (i, 0)
    const = lambda i: (0, 0)
    return pl.pallas_call(
        functools.partial(_merge_ffn_kernel, d_ff=d_ff),
        grid=(m // tm,),
        in_specs=[
            pl.BlockSpec((tm, D_MODEL), row),
            pl.BlockSpec((tm, A_WIDTH), row),
            pl.BlockSpec((tm, R_WIDTH), row),
            pl.BlockSpec((tm, 2 * D_MODEL), row),
            pl.BlockSpec(wpa.shape, const),
            pl.BlockSpec(wpb.shape, const),
            pl.BlockSpec(wout.shape, const),
            pl.BlockSpec((1, D_MODEL), const),
            pl.BlockSpec(wg.shape, const),
            pl.BlockSpec(wu.shape, const),
            pl.BlockSpec(wd.shape, const),
        ],
        out_specs=pl.BlockSpec((tm, D_MODEL), row),
        out_shape=jax.ShapeDtypeStruct((m, D_MODEL), F32),
        compiler_params=pltpu.CompilerParams(
            dimension_semantics=("arbitrary",), vmem_limit_bytes=V7X_VMEM_LIMIT),
        name="merge_ffn",
    )(x2, o, rw, gates, wpa, wpb, wout, nf, wg, wu, wd)


def _row(v):
    return v.reshape(1, -1).astype(F32)


def _pick_tile(m, target):
    t = min(m, target)
    while m % t:
        t //= 2
    return t


def kernel(x_prompt, x_sample, cache_k, cache_v, page_table, state_wkv, state_shift, norm_mix, w_in, q_gain, k_gain, lambda_q1, lambda_k1, lambda_q2, lambda_k2, attn_out_gain, w_pa, shift_mu, w0, w2, a0, a2, g2, k_k, k_a, r_k, gn_w, gn_b, w_pb, w_out, norm_ffn, w_gate, w_up, w_down):
    b, t, _ = x_prompt.shape
    db, ds, _ = x_sample.shape
    assert ds == 1, "sample group carries one new token per sequence"
    depth = w_in.shape[0]
    n_pages = page_table.shape[1]
    xp = x_prompt.reshape(b * t, D_MODEL)
    xs = x_sample.reshape(db, D_MODEL)
    head_ones = (jnp.arange(A_WIDTH)[:, None] // A_HEAD_DIM
                 == jnp.arange(A_WIDTH)[None, :] // A_HEAD_DIM).astype(BF16)
    outs = [[] for _ in range(8)]
    tm_p = _pick_tile(b * t, 256)
    tm_s = _pick_tile(db, 256)
    tq = _pick_tile(t, 512)
    chunk = _pick_tile(t, RWKV_CHUNK)
    chunk_nb = _pick_tile(b, 4)
    pps = _pick_tile(n_pages, 8)
    step_rows = _pick_tile(db, 16)
    for l in range(depth):
        lambda_init = 0.8 - 0.6 * math.exp(-0.3 * l)
        w_in_bf = w_in[l].astype(BF16)
        qg = _row(jnp.tile(q_gain[l], A_WIDTH // A_HEAD_DIM))
        kg = _row(jnp.tile(k_gain[l], A_WIDTH // A_HEAD_DIM))
        nm = _row(norm_mix[l])
        lam_vecs = (_row(lambda_q1[l]), _row(lambda_k1[l]), _row(lambda_q2[l]), _row(lambda_k2[l]))
        og = _row(attn_out_gain[l])
        zeros_lora = jnp.zeros((R_LORA_W, R_WIDTH), F32)
        rwkv_params = (
            _row(shift_mu[l]), _row(w0[l]),
            jnp.concatenate([w2[l], zeros_lora], axis=0).astype(BF16),
            _row(a0[l]),
            jnp.concatenate([zeros_lora, a2[l]], axis=0).astype(BF16),
            g2[l].astype(BF16), _row(k_k[l]), _row(k_a[l]), _row(r_k[l]),
            _row(gn_w[l]), _row(gn_b[l]), head_ones)
        ffn_w = (w_pa[l].astype(BF16), w_pb[l].astype(BF16), w_out[l].astype(BF16),
                 _row(norm_ffn[l]), w_gate[l].astype(BF16), w_up[l].astype(BF16),
                 w_down[l].astype(BF16))

        qb, k32, kb, v32, vb, rc, gates = _proj(xp, nm, w_in_bf, qg, kg, head_ones, tm_p)
        o = _prompt_attn(qb, kb, vb, *lam_vecs, og, b, t, tq, lambda_init)
        rw, st_p, sh_p = _rwkv_prompt(rc.reshape(b, t, R_SHIFT_W), rwkv_params, chunk, chunk_nb)
        xp = _merge_ffn(xp, o, rw.reshape(b * t, R_WIDTH), gates, *ffn_w, tm_p)
        outs[0].append(k32.reshape(b, t, A_HEADS, A_QK_DIM))
        outs[1].append(v32.reshape(b, t, A_HEADS, A_V_DIM))
        outs[2].append(_pairs_to_state(st_p))
        outs[3].append(sh_p.reshape(b, R_SHIFT_W))

        qb, k32, kb, v32, vb, rc, gates = _proj(xs, nm, w_in_bf, qg, kg, head_ones, tm_s)
        o = _decode_attn(page_table, qb, kb, vb, cache_k, cache_v, l, *lam_vecs, og,
                         pps, lambda_init)
        rw, st_s = _rwkv_step(rc, state_shift[l], _state_to_pairs(state_wkv[l]), rwkv_params,
                              step_rows)
        xs = _merge_ffn(xs, o, rw, gates, *ffn_w, tm_s)
        outs[4].append(k32.reshape(db, 1, A_HEADS, A_QK_DIM))
        outs[5].append(v32.reshape(db, 1, A_HEADS, A_V_DIM))
        outs[6].append(_pairs_to_state(st_s))
        outs[7].append(rc)
    return (xp.reshape(b, t, D_MODEL), xs.reshape(db, 1, D_MODEL),
            *(jnp.stack(o) for o in outs))
```

```python
import functools
import math

import jax
import jax.numpy as jnp
from jax import lax
from jax.experimental import pallas as pl
from jax.experimental.pallas import tpu as pltpu

F32 = jnp.float32
BF16 = jnp.bfloat16

D_MODEL = 1024
PAGE_SIZE = 128
A_HEADS = 4
A_HEAD_DIM = 64
A_QK_DIM = 2 * A_HEAD_DIM
A_V_DIM = 2 * A_HEAD_DIM
A_WIDTH = A_HEADS * A_V_DIM
R_HEAD = 64
R_HEADS = 8
R_WIDTH = R_HEADS * R_HEAD
R_LORA_W = 64
R_LORA_A = 64
R_LORA_G = 128
R_SHIFT_W = 3 * R_WIDTH + R_LORA_W + R_LORA_A + R_LORA_G
OFF_Q = 0
OFF_K = OFF_Q + A_HEADS * A_QK_DIM
OFF_V = OFF_K + A_HEADS * A_QK_DIM
OFF_R = OFF_V + A_WIDTH
OFF_G = OFF_R + R_SHIFT_W
IN_WIDTH = OFF_G + 2 * D_MODEL
NORM_EPS = 1e-6
GN_EPS = 64e-5
NEG_BIG = -1e30

V7X_VMEM_LIMIT = 56 * 1024 * 1024
LANES = 128
R_PAIRS = R_WIDTH // LANES
RWKV_CHUNK = 64


def _dot(a, b):
    return jnp.dot(a, b, preferred_element_type=F32)


def _dot_nt(a, b):
    return lax.dot_general(a, b, (((1,), (1,)), ((), ())), preferred_element_type=F32)


def _dot_tn(a, b):
    return lax.dot_general(a, b, (((0,), (0,)), ((), ())), preferred_element_type=F32)


def _split2(x):
    hi = x.astype(BF16)
    lo = (x - hi.astype(F32)).astype(BF16)
    return hi, lo


def _mm(a, b, dot=_dot):
    return dot(a.astype(BF16), b.astype(BF16))


def _sigmoid(x):
    return 1.0 / (1.0 + jnp.exp(-x))


def _iota(shape, dim):
    return lax.broadcasted_iota(jnp.int32, shape, dim)


def _group_sum(z, ones_bf):
    hi, lo = _split2(z)
    return _dot(hi, ones_bf) + _dot(lo, ones_bf)


def _rms_rows(x, gain_row):
    ms = jnp.mean(x * x, axis=-1, keepdims=True)
    return x * lax.rsqrt(ms + NORM_EPS) * gain_row


def _proj_kernel(x_ref, nm_ref, w_ref, qg_ref, kg_ref, ones_ref,
                 q_ref, k_ref, kb_ref, v_ref, vb_ref, r_ref, g_ref):
    h = _rms_rows(x_ref[...], nm_ref[...]).astype(BF16)
    ones = ones_ref[...]

    def head_norm(z, gain_row):
        ss = _group_sum(z * z, ones)
        return z * lax.rsqrt(ss * (1.0 / A_HEAD_DIM) + NORM_EPS) * gain_row

    q = head_norm(_dot(h, w_ref[:, OFF_Q:OFF_K]), qg_ref[...])
    q_ref[...] = (q * (A_HEAD_DIM ** -0.5)).astype(BF16)
    k = head_norm(_dot(h, w_ref[:, OFF_K:OFF_V]), kg_ref[...])
    k_ref[...] = k
    kb_ref[...] = k.astype(BF16)
    v = _dot(h, w_ref[:, OFF_V:OFF_R])
    v_ref[...] = v
    vb_ref[...] = v.astype(BF16)
    step = 512
    for c in range(0, R_SHIFT_W, step):
        n = min(step, R_SHIFT_W - c)
        r_ref[:, c:c + n] = _dot(h, w_ref[:, OFF_R + c:OFF_R + c + n])
    for c in range(0, 2 * D_MODEL, step):
        g_ref[:, c:c + step] = _sigmoid(_dot(h, w_ref[:, OFF_G + c:OFF_G + c + step]))


def _proj(x2, nm, w_in_bf, qg, kg, ones_bf, tm):
    m = x2.shape[0]
    row = lambda i: (i, 0)
    const = lambda i: (0, 0)
    out_shape = (
        jax.ShapeDtypeStruct((m, A_WIDTH), BF16),
        jax.ShapeDtypeStruct((m, A_WIDTH), F32),
        jax.ShapeDtypeStruct((m, A_WIDTH), BF16),
        jax.ShapeDtypeStruct((m, A_WIDTH), F32),
        jax.ShapeDtypeStruct((m, A_WIDTH), BF16),
        jax.ShapeDtypeStruct((m, R_SHIFT_W), F32),
        jax.ShapeDtypeStruct((m, 2 * D_MODEL), F32),
    )
    return pl.pallas_call(
        _proj_kernel,
        grid=(m // tm,),
        in_specs=[
            pl.BlockSpec((tm, D_MODEL), row),
            pl.BlockSpec((1, D_MODEL), const),
            pl.BlockSpec((D_MODEL, IN_WIDTH), const),
            pl.BlockSpec((1, A_WIDTH), const),
            pl.BlockSpec((1, A_WIDTH), const),
            pl.BlockSpec((A_WIDTH, A_WIDTH), const),
        ],
        out_specs=[
            pl.BlockSpec((tm, A_WIDTH), row),
            pl.BlockSpec((tm, A_WIDTH), row),
            pl.BlockSpec((tm, A_WIDTH), row),
            pl.BlockSpec((tm, A_WIDTH), row),
            pl.BlockSpec((tm, A_WIDTH), row),
            pl.BlockSpec((tm, R_SHIFT_W), row),
            pl.BlockSpec((tm, 2 * D_MODEL), row),
        ],
        out_shape=out_shape,
        compiler_params=pltpu.CompilerParams(
            dimension_semantics=("arbitrary",), vmem_limit_bytes=V7X_VMEM_LIMIT),
        name="proj",
    )(x2, nm, w_in_bf, qg, kg, ones_bf)


def _lambda_full(lq1, lk1, lq2, lk2, lambda_init):
    s1 = jnp.sum(lq1 * lk1, axis=-1, keepdims=True)
    s2 = jnp.sum(lq2 * lk2, axis=-1, keepdims=True)
    return jnp.exp(s1) - jnp.exp(s2) + lambda_init


def _softmax_update(s, m, l, acc, vs):
    m_new = jnp.maximum(m, jnp.max(s, axis=-1, keepdims=True))
    alpha = jnp.exp(m - m_new)
    p = jnp.exp(s - m_new)
    l = alpha * l + jnp.sum(p, axis=-1, keepdims=True)
    acc = alpha * acc + _dot(p.astype(BF16), vs)
    return m_new, l, acc


def _prompt_attn_kernel(q_ref, k_ref, v_ref, lq1_ref, lk1_ref, lq2_ref, lk2_ref, gain_ref,
                        o_ref, *, tq, lambda_init):
    h = pl.program_id(1)
    i = pl.program_id(2)
    slope = jnp.exp2(jnp.full((1, 1), -8.0 / A_HEADS, F32) * (h + 1).astype(F32))
    q2 = _lane_halves(q_ref[...], A_HEAD_DIM)
    row = _iota((2 * tq, tq), 0) & (tq - 1)
    col = _iota((2 * tq, tq), 1)
    rel_bias = slope * (col - row).astype(F32)

    def tile(j, carry, masked):
        m, l, acc = carry
        ks = k_ref[pl.ds(pl.multiple_of(j * tq, tq), tq), :]
        vs = v_ref[pl.ds(pl.multiple_of(j * tq, tq), tq), :]
        off = slope * ((j - i) * tq).astype(F32)
        s = _dot_nt(q2, ks) + rel_bias
        if masked:
            s = jnp.where(row >= col, s, NEG_BIG)
        m_new, l, acc = _softmax_update(s, m - off, l, acc, vs)
        return m_new + off, l, acc

    init = (jnp.full((2 * tq, 1), NEG_BIG, F32), jnp.zeros((2 * tq, 1), F32),
            jnp.zeros((2 * tq, A_V_DIM), F32))
    carry = lax.fori_loop(0, i, lambda j, c: tile(j, c, False), init)
    _, l, acc = tile(i, carry, True)
    lam = _lambda_full(lq1_ref[...], lk1_ref[...], lq2_ref[...], lk2_ref[...], lambda_init)
    an = acc / l
    o = an[0:tq] - lam * an[tq:2 * tq]
    o_ref[...] = (_rms_rows(o, gain_ref[...]) * (1.0 - lambda_init)).astype(BF16)


def _prompt_attn(qb, kb, vb, lq1, lk1, lq2, lk2, gain, b, t, tq, lambda_init):
    nq = t // tq
    vec = lambda bi, h, i: (0, 0)
    return pl.pallas_call(
        functools.partial(_prompt_attn_kernel, tq=tq, lambda_init=lambda_init),
        grid=(b, A_HEADS, nq),
        in_specs=[
            pl.BlockSpec((tq, A_QK_DIM), lambda bi, h, i: (bi * nq + i, h)),
            pl.BlockSpec((t, A_QK_DIM), lambda bi, h, i: (bi, h)),
            pl.BlockSpec((t, A_V_DIM), lambda bi, h, i: (bi, h)),
            pl.BlockSpec((1, A_HEAD_DIM), vec),
            pl.BlockSpec((1, A_HEAD_DIM), vec),
            pl.BlockSpec((1, A_HEAD_DIM), vec),
            pl.BlockSpec((1, A_HEAD_DIM), vec),
            pl.BlockSpec((1, A_V_DIM), vec),
        ],
        out_specs=pl.BlockSpec((tq, A_V_DIM), lambda bi, h, i: (bi * nq + i, h)),
        out_shape=jax.ShapeDtypeStruct((b * t, A_WIDTH), BF16),
        compiler_params=pltpu.CompilerParams(
            dimension_semantics=("arbitrary", "arbitrary", "arbitrary"),
            vmem_limit_bytes=V7X_VMEM_LIMIT),
        name="prompt_attn",
    )(qb, kb, vb, lq1, lk1, lq2, lk2, gain)


def _decode_attn_kernel(pt_ref, q_ref, kn_ref, vn_ref, lq1_ref, lk1_ref, lq2_ref, lk2_ref,
                        gain_ref, *refs, pps, n_pages, lambda_init):
    k_refs = refs[:pps]
    v_refs = refs[pps:2 * pps]
    o_ref = refs[2 * pps]
    m_ref, l_ref, acc_ref = refs[2 * pps + 1:]
    j = pl.program_id(1)
    rows = 2 * A_HEADS
    page_rows = PAGE_SIZE * A_HEADS
    row_q = _iota((rows, A_QK_DIM), 0)
    lane_q = _iota((rows, A_QK_DIM), 1)
    qm = jnp.where((lane_q >> 6) == (row_q & 1), q_ref[0], 0.0).astype(BF16)
    slope = jnp.exp2((-8.0 / A_HEADS) * ((_iota((rows, 1), 0) >> 1) + 1).astype(F32))
    col = _iota((rows, pps * page_rows), 1)
    own_head = (col & (A_HEADS - 1)) == (_iota((rows, pps * page_rows), 0) >> 1)
    dist = (n_pages - j * pps) * PAGE_SIZE - (col >> 2)

    @pl.when(j == 0)
    def _():
        m_ref[...] = jnp.full((rows, 1), NEG_BIG, F32)
        l_ref[...] = jnp.zeros((rows, 1), F32)
        acc_ref[...] = jnp.zeros((rows, A_V_DIM), F32)

    s = jnp.concatenate([_dot_nt(qm, k_refs[u][...].astype(BF16)) for u in range(pps)], axis=1)
    s = jnp.where(own_head, s - slope * dist.astype(F32), NEG_BIG)
    m = m_ref[...]
    m_new = jnp.maximum(m, jnp.max(s, axis=-1, keepdims=True))
    alpha = jnp.exp(m - m_new)
    p = jnp.exp(s - m_new)
    pb = p.astype(BF16)
    pv = _dot(pb[:, 0:page_rows], v_refs[0][...].astype(BF16))
    for u in range(1, pps):
        pv = pv + _dot(pb[:, u * page_rows:(u + 1) * page_rows], v_refs[u][...].astype(BF16))
    m = m_new
    l = alpha * l_ref[...] + jnp.sum(p, axis=-1, keepdims=True)
    acc = alpha * acc_ref[...] + pv
    m_ref[...], l_ref[...], acc_ref[...] = m, l, acc

    @pl.when(j == pl.num_programs(1) - 1)
    def _():
        s_new = jnp.sum(qm.astype(F32) * kn_ref[0], axis=-1, keepdims=True)
        m_new = jnp.maximum(m, s_new)
        alpha = jnp.exp(m - m_new)
        p_new = jnp.exp(s_new - m_new)
        l_fin = alpha * l + p_new
        acc_fin = alpha * acc + p_new.astype(BF16).astype(F32) * vn_ref[0]
        lam = _lambda_full(lq1_ref[...], lk1_ref[...], lq2_ref[...], lk2_ref[...], lambda_init)
        comp = _iota((rows, 1), 0) & 1
        x = acc_fin * (jnp.where(comp == 0, 1.0, -lam) / l_fin)
        o = x + pltpu.roll(x, shift=rows - 1, axis=0)
        o_ref[0] = _rms_rows(o, gain_ref[...]) * (1.0 - lambda_init)


def _decode_attn(page_table, qb, kb, vb, cache_k, cache_v, layer, lq1, lk1, lq2, lk2, gain,
                 pps, lambda_init):
    db, n_pages = page_table.shape
    n_pool = cache_k.shape[1]
    page_rows = PAGE_SIZE * A_HEADS
    ck = cache_k.reshape(-1, A_QK_DIM)
    cv = cache_v.reshape(-1, A_V_DIM)
    pt = page_table.reshape(db * n_pages)
    rows = 2 * A_HEADS
    tok3 = lambda b, j, pt_ref: (b, 0, 0)
    vec = lambda b, j, pt_ref: (0, 0)

    def page_spec(u):
        return pl.BlockSpec(
            (page_rows, A_QK_DIM),
            lambda b, j, pt_ref: (layer * n_pool + pt_ref[b * n_pages + j * pps + u], 0))

    def per_row(z):
        return jnp.repeat(z.astype(F32).reshape(db, A_HEADS, A_QK_DIM), 2, axis=1)

    grid_spec = pltpu.PrefetchScalarGridSpec(
        num_scalar_prefetch=1,
        grid=(db, n_pages // pps),
        in_specs=[
            pl.BlockSpec((1, rows, A_QK_DIM), tok3),
            pl.BlockSpec((1, rows, A_QK_DIM), tok3),
            pl.BlockSpec((1, rows, A_V_DIM), tok3),
            pl.BlockSpec((1, A_HEAD_DIM), vec),
            pl.BlockSpec((1, A_HEAD_DIM), vec),
            pl.BlockSpec((1, A_HEAD_DIM), vec),
            pl.BlockSpec((1, A_HEAD_DIM), vec),
            pl.BlockSpec((1, A_V_DIM), vec),
        ] + [page_spec(u) for u in range(pps)] + [page_spec(u) for u in range(pps)],
        out_specs=pl.BlockSpec((1, rows, A_V_DIM), tok3),
        scratch_shapes=[pltpu.VMEM((rows, 1), F32), pltpu.VMEM((rows, 1), F32),
                        pltpu.VMEM((rows, A_V_DIM), F32)],
    )
    out = pl.pallas_call(
        functools.partial(_decode_attn_kernel, pps=pps, n_pages=n_pages, lambda_init=lambda_init),
        grid_spec=grid_spec,
        out_shape=jax.ShapeDtypeStruct((db, rows, A_V_DIM), F32),
        compiler_params=pltpu.CompilerParams(
            dimension_semantics=("arbitrary", "arbitrary"), vmem_limit_bytes=V7X_VMEM_LIMIT),
        name="decode_attn",
    )(pt, per_row(qb), per_row(kb), per_row(vb),
      lq1, lk1, lq2, lk2, gain, *([ck] * pps), *([cv] * pps))
    return out[:, 0::2, :].reshape(db, A_WIDTH)


def _rwkv_prep(x, prev, mu, w0, w2e, a0, a2e, g2, k_k, k_a, r_k, ones):
    m = x + (prev - x) * mu
    r = m[:, 0:R_WIDTH]
    k = m[:, R_WIDTH:2 * R_WIDTH]
    v = m[:, 2 * R_WIDTH:3 * R_WIDTH]
    xwa = m[:, 3 * R_WIDTH:3 * R_WIDTH + R_LORA_W + R_LORA_A]
    xg = m[:, 3 * R_WIDTH + R_LORA_W + R_LORA_A:]
    z = w0 + _dot(jnp.tanh(xwa).astype(BF16), w2e)
    lw = (-math.exp(-0.5)) * _sigmoid(z)
    a = _sigmoid(a0 + _dot(xwa.astype(BF16), a2e))
    g = _dot(_sigmoid(xg).astype(BF16), g2)
    kk = k * k_k
    kk = kk * lax.rsqrt(jnp.maximum(_group_sum(kk * kk, ones), 1e-24))
    k = k * (1.0 + (a - 1.0) * k_a)
    bonus = _group_sum(r * k * r_k, ones) * v
    return r, lw, k, v, kk, a, g, bonus


def _rwkv_finish(y, bonus, g, gn_w, gn_b, ones):
    mu = _group_sum(y, ones) * (1.0 / R_HEAD)
    d = y - mu
    var = _group_sum(d * d, ones) * (1.0 / R_HEAD)
    yn = d * lax.rsqrt(var + GN_EPS) * gn_w + gn_b
    return ((yn + bonus) * g).astype(BF16)


def _lane_halves(x, width):
    lane = _iota(x.shape, 1)
    zero = jnp.zeros_like(x)
    return jnp.concatenate([jnp.where(lane < width, x, zero), jnp.where(lane >= width, x, zero)],
                           axis=0)


def _rwkv_chunk_kernel(x_ref, mu_ref, w0_ref, w2e_ref, a0_ref, a2e_ref, g2_ref, kk_ref, ka_ref,
                       rk_ref, gnw_ref, gnb_ref, ones_ref,
                       y_ref, st_ref, sh_ref, prev_ref, *, chunk, nb):
    c = chunk
    j = pl.program_id(1)

    @pl.when(j == 0)
    def _():
        prev_ref[...] = jnp.zeros_like(prev_ref)
        st_ref[...] = jnp.zeros_like(st_ref)

    ones = ones_ref[...]
    rows = _iota((c, R_SHIFT_W), 0)
    tri = (_iota((c, c), 0) >= _iota((c, c), 1)).astype(BF16)
    sls = [slice(p * LANES, (p + 1) * LANES) for p in range(R_PAIRS)]
    ah_p, rh_p, v_p, bt_p, kt_p, bb_p, kb_p, pe_p, st_idx, fin = ([] for _ in range(10))
    for bi in range(nb):
        x = x_ref[bi]
        prev = jnp.where(rows == 0, prev_ref[bi], pltpu.roll(x, shift=1, axis=0))
        prev_ref[bi] = x[c - 1:c, :]
        sh_ref[bi] = x[c - 1:c, :]
        r, lw, k, v, kk, a, g, bonus = _rwkv_prep(
            x, prev, mu_ref[...], w0_ref[...], w2e_ref[...], a0_ref[...], a2e_ref[...],
            g2_ref[...], kk_ref[...], ka_ref[...], rk_ref[...], ones)
        fin.append((bonus, g))
        l_hi = lw.astype(BF16)
        l_r1 = lw - l_hi.astype(F32)
        l_mid = l_r1.astype(BF16)
        l_lo = (l_r1 - l_mid.astype(F32)).astype(BF16)
        cum = _dot(tri, l_hi) + _dot(tri, l_mid) + _dot(tri, l_lo)
        cum_end = cum[c - 1:c, :]
        e_neg = jnp.exp(-cum)
        e_end = jnp.exp(cum_end - cum)
        kka = kk * a
        ah = -kk * jnp.exp(cum - lw)
        rh = r * jnp.exp(cum)
        bt = kka * e_neg
        kt = k * e_neg
        bb = kka * e_end
        kb = k * e_end
        p_end = jnp.exp(cum_end)
        for p, sl in enumerate(sls):
            for lst, z in ((ah_p, ah), (rh_p, rh), (v_p, v), (bt_p, bt), (kt_p, kt),
                           (bb_p, bb), (kb_p, kb), (pe_p, p_end)):
                lst.append(z[:, sl])
            st_idx.append((bi, p))

    t_idx = _iota((c, 2 * c), 0)
    s_idx = _iota((c, 2 * c), 1) & (c - 1)
    strict = t_idx > s_idx
    incl = t_idx >= s_idx
    eye_pack = (t_idx == s_idx).astype(F32)
    bd_mask = (_iota((LANES, LANES), 0) >> 6) == (_iota((LANES, LANES), 1) >> 6)
    eye_l = _iota((LANES, LANES), 0) == _iota((LANES, LANES), 1)
    zeros_c = jnp.zeros((c, LANES), F32)

    units = range(len(st_idx))
    v_blk = [_lane_halves(z, R_HEAD) for z in v_p]
    a_all = [_mm(jnp.concatenate([ah_p[u], rh_p[u]], axis=0),
                 jnp.concatenate([_lane_halves(bt_p[u], R_HEAD), _lane_halves(kt_p[u], R_HEAD)],
                                 axis=0), _dot_nt) for u in units]
    a_ab = [jnp.where(strict, z[0:c, 0:2 * c], 0.0) for z in a_all]
    a_ak = [jnp.where(strict, z[0:c, 2 * c:4 * c], 0.0) for z in a_all]
    a_rb = [jnp.where(incl, z[c:2 * c, 0:2 * c], 0.0) for z in a_all]
    a_rk = [jnp.where(incl, z[c:2 * c, 2 * c:4 * c], 0.0) for z in a_all]
    w1 = [_mm(a_ak[u], v_blk[u]) for u in units]
    inv = [eye_pack + z for z in a_ab]
    bd = [_lane_halves(z, c) for z in a_ab]
    pw = [_mm(a_ab[u], bd[u]) for u in units]
    n_sq = int(math.log2(c)) - 1
    for step in range(n_sq):
        bd = [_lane_halves(z, c) for z in pw]
        if step < n_sq - 1:
            both = [_mm(jnp.concatenate([pw[u], inv[u]], axis=0), bd[u]) for u in units]
            pw = [z[0:c] for z in both]
            inv = [inv[u] + both[u][c:2 * c] for u in units]
        else:
            inv = [inv[u] + _mm(inv[u], bd[u]) for u in units]
    t4 = [_mm(inv[u], jnp.concatenate([_lane_halves(w1[u], R_HEAD),
                                       _lane_halves(ah_p[u], R_HEAD)], axis=1)) for u in units]
    wu = [z[:, 0:LANES] for z in t4]
    ah2 = [z[:, LANES:] for z in t4]
    zeros_blk = jnp.zeros((2 * c, LANES), F32)
    t5 = [_mm(jnp.concatenate([a_rb[u], a_rk[u]], axis=1),
              jnp.concatenate([
                  jnp.concatenate([_lane_halves(wu[u], R_HEAD), _lane_halves(ah2[u], R_HEAD)],
                                  axis=1),
                  jnp.concatenate([v_blk[u], zeros_blk], axis=1)], axis=0)) for u in units]
    t6 = [_mm(jnp.concatenate([bb_p[u], kb_p[u]], axis=0),
              jnp.concatenate([jnp.concatenate([wu[u], ah2[u]], axis=1),
                               jnp.concatenate([v_p[u], zeros_c], axis=1)], axis=0), _dot_tn)
          for u in units]
    st = [st_ref[bi, p] for bi, p in st_idx]
    ys = [_mm(rh_p[u] + t5[u][:, LANES:], st[u]) + t5[u][:, 0:LANES] for u in units]
    for u in units:
        h_new = jnp.where(bd_mask, t6[u][:, 0:LANES], 0.0)
        g_x = jnp.where(bd_mask, t6[u][:, LANES:], 0.0)
        p_col = jnp.sum(jnp.where(eye_l, pe_p[u], 0.0), axis=1, keepdims=True)
        st_ref[st_idx[u]] = p_col * st[u] + _mm(g_x, st[u]) + h_new

    for bi in range(nb):
        y = jnp.concatenate(ys[bi * R_PAIRS:(bi + 1) * R_PAIRS], axis=1)
        bonus, g = fin[bi]
        y_ref[bi] = _rwkv_finish(y, bonus, g, gnw_ref[...], gnb_ref[...], ones)


def _rwkv_prompt(rc3, params, chunk, nb):
    b, t, _ = rc3.shape
    const = lambda bi, j: (0, 0)
    specs = [pl.BlockSpec(p.shape, const) for p in params]
    return pl.pallas_call(
        functools.partial(_rwkv_chunk_kernel, chunk=chunk, nb=nb),
        grid=(b // nb, t // chunk),
        in_specs=[pl.BlockSpec((nb, chunk, R_SHIFT_W), lambda bi, j: (bi, j, 0))] + specs,
        out_specs=[
            pl.BlockSpec((nb, chunk, R_WIDTH), lambda bi, j: (bi, j, 0)),
            pl.BlockSpec((nb, R_PAIRS, LANES, LANES), lambda bi, j: (bi, 0, 0, 0)),
            pl.BlockSpec((nb, 1, R_SHIFT_W), lambda bi, j: (bi, 0, 0)),
        ],
        out_shape=(
            jax.ShapeDtypeStruct((b, t, R_WIDTH), BF16),
            jax.ShapeDtypeStruct((b, R_PAIRS, LANES, LANES), F32),
            jax.ShapeDtypeStruct((b, 1, R_SHIFT_W), F32),
        ),
        scratch_shapes=[pltpu.VMEM((nb, 1, R_SHIFT_W), F32)],
        compiler_params=pltpu.CompilerParams(
            dimension_semantics=("arbitrary", "arbitrary"), vmem_limit_bytes=V7X_VMEM_LIMIT),
        name="rwkv_chunk",
    )(rc3, *params)


def _rwkv_step_kernel(x_ref, prev_ref, st_in_ref, mu_ref, w0_ref, w2e_ref, a0_ref, a2e_ref,
                      g2_ref, kk_ref, ka_ref, rk_ref, gnw_ref, gnb_ref, ones_ref,
                      y_ref, st_ref, yrow_ref, *, rows):
    ones = ones_ref[...]
    r, lw, k, v, kk, a, g, bonus = _rwkv_prep(
        x_ref[...], prev_ref[...], mu_ref[...], w0_ref[...], w2e_ref[...], a0_ref[...],
        a2e_ref[...], g2_ref[...], kk_ref[...], ka_ref[...], rk_ref[...], ones)
    vecs = (jnp.exp(lw), kk * a, k, v, kk, r)
    bd_mask = (_iota((LANES, LANES), 0) >> 6) == (_iota((LANES, LANES), 1) >> 6)
    eye_l = _iota((LANES, LANES), 0) == _iota((LANES, LANES), 1)

    def col(row_vec):
        return jnp.sum(jnp.where(eye_l, row_vec, 0.0), axis=1, keepdims=True)

    for bi in range(rows):
        for p in range(R_PAIRS):
            sl = slice(p * LANES, (p + 1) * LANES)
            w_r, b_r, k_r, v_r, kk_r, r_r = (z[bi:bi + 1, sl] for z in vecs)
            st = st_in_ref[bi, p]
            sa = -jnp.sum(col(kk_r) * st, axis=0, keepdims=True)
            upd = jnp.where(bd_mask, col(b_r) * sa + col(k_r) * v_r, 0.0)
            st_new = col(w_r) * st + upd
            st_ref[bi, p] = st_new
            yrow_ref[bi:bi + 1, sl] = jnp.sum(col(r_r) * st_new, axis=0, keepdims=True)
    y_ref[...] = _rwkv_finish(yrow_ref[...], bonus, g, gnw_ref[...], gnb_ref[...], ones)


def _rwkv_step(rc, shift_prev, st_pairs, params, rows):
    db = rc.shape[0]
    const = lambda i: (0, 0)
    specs = [pl.BlockSpec(p.shape, const) for p in params]
    return pl.pallas_call(
        functools.partial(_rwkv_step_kernel, rows=rows),
        grid=(db // rows,),
        in_specs=[
            pl.BlockSpec((rows, R_SHIFT_W), lambda i: (i, 0)),
            pl.BlockSpec((rows, R_SHIFT_W), lambda i: (i, 0)),
            pl.BlockSpec((rows, R_PAIRS, LANES, LANES), lambda i: (i, 0, 0, 0)),
        ] + specs,
        out_specs=[
            pl.BlockSpec((rows, R_WIDTH), lambda i: (i, 0)),
            pl.BlockSpec((rows, R_PAIRS, LANES, LANES), lambda i: (i, 0, 0, 0)),
        ],
        out_shape=(
            jax.ShapeDtypeStruct((db, R_WIDTH), BF16),
            jax.ShapeDtypeStruct((db, R_PAIRS, LANES, LANES), F32),
        ),
        scratch_shapes=[pltpu.VMEM((rows, R_WIDTH), F32)],
        compiler_params=pltpu.CompilerParams(
            dimension_semantics=("arbitrary",), vmem_limit_bytes=V7X_VMEM_LIMIT),
        name="rwkv_step",
    )(rc, shift_prev, st_pairs, *params)


def _state_to_pairs(s):
    b = s.shape[0]
    st = jnp.swapaxes(s, -1, -2).reshape(b, R_PAIRS, 2, R_HEAD, R_HEAD)
    eye2 = jnp.eye(2, dtype=s.dtype)
    return jnp.einsum("bpgij,gh->bpgihj", st, eye2).reshape(b, R_PAIRS, LANES, LANES)


def _pairs_to_state(st):
    b = st.shape[0]
    blocks = st.reshape(b, R_PAIRS, 2, R_HEAD, 2, R_HEAD)
    diag = jnp.stack([blocks[:, :, 0, :, 0, :], blocks[:, :, 1, :, 1, :]], axis=2)
    return jnp.swapaxes(diag.reshape(b, R_HEADS, R_HEAD, R_HEAD), -1, -2)


def _merge_ffn_kernel(x_ref, o_ref, rw_ref, g_ref, wpa_ref, wpb_ref, wout_ref, nf_ref,
                      wg_ref, wu_ref, wd_ref, y_ref, *, d_ff):
    ya = _dot(o_ref[...].astype(BF16), wpa_ref[...])
    yb = _dot(rw_ref[...], wpb_ref[...])
    merged = (g_ref[:, 0:D_MODEL] * ya + g_ref[:, D_MODEL:] * yb).astype(BF16)
    x1 = x_ref[...] + _dot(merged, wout_ref[...])
    hf = _rms_rows(x1, nf_ref[...]).astype(BF16)
    acc = x1
    step = 512
    for c in range(0, d_ff, step):
        n = min(step, d_ff - c)
        gate = _dot(hf, wg_ref[:, c:c + n])
        up = _dot(hf, wu_ref[:, c:c + n])
        act = (gate * _sigmoid(gate) * up).astype(BF16)
        acc = acc + _dot(act, wd_ref[c:c + n, :])
    y_ref[...] = acc


def _merge_ffn(x2, o, rw, gates, wpa, wpb, wout, nf, wg, wu, wd, tm):
    m = x2.shape[0]
    d_ff = wg.shape[1]
    row = lambda i: (i, 0)
    const = lambda i: (0, 0)
    return pl.pallas_call(
        functools.partial(_merge_ffn_kernel, d_ff=d_ff),
        grid=(m // tm,),
        in_specs=[
            pl.BlockSpec((tm, D_MODEL), row),
            pl.BlockSpec((tm, A_WIDTH), row),
            pl.BlockSpec((tm, R_WIDTH), row),
            pl.BlockSpec((tm, 2 * D_MODEL), row),
            pl.BlockSpec(wpa.shape, const),
            pl.BlockSpec(wpb.shape, const),
            pl.BlockSpec(wout.shape, const),
            pl.BlockSpec((1, D_MODEL), const),
            pl.BlockSpec(wg.shape, const),
            pl.BlockSpec(wu.shape, const),
            pl.BlockSpec(wd.shape, const),
        ],
        out_specs=pl.BlockSpec((tm, D_MODEL), row),
        out_shape=jax.ShapeDtypeStruct((m, D_MODEL), F32),
        compiler_params=pltpu.CompilerParams(
            dimension_semantics=("arbitrary",), vmem_limit_bytes=V7X_VMEM_LIMIT),
        name="merge_ffn",
    )(x2, o, rw, gates, wpa, wpb, wout, nf, wg, wu, wd)


def _row(v):
    return v.reshape(1, -1).astype(F32)


def _pick_tile(m, target):
    t = min(m, target)
    while m % t:
        t //= 2
    return t


def kernel(x_prompt, x_sample, cache_k, cache_v, page_table, state_wkv, state_shift, norm_mix, w_in, q_gain, k_gain, lambda_q1, lambda_k1, lambda_q2, lambda_k2, attn_out_gain, w_pa, shift_mu, w0, w2, a0, a2, g2, k_k, k_a, r_k, gn_w, gn_b, w_pb, w_out, norm_ffn, w_gate, w_up, w_down):
    b, t, _ = x_prompt.shape
    db, ds, _ = x_sample.shape
    assert ds == 1, "sample group carries one new token per sequence"
    depth = w_in.shape[0]
    n_pages = page_table.shape[1]
    xp = x_prompt.reshape(b * t, D_MODEL)
    xs = x_sample.reshape(db, D_MODEL)
    head_ones = (jnp.arange(A_WIDTH)[:, None] // A_HEAD_DIM
                 == jnp.arange(A_WIDTH)[None, :] // A_HEAD_DIM).astype(BF16)
    outs = [[] for _ in range(8)]
    tm_p = _pick_tile(b * t, 512)
    tm_s = _pick_tile(db, 512)
    tq = _pick_tile(t, 512)
    chunk = _pick_tile(t, RWKV_CHUNK)
    chunk_nb = _pick_tile(b, 4)
    pps = _pick_tile(n_pages, 16)
    step_rows = _pick_tile(db, 16)
    for l in range(depth):
        lambda_init = 0.8 - 0.6 * math.exp(-0.3 * l)
        w_in_bf = w_in[l].astype(BF16)
        qg = _row(jnp.tile(q_gain[l], A_WIDTH // A_HEAD_DIM))
        kg = _row(jnp.tile(k_gain[l], A_WIDTH // A_HEAD_DIM))
        nm = _row(norm_mix[l])
        lam_vecs = (_row(lambda_q1[l]), _row(lambda_k1[l]), _row(lambda_q2[l]), _row(lambda_k2[l]))
        og = _row(attn_out_gain[l])
        zeros_lora = jnp.zeros((R_LORA_W, R_WIDTH), F32)
        rwkv_params = (
            _row(shift_mu[l]), _row(w0[l]),
            jnp.concatenate([w2[l], zeros_lora], axis=0).astype(BF16),
            _row(a0[l]),
            jnp.concatenate([zeros_lora, a2[l]], axis=0).astype(BF16),
            g2[l].astype(BF16), _row(k_k[l]), _row(k_a[l]), _row(r_k[l]),
            _row(gn_w[l]), _row(gn_b[l]), head_ones)
        ffn_w = (w_pa[l].astype(BF16), w_pb[l].astype(BF16), w_out[l].astype(BF16),
                 _row(norm_ffn[l]), w_gate[l].astype(BF16), w_up[l].astype(BF16),
                 w_down[l].astype(BF16))

        qb, k32, kb, v32, vb, rc, gates = _proj(xp, nm, w_in_bf, qg, kg, head_ones, tm_p)
        o = _prompt_attn(qb, kb, vb, *lam_vecs, og, b, t, tq, lambda_init)
        rw, st_p, sh_p = _rwkv_prompt(rc.reshape(b, t, R_SHIFT_W), rwkv_params, chunk, chunk_nb)
        xp = _merge_ffn(xp, o, rw.reshape(b * t, R_WIDTH), gates, *ffn_w, tm_p)
        outs[0].append(k32.reshape(b, t, A_HEADS, A_QK_DIM))
        outs[1].append(v32.reshape(b, t, A_HEADS, A_V_DIM))
        outs[2].append(_pairs_to_state(st_p))
        outs[3].append(sh_p.reshape(b, R_SHIFT_W))

        qb, k32, kb, v32, vb, rc, gates = _proj(xs, nm, w_in_bf, qg, kg, head_ones, tm_s)
        o = _decode_attn(page_table, qb, kb, vb, cache_k, cache_v, l, *lam_vecs, og,
                         pps, lambda_init)
        rw, st_s = _rwkv_step(rc, state_shift[l], _state_to_pairs(state_wkv[l]), rwkv_params,
                              step_rows)
        xs = _merge_ffn(xs, o, rw, gates, *ffn_w, tm_s)
        outs[4].append(k32.reshape(db, 1, A_HEADS, A_QK_DIM))
        outs[5].append(v32.reshape(db, 1, A_HEADS, A_V_DIM))
        outs[6].append(_pairs_to_state(st_s))
        outs[7].append(rc)
    return (xp.reshape(b, t, D_MODEL), xs.reshape(db, 1, D_MODEL),
            *(jnp.stack(o) for o in outs))
```

```python
import functools
import math

import jax
import jax.numpy as jnp
from jax import lax
from jax.experimental import pallas as pl
from jax.experimental.pallas import tpu as pltpu

F32 = jnp.float32
BF16 = jnp.bfloat16

D_MODEL = 1024
PAGE_SIZE = 128
A_HEADS = 4
A_HEAD_DIM = 64
A_QK_DIM = 2 * A_HEAD_DIM
A_V_DIM = 2 * A_HEAD_DIM
A_WIDTH = A_HEADS * A_V_DIM
R_HEAD = 64
R_HEADS = 8
R_WIDTH = R_HEADS * R_HEAD
R_LORA_W = 64
R_LORA_A = 64
R_LORA_G = 128
R_SHIFT_W = 3 * R_WIDTH + R_LORA_W + R_LORA_A + R_LORA_G
OFF_Q = 0
OFF_K = OFF_Q + A_HEADS * A_QK_DIM
OFF_V = OFF_K + A_HEADS * A_QK_DIM
OFF_R = OFF_V + A_WIDTH
OFF_G = OFF_R + R_SHIFT_W
IN_WIDTH = OFF_G + 2 * D_MODEL
NORM_EPS = 1e-6
GN_EPS = 64e-5
NEG_BIG = -1e30

V7X_VMEM_LIMIT = 56 * 1024 * 1024
LANES = 128
R_PAIRS = R_WIDTH // LANES
RWKV_CHUNK = 64


def _dot(a, b):
    return jnp.dot(a, b, preferred_element_type=F32)


def _dot_nt(a, b):
    return lax.dot_general(a, b, (((1,), (1,)), ((), ())), preferred_element_type=F32)


def _dot_tn(a, b):
    return lax.dot_general(a, b, (((0,), (0,)), ((), ())), preferred_element_type=F32)


def _split2(x):
    hi = x.astype(BF16)
    lo = (x - hi.astype(F32)).astype(BF16)
    return hi, lo


def _mm(a, b, dot=_dot):
    return dot(a.astype(BF16), b.astype(BF16))


def _sigmoid(x):
    return 1.0 / (1.0 + jnp.exp(-x))


def _iota(shape, dim):
    return lax.broadcasted_iota(jnp.int32, shape, dim)


def _group_sum(z, ones_bf):
    hi, lo = _split2(z)
    return _dot(hi, ones_bf) + _dot(lo, ones_bf)


def _rms_rows(x, gain_row):
    ms = jnp.mean(x * x, axis=-1, keepdims=True)
    return x * lax.rsqrt(ms + NORM_EPS) * gain_row


def _proj_kernel(x_ref, nm_ref, w_ref, qg_ref, kg_ref, ones_ref,
                 q_ref, k_ref, kb_ref, v_ref, vb_ref, r_ref, g_ref):
    h = _rms_rows(x_ref[...], nm_ref[...]).astype(BF16)
    ones = ones_ref[...]

    def head_norm(z, gain_row):
        ss = _group_sum(z * z, ones)
        return z * lax.rsqrt(ss * (1.0 / A_HEAD_DIM) + NORM_EPS) * gain_row

    q = head_norm(_dot(h, w_ref[:, OFF_Q:OFF_K]), qg_ref[...])
    q_ref[...] = (q * (A_HEAD_DIM ** -0.5)).astype(BF16)
    k = head_norm(_dot(h, w_ref[:, OFF_K:OFF_V]), kg_ref[...])
    k_ref[...] = k
    kb_ref[...] = k.astype(BF16)
    v = _dot(h, w_ref[:, OFF_V:OFF_R])
    v_ref[...] = v
    vb_ref[...] = v.astype(BF16)
    step = 512
    for c in range(0, R_SHIFT_W, step):
        n = min(step, R_SHIFT_W - c)
        r_ref[:, c:c + n] = _dot(h, w_ref[:, OFF_R + c:OFF_R + c + n])
    for c in range(0, 2 * D_MODEL, step):
        g_ref[:, c:c + step] = _sigmoid(_dot(h, w_ref[:, OFF_G + c:OFF_G + c + step]))


def _proj(x2, nm, w_in_bf, qg, kg, ones_bf, tm):
    m = x2.shape[0]
    row = lambda i: (i, 0)
    const = lambda i: (0, 0)
    out_shape = (
        jax.ShapeDtypeStruct((m, A_WIDTH), BF16),
        jax.ShapeDtypeStruct((m, A_WIDTH), F32),
        jax.ShapeDtypeStruct((m, A_WIDTH), BF16),
        jax.ShapeDtypeStruct((m, A_WIDTH), F32),
        jax.ShapeDtypeStruct((m, A_WIDTH), BF16),
        jax.ShapeDtypeStruct((m, R_SHIFT_W), F32),
        jax.ShapeDtypeStruct((m, 2 * D_MODEL), F32),
    )
    return pl.pallas_call(
        _proj_kernel,
        grid=(m // tm,),
        in_specs=[
            pl.BlockSpec((tm, D_MODEL), row),
            pl.BlockSpec((1, D_MODEL), const),
            pl.BlockSpec((D_MODEL, IN_WIDTH), const),
            pl.BlockSpec((1, A_WIDTH), const),
            pl.BlockSpec((1, A_WIDTH), const),
            pl.BlockSpec((A_WIDTH, A_WIDTH), const),
        ],
        out_specs=[
            pl.BlockSpec((tm, A_WIDTH), row),
            pl.BlockSpec((tm, A_WIDTH), row),
            pl.BlockSpec((tm, A_WIDTH), row),
            pl.BlockSpec((tm, A_WIDTH), row),
            pl.BlockSpec((tm, A_WIDTH), row),
            pl.BlockSpec((tm, R_SHIFT_W), row),
            pl.BlockSpec((tm, 2 * D_MODEL), row),
        ],
        out_shape=out_shape,
        compiler_params=pltpu.CompilerParams(
            dimension_semantics=("arbitrary",), vmem_limit_bytes=V7X_VMEM_LIMIT),
        name="proj",
    )(x2, nm, w_in_bf, qg, kg, ones_bf)


def _lambda_full(lq1, lk1, lq2, lk2, lambda_init):
    s1 = jnp.sum(lq1 * lk1, axis=-1, keepdims=True)
    s2 = jnp.sum(lq2 * lk2, axis=-1, keepdims=True)
    return jnp.exp(s1) - jnp.exp(s2) + lambda_init


def _softmax_update(s, m, l, acc, vs):
    m_new = jnp.maximum(m, jnp.max(s, axis=-1, keepdims=True))
    alpha = jnp.exp(m - m_new)
    p = jnp.exp(s - m_new)
    l = alpha * l + jnp.sum(p, axis=-1, keepdims=True)
    acc = alpha * acc + _dot(p.astype(BF16), vs)
    return m_new, l, acc


ATTN_ACC_ROWS = A_V_DIM + 16


def _alibi_lanes(pos, slope, keys):
    l4 = _iota(pos.shape, 1) & (A_HEAD_DIM - 1)
    lo = slope * (pos & 255).astype(F32)
    hi = slope * ((pos >> 8) << 8).astype(F32)
    if keys:
        val = jnp.where(l4 < 2, 1.0, jnp.where(l4 == 2, lo, jnp.where(l4 == 3, hi, 0.0)))
    else:
        val = jnp.where(l4 == 0, -lo, jnp.where(l4 == 1, -hi, jnp.where(l4 < 4, 1.0, 0.0)))
    return val.astype(BF16)


def _prompt_attn_kernel(q_ref, k_ref, v_ref, lq1_ref, lk1_ref, lq2_ref, lk2_ref, gain_ref,
                        o_ref, kc_ref, vt_ref, acc_ref, *, tq, lambda_init):
    h = pl.program_id(1)
    i = pl.program_id(2)
    t = k_ref.shape[0]
    slope = jnp.exp2(jnp.full((1, 1), -8.0 / A_HEADS, F32) * (h + 1).astype(F32))
    pos = _iota((tq, A_QK_DIM), 0)
    low = _iota((tq, A_QK_DIM), 1) < A_HEAD_DIM

    @pl.when(i == 0)
    def _():
        ek = _alibi_lanes(pos, slope, keys=True)
        ones_rows = (_iota((ATTN_ACC_ROWS - A_V_DIM, tq), 0) == 0).astype(BF16)
        for jc in range(t // tq):
            rows = slice(jc * tq, (jc + 1) * tq)
            kt = k_ref[rows, :]
            kc_ref[0, rows, :] = jnp.where(low, kt, ek)
            kc_ref[1, rows, :] = jnp.where(low, ek, kt)
            vt_ref[jc, 0:A_V_DIM, :] = v_ref[rows, :].astype(F32).T.astype(BF16)
            vt_ref[jc, A_V_DIM:, :] = ones_rows

    q = q_ref[...]
    eq = _alibi_lanes(pos, slope, keys=False)
    qc = (jnp.where(low, q, eq), jnp.where(low, eq, q))
    kv_pos = _iota((tq, tq), 0)
    q_pos = _iota((tq, tq), 1)

    def tiles(group, ms):
        offs, s = [], []
        for j, masked in group:
            offs.append(slope * ((j - i) * tq).astype(F32))
            rows = pl.ds(pl.multiple_of(j * tq, tq), tq)
            sj = [_dot_nt(kc_ref[c, rows, :], qc[c]) for c in range(2)]
            if masked:
                sj = [jnp.where(kv_pos <= q_pos, sc, NEG_BIG) for sc in sj]
            s.append(sj)
        out = []
        for c in range(2):
            m_new = ms[c]
            for sj, off in zip(s, offs):
                m_new = jnp.maximum(m_new, jnp.max(sj[c], axis=0, keepdims=True) + off)
            pv = None
            for (j, _), sj, off in zip(group, s, offs):
                d = _dot(vt_ref[j], jnp.exp(sj[c] - (m_new - off)).astype(BF16))
                pv = d if pv is None else pv + d
            acc_ref[c] = jnp.exp(ms[c] - m_new) * acc_ref[c] + pv
            out.append(m_new)
        return tuple(out)

    acc_ref[...] = jnp.zeros_like(acc_ref)
    ms = lax.fori_loop(0, i // 2,
                       lambda pr, ms: tiles([(2 * pr, False), (2 * pr + 1, False)], ms),
                       tuple(jnp.full((1, tq), NEG_BIG, F32) for _ in range(2)))
    lax.cond((i & 1) == 1,
             lambda ms: tiles([(i - 1, False), (i, True)], ms),
             lambda ms: tiles([(i, True)], ms), ms)
    a0, a1 = acc_ref[0], acc_ref[1]
    lam = _lambda_full(lq1_ref[...], lk1_ref[...], lq2_ref[...], lk2_ref[...], lambda_init)
    o_t = (a0[0:A_V_DIM] / a0[A_V_DIM:A_V_DIM + 1]
           - lam * (a1[0:A_V_DIM] / a1[A_V_DIM:A_V_DIM + 1]))
    ms = jnp.mean(o_t * o_t, axis=0, keepdims=True)
    o_t = o_t * lax.rsqrt(ms + NORM_EPS) * gain_ref[...] * (1.0 - lambda_init)
    o_ref[...] = o_t.T.astype(BF16)


def _prompt_attn(qb, kb, vb, lq1, lk1, lq2, lk2, gain, b, t, tq, lambda_init):
    nq = t // tq
    vec = lambda bi, h, i: (0, 0)
    return pl.pallas_call(
        functools.partial(_prompt_attn_kernel, tq=tq, lambda_init=lambda_init),
        grid=(b, A_HEADS, nq),
        in_specs=[
            pl.BlockSpec((tq, A_QK_DIM), lambda bi, h, i: (bi * nq + i, h)),
            pl.BlockSpec((t, A_QK_DIM), lambda bi, h, i: (bi, h)),
            pl.BlockSpec((t, A_V_DIM), lambda bi, h, i: (bi, h)),
            pl.BlockSpec((1, A_HEAD_DIM), vec),
            pl.BlockSpec((1, A_HEAD_DIM), vec),
            pl.BlockSpec((1, A_HEAD_DIM), vec),
            pl.BlockSpec((1, A_HEAD_DIM), vec),
            pl.BlockSpec((A_V_DIM, 1), vec),
        ],
        out_specs=pl.BlockSpec((tq, A_V_DIM), lambda bi, h, i: (bi * nq + i, h)),
        out_shape=jax.ShapeDtypeStruct((b * t, A_WIDTH), BF16),
        scratch_shapes=[pltpu.VMEM((2, t, A_QK_DIM), BF16),
                        pltpu.VMEM((nq, ATTN_ACC_ROWS, tq), BF16),
                        pltpu.VMEM((2, ATTN_ACC_ROWS, tq), F32)],
        compiler_params=pltpu.CompilerParams(
            dimension_semantics=("arbitrary", "arbitrary", "arbitrary"),
            vmem_limit_bytes=V7X_VMEM_LIMIT),
        name="prompt_attn",
    )(qb, kb, vb, lq1, lk1, lq2, lk2, gain.reshape(A_V_DIM, 1))


def _decode_attn_kernel(pt_ref, q_ref, kn_ref, vn_ref, lq1_ref, lk1_ref, lq2_ref, lk2_ref,
                        gain_ref, *refs, pps, n_pages, lambda_init):
    k_refs = refs[:pps]
    v_refs = refs[pps:2 * pps]
    o_ref = refs[2 * pps]
    m_ref, l_ref, acc_ref = refs[2 * pps + 1:]
    j = pl.program_id(1)
    rows = 2 * A_HEADS
    page_rows = PAGE_SIZE * A_HEADS
    row_q = _iota((rows, A_QK_DIM), 0)
    lane_q = _iota((rows, A_QK_DIM), 1)
    qm = jnp.where((lane_q >> 6) == (row_q & 1), q_ref[0], 0.0).astype(BF16)
    slope = jnp.exp2((-8.0 / A_HEADS) * ((_iota((rows, 1), 0) >> 1) + 1).astype(F32))
    col = _iota((rows, pps * page_rows), 1)
    own_head = (col & (A_HEADS - 1)) == (_iota((rows, pps * page_rows), 0) >> 1)
    dist = (n_pages - j * pps) * PAGE_SIZE - (col >> 2)

    @pl.when(j == 0)
    def _():
        m_ref[...] = jnp.full((rows, 1), NEG_BIG, F32)
        l_ref[...] = jnp.zeros((rows, 1), F32)
        acc_ref[...] = jnp.zeros((rows, A_V_DIM), F32)

    s = jnp.concatenate([_dot_nt(qm, k_refs[u][...].astype(BF16)) for u in range(pps)], axis=1)
    s = jnp.where(own_head, s - slope * dist.astype(F32), NEG_BIG)
    m = m_ref[...]
    m_new = jnp.maximum(m, jnp.max(s, axis=-1, keepdims=True))
    alpha = jnp.exp(m - m_new)
    p = jnp.exp(s - m_new)
    pb = p.astype(BF16)
    pv = _dot(pb[:, 0:page_rows], v_refs[0][...].astype(BF16))
    for u in range(1, pps):
        pv = pv + _dot(pb[:, u * page_rows:(u + 1) * page_rows], v_refs[u][...].astype(BF16))
    m = m_new
    l = alpha * l_ref[...] + jnp.sum(p, axis=-1, keepdims=True)
    acc = alpha * acc_ref[...] + pv
    m_ref[...], l_ref[...], acc_ref[...] = m, l, acc

    @pl.when(j == pl.num_programs(1) - 1)
    def _():
        s_new = jnp.sum(qm.astype(F32) * kn_ref[0], axis=-1, keepdims=True)
        m_new = jnp.maximum(m, s_new)
        alpha = jnp.exp(m - m_new)
        p_new = jnp.exp(s_new - m_new)
        l_fin = alpha * l + p_new
        acc_fin = alpha * acc + p_new.astype(BF16).astype(F32) * vn_ref[0]
        lam = _lambda_full(lq1_ref[...], lk1_ref[...], lq2_ref[...], lk2_ref[...], lambda_init)
        comp = _iota((rows, 1), 0) & 1
        x = acc_fin * (jnp.where(comp == 0, 1.0, -lam) / l_fin)
        o = x + pltpu.roll(x, shift=rows - 1, axis=0)
        o_ref[0] = _rms_rows(o, gain_ref[...]) * (1.0 - lambda_init)


def _decode_attn(page_table, qb, kb, vb, cache_k, cache_v, layer, lq1, lk1, lq2, lk2, gain,
                 pps, lambda_init):
    db, n_pages = page_table.shape
    n_pool = cache_k.shape[1]
    page_rows = PAGE_SIZE * A_HEADS
    ck = cache_k.reshape(-1, A_QK_DIM)
    cv = cache_v.reshape(-1, A_V_DIM)
    pt = page_table.reshape(db * n_pages)
    rows = 2 * A_HEADS
    tok3 = lambda b, j, pt_ref: (b, 0, 0)
    vec = lambda b, j, pt_ref: (0, 0)

    def page_spec(u):
        return pl.BlockSpec(
            (page_rows, A_QK_DIM),
            lambda b, j, pt_ref: (layer * n_pool + pt_ref[b * n_pages + j * pps + u], 0))

    def per_row(z):
        return jnp.repeat(z.astype(F32).reshape(db, A_HEADS, A_QK_DIM), 2, axis=1)

    grid_spec = pltpu.PrefetchScalarGridSpec(
        num_scalar_prefetch=1,
        grid=(db, n_pages // pps),
        in_specs=[
            pl.BlockSpec((1, rows, A_QK_DIM), tok3),
            pl.BlockSpec((1, rows, A_QK_DIM), tok3),
            pl.BlockSpec((1, rows, A_V_DIM), tok3),
            pl.BlockSpec((1, A_HEAD_DIM), vec),
            pl.BlockSpec((1, A_HEAD_DIM), vec),
            pl.BlockSpec((1, A_HEAD_DIM), vec),
            pl.BlockSpec((1, A_HEAD_DIM), vec),
            pl.BlockSpec((1, A_V_DIM), vec),
        ] + [page_spec(u) for u in range(pps)] + [page_spec(u) for u in range(pps)],
        out_specs=pl.BlockSpec((1, rows, A_V_DIM), tok3),
        scratch_shapes=[pltpu.VMEM((rows, 1), F32), pltpu.VMEM((rows, 1), F32),
                        pltpu.VMEM((rows, A_V_DIM), F32)],
    )
    out = pl.pallas_call(
        functools.partial(_decode_attn_kernel, pps=pps, n_pages=n_pages, lambda_init=lambda_init),
        grid_spec=grid_spec,
        out_shape=jax.ShapeDtypeStruct((db, rows, A_V_DIM), F32),
        compiler_params=pltpu.CompilerParams(
            dimension_semantics=("arbitrary", "arbitrary"), vmem_limit_bytes=V7X_VMEM_LIMIT),
        name="decode_attn",
    )(pt, per_row(qb), per_row(kb), per_row(vb),
      lq1, lk1, lq2, lk2, gain, *([ck] * pps), *([cv] * pps))
    return out[:, 0::2, :].reshape(db, A_WIDTH)


def _rwkv_prep(x, prev, mu, w0, w2e, a0, a2e, g2, k_k, k_a, r_k, ones):
    m = x + (prev - x) * mu
    r = m[:, 0:R_WIDTH]
    k = m[:, R_WIDTH:2 * R_WIDTH]
    v = m[:, 2 * R_WIDTH:3 * R_WIDTH]
    xwa = m[:, 3 * R_WIDTH:3 * R_WIDTH + R_LORA_W + R_LORA_A]
    xg = m[:, 3 * R_WIDTH + R_LORA_W + R_LORA_A:]
    z = w0 + _dot(jnp.tanh(xwa).astype(BF16), w2e)
    lw = (-math.exp(-0.5)) * _sigmoid(z)
    a = _sigmoid(a0 + _dot(xwa.astype(BF16), a2e))
    g = _dot(_sigmoid(xg).astype(BF16), g2)
    kk = k * k_k
    kk = kk * lax.rsqrt(jnp.maximum(_group_sum(kk * kk, ones), 1e-24))
    k = k * (1.0 + (a - 1.0) * k_a)
    bonus = _group_sum(r * k * r_k, ones) * v
    return r, lw, k, v, kk, a, g, bonus


def _rwkv_finish(y, bonus, g, gn_w, gn_b, ones):
    mu = _group_sum(y, ones) * (1.0 / R_HEAD)
    d = y - mu
    var = _group_sum(d * d, ones) * (1.0 / R_HEAD)
    yn = d * lax.rsqrt(var + GN_EPS) * gn_w + gn_b
    return ((yn + bonus) * g).astype(BF16)


def _lane_halves(x, width):
    lane = _iota(x.shape, 1)
    zero = jnp.zeros_like(x)
    return jnp.concatenate([jnp.where(lane < width, x, zero), jnp.where(lane >= width, x, zero)],
                           axis=0)


def _rwkv_chunk_kernel(x_ref, mu_ref, w0_ref, w2e_ref, a0_ref, a2e_ref, g2_ref, kk_ref, ka_ref,
                       rk_ref, gnw_ref, gnb_ref, ones_ref,
                       y_ref, st_ref, sh_ref, prev_ref, *, chunk, nb):
    c = chunk
    j = pl.program_id(1)

    @pl.when(j == 0)
    def _():
        prev_ref[...] = jnp.zeros_like(prev_ref)
        st_ref[...] = jnp.zeros_like(st_ref)

    ones = ones_ref[...]
    rows = _iota((c, R_SHIFT_W), 0)
    tri = (_iota((c, c), 0) >= _iota((c, c), 1)).astype(BF16)
    sls = [slice(p * LANES, (p + 1) * LANES) for p in range(R_PAIRS)]
    ah_p, rh_p, v_p, bt_p, kt_p, bb_p, kb_p, pe_p, st_idx, fin = ([] for _ in range(10))
    for bi in range(nb):
        x = x_ref[bi]
        prev = jnp.where(rows == 0, prev_ref[bi], pltpu.roll(x, shift=1, axis=0))
        prev_ref[bi] = x[c - 1:c, :]
        sh_ref[bi] = x[c - 1:c, :]
        r, lw, k, v, kk, a, g, bonus = _rwkv_prep(
            x, prev, mu_ref[...], w0_ref[...], w2e_ref[...], a0_ref[...], a2e_ref[...],
            g2_ref[...], kk_ref[...], ka_ref[...], rk_ref[...], ones)
        fin.append((bonus, g))
        l_hi = lw.astype(BF16)
        l_r1 = lw - l_hi.astype(F32)
        l_mid = l_r1.astype(BF16)
        l_lo = (l_r1 - l_mid.astype(F32)).astype(BF16)
        cum = _dot(tri, l_hi) + _dot(tri, l_mid) + _dot(tri, l_lo)
        cum_end = cum[c - 1:c, :]
        e_neg = jnp.exp(-cum)
        e_end = jnp.exp(cum_end - cum)
        kka = kk * a
        ah = -kk * jnp.exp(cum - lw)
        rh = r * jnp.exp(cum)
        bt = kka * e_neg
        kt = k * e_neg
        bb = kka * e_end
        kb = k * e_end
        p_end = jnp.exp(cum_end)
        for p, sl in enumerate(sls):
            for lst, z in ((ah_p, ah), (rh_p, rh), (v_p, v), (bt_p, bt), (kt_p, kt),
                           (bb_p, bb), (kb_p, kb), (pe_p, p_end)):
                lst.append(z[:, sl])
            st_idx.append((bi, p))

    t_idx = _iota((c, 2 * c), 0)
    s_idx = _iota((c, 2 * c), 1) & (c - 1)
    strict = t_idx > s_idx
    incl = t_idx >= s_idx
    eye_pack = (t_idx == s_idx).astype(F32)
    bd_mask = (_iota((LANES, LANES), 0) >> 6) == (_iota((LANES, LANES), 1) >> 6)
    eye_l = _iota((LANES, LANES), 0) == _iota((LANES, LANES), 1)
    zeros_c = jnp.zeros((c, LANES), F32)

    units = range(len(st_idx))
    v_blk = [_lane_halves(z, R_HEAD) for z in v_p]
    a_all = [_mm(jnp.concatenate([ah_p[u], rh_p[u]], axis=0),
                 jnp.concatenate([_lane_halves(bt_p[u], R_HEAD), _lane_halves(kt_p[u], R_HEAD)],
                                 axis=0), _dot_nt) for u in units]
    a_ab = [jnp.where(strict, z[0:c, 0:2 * c], 0.0) for z in a_all]
    a_ak = [jnp.where(strict, z[0:c, 2 * c:4 * c], 0.0) for z in a_all]
    a_rb = [jnp.where(incl, z[c:2 * c, 0:2 * c], 0.0) for z in a_all]
    a_rk = [jnp.where(incl, z[c:2 * c, 2 * c:4 * c], 0.0) for z in a_all]
    w1 = [_mm(a_ak[u], v_blk[u]) for u in units]
    inv = [eye_pack + z for z in a_ab]
    bd = [_lane_halves(z, c) for z in a_ab]
    pw = [_mm(a_ab[u], bd[u]) for u in units]
    n_sq = int(math.log2(c)) - 1
    for step in range(n_sq):
        bd = [_lane_halves(z, c) for z in pw]
        if step < n_sq - 1:
            both = [_mm(jnp.concatenate([pw[u], inv[u]], axis=0), bd[u]) for u in units]
            pw = [z[0:c] for z in both]
            inv = [inv[u] + both[u][c:2 * c] for u in units]
        else:
            inv = [inv[u] + _mm(inv[u], bd[u]) for u in units]
    t4 = [_mm(inv[u], jnp.concatenate([_lane_halves(w1[u], R_HEAD),
                                       _lane_halves(ah_p[u], R_HEAD)], axis=1)) for u in units]
    wu = [z[:, 0:LANES] for z in t4]
    ah2 = [z[:, LANES:] for z in t4]
    zeros_blk = jnp.zeros((2 * c, LANES), F32)
    t5 = [_mm(jnp.concatenate([a_rb[u], a_rk[u]], axis=1),
              jnp.concatenate([
                  jnp.concatenate([_lane_halves(wu[u], R_HEAD), _lane_halves(ah2[u], R_HEAD)],
                                  axis=1),
                  jnp.concatenate([v_blk[u], zeros_blk], axis=1)], axis=0)) for u in units]
    t6 = [_mm(jnp.concatenate([bb_p[u], kb_p[u]], axis=0),
              jnp.concatenate([jnp.concatenate([wu[u], ah2[u]], axis=1),
                               jnp.concatenate([v_p[u], zeros_c], axis=1)], axis=0), _dot_tn)
          for u in units]
    st = [st_ref[bi, p] for bi, p in st_idx]
    ys = [_mm(rh_p[u] + t5[u][:, LANES:], st[u]) + t5[u][:, 0:LANES] for u in units]
    for u in units:
        h_new = jnp.where(bd_mask, t6[u][:, 0:LANES], 0.0)
        g_x = jnp.where(bd_mask, t6[u][:, LANES:], 0.0)
        p_col = jnp.sum(jnp.where(eye_l, pe_p[u], 0.0), axis=1, keepdims=True)
        st_ref[st_idx[u]] = p_col * st[u] + _mm(g_x, st[u]) + h_new

    for bi in range(nb):
        y = jnp.concatenate(ys[bi * R_PAIRS:(bi + 1) * R_PAIRS], axis=1)
        bonus, g = fin[bi]
        y_ref[bi] = _rwkv_finish(y, bonus, g, gnw_ref[...], gnb_ref[...], ones)


def _rwkv_prompt(rc3, params, chunk, nb):
    b, t, _ = rc3.shape
    const = lambda bi, j: (0, 0)
    specs = [pl.BlockSpec(p.shape, const) for p in params]
    return pl.pallas_call(
        functools.partial(_rwkv_chunk_kernel, chunk=chunk, nb=nb),
        grid=(b // nb, t // chunk),
        in_specs=[pl.BlockSpec((nb, chunk, R_SHIFT_W), lambda bi, j: (bi, j, 0))] + specs,
        out_specs=[
            pl.BlockSpec((nb, chunk, R_WIDTH), lambda bi, j: (bi, j, 0)),
            pl.BlockSpec((nb, R_PAIRS, LANES, LANES), lambda bi, j: (bi, 0, 0, 0)),
            pl.BlockSpec((nb, 1, R_SHIFT_W), lambda bi, j: (bi, 0, 0)),
        ],
        out_shape=(
            jax.ShapeDtypeStruct((b, t, R_WIDTH), BF16),
            jax.ShapeDtypeStruct((b, R_PAIRS, LANES, LANES), F32),
            jax.ShapeDtypeStruct((b, 1, R_SHIFT_W), F32),
        ),
        scratch_shapes=[pltpu.VMEM((nb, 1, R_SHIFT_W), F32)],
        compiler_params=pltpu.CompilerParams(
            dimension_semantics=("arbitrary", "arbitrary"), vmem_limit_bytes=V7X_VMEM_LIMIT),
        name="rwkv_chunk",
    )(rc3, *params)


def _rwkv_step_kernel(x_ref, prev_ref, st_in_ref, mu_ref, w0_ref, w2e_ref, a0_ref, a2e_ref,
                      g2_ref, kk_ref, ka_ref, rk_ref, gnw_ref, gnb_ref, ones_ref,
                      y_ref, st_ref, yrow_ref, *, rows):
    ones = ones_ref[...]
    r, lw, k, v, kk, a, g, bonus = _rwkv_prep(
        x_ref[...], prev_ref[...], mu_ref[...], w0_ref[...], w2e_ref[...], a0_ref[...],
        a2e_ref[...], g2_ref[...], kk_ref[...], ka_ref[...], rk_ref[...], ones)
    vecs = (jnp.exp(lw), kk * a, k, v, kk, r)
    bd_mask = (_iota((LANES, LANES), 0) >> 6) == (_iota((LANES, LANES), 1) >> 6)
    eye_l = _iota((LANES, LANES), 0) == _iota((LANES, LANES), 1)

    def col(row_vec):
        return jnp.sum(jnp.where(eye_l, row_vec, 0.0), axis=1, keepdims=True)

    for bi in range(rows):
        for p in range(R_PAIRS):
            sl = slice(p * LANES, (p + 1) * LANES)
            w_r, b_r, k_r, v_r, kk_r, r_r = (z[bi:bi + 1, sl] for z in vecs)
            st = st_in_ref[bi, p]
            sa = -jnp.sum(col(kk_r) * st, axis=0, keepdims=True)
            upd = jnp.where(bd_mask, col(b_r) * sa + col(k_r) * v_r, 0.0)
            st_new = col(w_r) * st + upd
            st_ref[bi, p] = st_new
            yrow_ref[bi:bi + 1, sl] = jnp.sum(col(r_r) * st_new, axis=0, keepdims=True)
    y_ref[...] = _rwkv_finish(yrow_ref[...], bonus, g, gnw_ref[...], gnb_ref[...], ones)


def _rwkv_step(rc, shift_prev, st_pairs, params, rows):
    db = rc.shape[0]
    const = lambda i: (0, 0)
    specs = [pl.BlockSpec(p.shape, const) for p in params]
    return pl.pallas_call(
        functools.partial(_rwkv_step_kernel, rows=rows),
        grid=(db // rows,),
        in_specs=[
            pl.BlockSpec((rows, R_SHIFT_W), lambda i: (i, 0)),
            pl.BlockSpec((rows, R_SHIFT_W), lambda i: (i, 0)),
            pl.BlockSpec((rows, R_PAIRS, LANES, LANES), lambda i: (i, 0, 0, 0)),
        ] + specs,
        out_specs=[
            pl.BlockSpec((rows, R_WIDTH), lambda i: (i, 0)),
            pl.BlockSpec((rows, R_PAIRS, LANES, LANES), lambda i: (i, 0, 0, 0)),
        ],
        out_shape=(
            jax.ShapeDtypeStruct((db, R_WIDTH), BF16),
            jax.ShapeDtypeStruct((db, R_PAIRS, LANES, LANES), F32),
        ),
        scratch_shapes=[pltpu.VMEM((rows, R_WIDTH), F32)],
        compiler_params=pltpu.CompilerParams(
            dimension_semantics=("arbitrary",), vmem_limit_bytes=V7X_VMEM_LIMIT),
        name="rwkv_step",
    )(rc, shift_prev, st_pairs, *params)


def _state_to_pairs(s):
    b = s.shape[0]
    st = jnp.swapaxes(s, -1, -2).reshape(b, R_PAIRS, 2, R_HEAD, R_HEAD)
    eye2 = jnp.eye(2, dtype=s.dtype)
    return jnp.einsum("bpgij,gh->bpgihj", st, eye2).reshape(b, R_PAIRS, LANES, LANES)


def _pairs_to_state(st):
    b = st.shape[0]
    blocks = st.reshape(b, R_PAIRS, 2, R_HEAD, 2, R_HEAD)
    diag = jnp.stack([blocks[:, :, 0, :, 0, :], blocks[:, :, 1, :, 1, :]], axis=2)
    return jnp.swapaxes(diag.reshape(b, R_HEADS, R_HEAD, R_HEAD), -1, -2)


def _merge_ffn_kernel(x_ref, o_ref, rw_ref, g_ref, wpa_ref, wpb_ref, wout_ref, nf_ref,
                      wg_ref, wu_ref, wd_ref, y_ref, *, d_ff):
    ya = _dot(o_ref[...].astype(BF16), wpa_ref[...])
    yb = _dot(rw_ref[...], wpb_ref[...])
    merged = (g_ref[:, 0:D_MODEL] * ya + g_ref[:, D_MODEL:] * yb).astype(BF16)
    x1 = x_ref[...] + _dot(merged, wout_ref[...])
    hf = _rms_rows(x1, nf_ref[...]).astype(BF16)
    acc = x1
    step = 512
    for c in range(0, d_ff, step):
        n = min(step, d_ff - c)
        gate = _dot(hf, wg_ref[:, c:c + n])
        up = _dot(hf, wu_ref[:, c:c + n])
        act = (gate * _sigmoid(gate) * up).astype(BF16)
        acc = acc + _dot(act, wd_ref[c:c + n, :])
    y_ref[...] = acc


def _merge_ffn(x2, o, rw, gates, wpa, wpb, wout, nf, wg, wu, wd, tm):
    m = x2.shape[0]
    d_ff = wg.shape[1]
    row = lambda i: (i, 0)
    const = lambda i: (0, 0)
    return pl.pallas_call(
        functools.partial(_merge_ffn_kernel, d_ff=d_ff),
        grid=(m // tm,),
        in_specs=[
            pl.BlockSpec((tm, D_MODEL), row),
            pl.BlockSpec((tm, A_WIDTH), row),
            pl.BlockSpec((tm, R_WIDTH), row),
            pl.BlockSpec((tm, 2 * D_MODEL), row),
            pl.BlockSpec(wpa.shape, const),
            pl.BlockSpec(wpb.shape, const),
            pl.BlockSpec(wout.shape, const),
            pl.BlockSpec((1, D_MODEL), const),
            pl.BlockSpec(wg.shape, const),
            pl.BlockSpec(wu.shape, const),
            pl.BlockSpec(wd.shape, const),
        ],
        out_specs=pl.BlockSpec((tm, D_MODEL), row),
        out_shape=jax.ShapeDtypeStruct((m, D_MODEL), F32),
        compiler_params=pltpu.CompilerParams(
            dimension_semantics=("arbitrary",), vmem_limit_bytes=V7X_VMEM_LIMIT),
        name="merge_ffn",
    )(x2, o, rw, gates, wpa, wpb, wout, nf, wg, wu, wd)


def _row(v):
    return v.reshape(1, -1).astype(F32)


def _pick_tile(m, target):
    t = min(m, target)
    while m % t:
        t //= 2
    return t


def kernel(x_prompt, x_sample, cache_k, cache_v, page_table, state_wkv, state_shift, norm_mix, w_in, q_gain, k_gain, lambda_q1, lambda_k1, lambda_q2, lambda_k2, attn_out_gain, w_pa, shift_mu, w0, w2, a0, a2, g2, k_k, k_a, r_k, gn_w, gn_b, w_pb, w_out, norm_ffn, w_gate, w_up, w_down):
    b, t, _ = x_prompt.shape
    db, ds, _ = x_sample.shape
    assert ds == 1, "sample group carries one new token per sequence"
    depth = w_in.shape[0]
    n_pages = page_table.shape[1]
    xp = x_prompt.reshape(b * t, D_MODEL)
    xs = x_sample.reshape(db, D_MODEL)
    head_ones = (jnp.arange(A_WIDTH)[:, None] // A_HEAD_DIM
                 == jnp.arange(A_WIDTH)[None, :] // A_HEAD_DIM).astype(BF16)
    outs = [[] for _ in range(8)]
    tm_p = _pick_tile(b * t, 512)
    tm_s = _pick_tile(db, 512)
    tq = _pick_tile(t, 512)
    chunk = _pick_tile(t, RWKV_CHUNK)
    chunk_nb = _pick_tile(b, 4)
    pps = _pick_tile(n_pages, 16)
    step_rows = _pick_tile(db, 16)
    for l in range(depth):
        lambda_init = 0.8 - 0.6 * math.exp(-0.3 * l)
        w_in_bf = w_in[l].astype(BF16)
        qg = _row(jnp.tile(q_gain[l], A_WIDTH // A_HEAD_DIM))
        kg = _row(jnp.tile(k_gain[l], A_WIDTH // A_HEAD_DIM))
        nm = _row(norm_mix[l])
        lam_vecs = (_row(lambda_q1[l]), _row(lambda_k1[l]), _row(lambda_q2[l]), _row(lambda_k2[l]))
        og = _row(attn_out_gain[l])
        zeros_lora = jnp.zeros((R_LORA_W, R_WIDTH), F32)
        rwkv_params = (
            _row(shift_mu[l]), _row(w0[l]),
            jnp.concatenate([w2[l], zeros_lora], axis=0).astype(BF16),
            _row(a0[l]),
            jnp.concatenate([zeros_lora, a2[l]], axis=0).astype(BF16),
            g2[l].astype(BF16), _row(k_k[l]), _row(k_a[l]), _row(r_k[l]),
            _row(gn_w[l]), _row(gn_b[l]), head_ones)
        ffn_w = (w_pa[l].astype(BF16), w_pb[l].astype(BF16), w_out[l].astype(BF16),
                 _row(norm_ffn[l]), w_gate[l].astype(BF16), w_up[l].astype(BF16),
                 w_down[l].astype(BF16))

        qb, k32, kb, v32, vb, rc, gates = _proj(xp, nm, w_in_bf, qg, kg, head_ones, tm_p)
        o = _prompt_attn(qb, kb, vb, *lam_vecs, og, b, t, tq, lambda_init)
        rw, st_p, sh_p = _rwkv_prompt(rc.reshape(b, t, R_SHIFT_W), rwkv_params, chunk, chunk_nb)
        xp = _merge_ffn(xp, o, rw.reshape(b * t, R_WIDTH), gates, *ffn_w, tm_p)
        outs[0].append(k32.reshape(b, t, A_HEADS, A_QK_DIM))
        outs[1].append(v32.reshape(b, t, A_HEADS, A_V_DIM))
        outs[2].append(_pairs_to_state(st_p))
        outs[3].append(sh_p.reshape(b, R_SHIFT_W))

        qb, k32, kb, v32, vb, rc, gates = _proj(xs, nm, w_in_bf, qg, kg, head_ones, tm_s)
        o = _decode_attn(page_table, qb, kb, vb, cache_k, cache_v, l, *lam_vecs, og,
                         pps, lambda_init)
        rw, st_s = _rwkv_step(rc, state_shift[l], _state_to_pairs(state_wkv[l]), rwkv_params,
                              step_rows)
        xs = _merge_ffn(xs, o, rw, gates, *ffn_w, tm_s)
        outs[4].append(k32.reshape(db, 1, A_HEADS, A_QK_DIM))
        outs[5].append(v32.reshape(db, 1, A_HEADS, A_V_DIM))
        outs[6].append(_pairs_to_state(st_s))
        outs[7].append(rc)
    return (xp.reshape(b, t, D_MODEL), xs.reshape(db, 1, D_MODEL),
            *(jnp.stack(o) for o in outs))
```

```python
import functools
import math

import jax
import jax.numpy as jnp
from jax import lax
from jax.experimental import pallas as pl
from jax.experimental.pallas import tpu as pltpu

F32 = jnp.float32
BF16 = jnp.bfloat16

D_MODEL = 1024
PAGE_SIZE = 128
A_HEADS = 4
A_HEAD_DIM = 64
A_QK_DIM = 2 * A_HEAD_DIM
A_V_DIM = 2 * A_HEAD_DIM
A_WIDTH = A_HEADS * A_V_DIM
R_HEAD = 64
R_HEADS = 8
R_WIDTH = R_HEADS * R_HEAD
R_LORA_W = 64
R_LORA_A = 64
R_LORA_G = 128
R_SHIFT_W = 3 * R_WIDTH + R_LORA_W + R_LORA_A + R_LORA_G
OFF_Q = 0
OFF_K = OFF_Q + A_HEADS * A_QK_DIM
OFF_V = OFF_K + A_HEADS * A_QK_DIM
OFF_R = OFF_V + A_WIDTH
OFF_G = OFF_R + R_SHIFT_W
IN_WIDTH = OFF_G + 2 * D_MODEL
NORM_EPS = 1e-6
GN_EPS = 64e-5
NEG_BIG = -1e30

V7X_VMEM_LIMIT = 56 * 1024 * 1024
LANES = 128
R_PAIRS = R_WIDTH // LANES
RWKV_CHUNK = 64


def _dot(a, b):
    return jnp.dot(a, b, preferred_element_type=F32)


def _dot_nt(a, b):
    return lax.dot_general(a, b, (((1,), (1,)), ((), ())), preferred_element_type=F32)


def _dot_tn(a, b):
    return lax.dot_general(a, b, (((0,), (0,)), ((), ())), preferred_element_type=F32)


def _split2(x):
    hi = x.astype(BF16)
    lo = (x - hi.astype(F32)).astype(BF16)
    return hi, lo


def _mm(a, b, dot=_dot):
    return dot(a.astype(BF16), b.astype(BF16))


def _sigmoid(x):
    return 1.0 / (1.0 + jnp.exp(-x))


def _iota(shape, dim):
    return lax.broadcasted_iota(jnp.int32, shape, dim)


def _group_sum(z, ones_bf):
    return _dot(z.astype(BF16), ones_bf)


def _rms_rows(x, gain_row):
    ms = jnp.mean(x * x, axis=-1, keepdims=True)
    return x * lax.rsqrt(ms + NORM_EPS) * gain_row


def _proj_kernel(x_ref, nm_ref, w_ref, qg_ref, kg_ref, ones_ref,
                 q_ref, k_ref, kb_ref, v_ref, vb_ref, r_ref, g_ref):
    h = _rms_rows(x_ref[...], nm_ref[...]).astype(BF16)
    ones = ones_ref[...]

    def head_norm(z, gain_row):
        ss = _group_sum(z * z, ones)
        return z * lax.rsqrt(ss * (1.0 / A_HEAD_DIM) + NORM_EPS) * gain_row

    q = head_norm(_dot(h, w_ref[:, OFF_Q:OFF_K]), qg_ref[...])
    q_ref[...] = (q * (A_HEAD_DIM ** -0.5)).astype(BF16)
    tm = x_ref.shape[0]
    k = head_norm(_dot(h, w_ref[:, OFF_K:OFF_V]), kg_ref[...])
    kb_ref[...] = k.astype(BF16)
    v = _dot(h, w_ref[:, OFF_V:OFF_R])
    vb_ref[...] = v.astype(BF16)
    for hd in range(A_HEADS):
        sl = slice(hd * A_QK_DIM, (hd + 1) * A_QK_DIM)
        k_ref[pl.ds(hd, tm, stride=A_HEADS), :] = k[:, sl]
        v_ref[pl.ds(hd, tm, stride=A_HEADS), :] = v[:, sl]
    step = 512
    for c in range(0, R_SHIFT_W, step):
        n = min(step, R_SHIFT_W - c)
        r_ref[:, c:c + n] = _dot(h, w_ref[:, OFF_R + c:OFF_R + c + n])
    for c in range(0, 2 * D_MODEL, step):
        g_ref[:, c:c + step] = _sigmoid(_dot(h, w_ref[:, OFF_G + c:OFF_G + c + step]))


def _proj(x2, nm, w_in_bf, qg, kg, ones_bf, tm):
    m = x2.shape[0]
    row = lambda i: (i, 0)
    const = lambda i: (0, 0)
    out_shape = (
        jax.ShapeDtypeStruct((m, A_WIDTH), BF16),
        jax.ShapeDtypeStruct((m * A_HEADS, A_QK_DIM), F32),
        jax.ShapeDtypeStruct((m, A_WIDTH), BF16),
        jax.ShapeDtypeStruct((m * A_HEADS, A_V_DIM), F32),
        jax.ShapeDtypeStruct((m, A_WIDTH), BF16),
        jax.ShapeDtypeStruct((m, R_SHIFT_W), F32),
        jax.ShapeDtypeStruct((m, 2 * D_MODEL), F32),
    )
    return pl.pallas_call(
        _proj_kernel,
        grid=(m // tm,),
        in_specs=[
            pl.BlockSpec((tm, D_MODEL), row),
            pl.BlockSpec((1, D_MODEL), const),
            pl.BlockSpec((D_MODEL, IN_WIDTH), const),
            pl.BlockSpec((1, A_WIDTH), const),
            pl.BlockSpec((1, A_WIDTH), const),
            pl.BlockSpec((A_WIDTH, A_WIDTH), const),
        ],
        out_specs=[
            pl.BlockSpec((tm, A_WIDTH), row),
            pl.BlockSpec((tm * A_HEADS, A_QK_DIM), row),
            pl.BlockSpec((tm, A_WIDTH), row),
            pl.BlockSpec((tm * A_HEADS, A_V_DIM), row),
            pl.BlockSpec((tm, A_WIDTH), row),
            pl.BlockSpec((tm, R_SHIFT_W), row),
            pl.BlockSpec((tm, 2 * D_MODEL), row),
        ],
        out_shape=out_shape,
        compiler_params=pltpu.CompilerParams(
            dimension_semantics=("arbitrary",), vmem_limit_bytes=V7X_VMEM_LIMIT),
        name="proj",
    )(x2, nm, w_in_bf, qg, kg, ones_bf)


def _lambda_full(lq1, lk1, lq2, lk2, lambda_init):
    s1 = jnp.sum(lq1 * lk1, axis=-1, keepdims=True)
    s2 = jnp.sum(lq2 * lk2, axis=-1, keepdims=True)
    return jnp.exp(s1) - jnp.exp(s2) + lambda_init


def _softmax_update(s, m, l, acc, vs):
    m_new = jnp.maximum(m, jnp.max(s, axis=-1, keepdims=True))
    alpha = jnp.exp(m - m_new)
    p = jnp.exp(s - m_new)
    l = alpha * l + jnp.sum(p, axis=-1, keepdims=True)
    acc = alpha * acc + _dot(p.astype(BF16), vs)
    return m_new, l, acc


ATTN_ACC_ROWS = A_V_DIM + 16


def _alibi_lanes(pos, slope, keys):
    l4 = _iota(pos.shape, 1) & (A_HEAD_DIM - 1)
    lo = slope * (pos & 255).astype(F32)
    hi = slope * ((pos >> 8) << 8).astype(F32)
    if keys:
        val = jnp.where(l4 < 2, 1.0, jnp.where(l4 == 2, lo, jnp.where(l4 == 3, hi, 0.0)))
    else:
        val = jnp.where(l4 == 0, -lo, jnp.where(l4 == 1, -hi, jnp.where(l4 < 4, 1.0, 0.0)))
    return val.astype(BF16)


def _prompt_attn_kernel(q_ref, k_ref, v_ref, lq1_ref, lk1_ref, lq2_ref, lk2_ref, gain_ref,
                        o_ref, kc_ref, vt_ref, acc_ref, *, tq, lambda_init):
    h = pl.program_id(1)
    i = pl.program_id(2)
    t = k_ref.shape[0]
    slope = jnp.exp2(jnp.full((1, 1), -8.0 / A_HEADS, F32) * (h + 1).astype(F32))
    pos = _iota((tq, A_QK_DIM), 0)
    low = _iota((tq, A_QK_DIM), 1) < A_HEAD_DIM

    @pl.when(i == 0)
    def _():
        ek = _alibi_lanes(pos, slope, keys=True)
        ones_rows = (_iota((ATTN_ACC_ROWS - A_V_DIM, tq), 0) == 0).astype(BF16)
        for jc in range(t // tq):
            rows = slice(jc * tq, (jc + 1) * tq)
            kt = k_ref[rows, :]
            kc_ref[0, rows, :] = jnp.where(low, kt, ek)
            kc_ref[1, rows, :] = jnp.where(low, ek, kt)
            vt_ref[jc, 0:A_V_DIM, :] = v_ref[rows, :].astype(F32).T.astype(BF16)
            vt_ref[jc, A_V_DIM:, :] = ones_rows

    q = q_ref[...]
    eq = _alibi_lanes(pos, slope, keys=False)
    qc = (jnp.where(low, q, eq), jnp.where(low, eq, q))
    kv_pos = _iota((tq, tq), 0)
    q_pos = _iota((tq, tq), 1)

    def tiles(group, ms):
        offs, s = [], []
        for j, masked in group:
            offs.append(slope * ((j - i) * tq).astype(F32))
            rows = pl.ds(pl.multiple_of(j * tq, tq), tq)
            sj = [_dot_nt(kc_ref[c, rows, :], qc[c]) for c in range(2)]
            if masked:
                sj = [jnp.where(kv_pos <= q_pos, sc, NEG_BIG) for sc in sj]
            s.append(sj)
        out = []
        for c in range(2):
            m_new = ms[c]
            for sj, off in zip(s, offs):
                m_new = jnp.maximum(m_new, jnp.max(sj[c], axis=0, keepdims=True) + off)
            pv = None
            for (j, _), sj, off in zip(group, s, offs):
                d = _dot(vt_ref[j], jnp.exp(sj[c] - (m_new - off)).astype(BF16))
                pv = d if pv is None else pv + d
            acc_ref[c] = jnp.exp(ms[c] - m_new) * acc_ref[c] + pv
            out.append(m_new)
        return tuple(out)

    acc_ref[...] = jnp.zeros_like(acc_ref)
    ms = lax.fori_loop(0, i // 2,
                       lambda pr, ms: tiles([(2 * pr, False), (2 * pr + 1, False)], ms),
                       tuple(jnp.full((1, tq), NEG_BIG, F32) for _ in range(2)))
    lax.cond((i & 1) == 1,
             lambda ms: tiles([(i - 1, False), (i, True)], ms),
             lambda ms: tiles([(i, True)], ms), ms)
    a0, a1 = acc_ref[0], acc_ref[1]
    lam = _lambda_full(lq1_ref[...], lk1_ref[...], lq2_ref[...], lk2_ref[...], lambda_init)
    o_t = (a0[0:A_V_DIM] / a0[A_V_DIM:A_V_DIM + 1]
           - lam * (a1[0:A_V_DIM] / a1[A_V_DIM:A_V_DIM + 1]))
    ms = jnp.mean(o_t * o_t, axis=0, keepdims=True)
    o_t = o_t * lax.rsqrt(ms + NORM_EPS) * gain_ref[...] * (1.0 - lambda_init)
    o_ref[...] = o_t.T.astype(BF16)


def _prompt_attn(qb, kb, vb, lq1, lk1, lq2, lk2, gain, b, t, tq, lambda_init):
    nq = t // tq
    vec = lambda bi, h, i: (0, 0)
    return pl.pallas_call(
        functools.partial(_prompt_attn_kernel, tq=tq, lambda_init=lambda_init),
        grid=(b, A_HEADS, nq),
        in_specs=[
            pl.BlockSpec((tq, A_QK_DIM), lambda bi, h, i: (bi * nq + i, h)),
            pl.BlockSpec((t, A_QK_DIM), lambda bi, h, i: (bi, h)),
            pl.BlockSpec((t, A_V_DIM), lambda bi, h, i: (bi, h)),
            pl.BlockSpec((1, A_HEAD_DIM), vec),
            pl.BlockSpec((1, A_HEAD_DIM), vec),
            pl.BlockSpec((1, A_HEAD_DIM), vec),
            pl.BlockSpec((1, A_HEAD_DIM), vec),
            pl.BlockSpec((A_V_DIM, 1), vec),
        ],
        out_specs=pl.BlockSpec((tq, A_V_DIM), lambda bi, h, i: (bi * nq + i, h)),
        out_shape=jax.ShapeDtypeStruct((b * t, A_WIDTH), BF16),
        scratch_shapes=[pltpu.VMEM((2, t, A_QK_DIM), BF16),
                        pltpu.VMEM((nq, ATTN_ACC_ROWS, tq), BF16),
                        pltpu.VMEM((2, ATTN_ACC_ROWS, tq), F32)],
        compiler_params=pltpu.CompilerParams(
            dimension_semantics=("arbitrary", "arbitrary", "arbitrary"),
            vmem_limit_bytes=V7X_VMEM_LIMIT),
        name="prompt_attn",
    )(qb, kb, vb, lq1, lk1, lq2, lk2, gain.reshape(A_V_DIM, 1))


def _decode_attn_kernel(pt_ref, q_ref, kn_ref, vn_ref, lq1_ref, lk1_ref, lq2_ref, lk2_ref,
                        gain_ref, *refs, pps, n_pages, lambda_init):
    k_refs = refs[:pps]
    v_refs = refs[pps:2 * pps]
    o_ref = refs[2 * pps]
    m_ref, l_ref, acc_ref = refs[2 * pps + 1:]
    j = pl.program_id(1)
    rows = 2 * A_HEADS
    page_rows = PAGE_SIZE * A_HEADS
    row_q = _iota((rows, A_QK_DIM), 0)
    lane_q = _iota((rows, A_QK_DIM), 1)
    qm = jnp.where((lane_q >> 6) == (row_q & 1), q_ref[0], 0.0).astype(BF16)
    slope = jnp.exp2((-8.0 / A_HEADS) * ((_iota((rows, 1), 0) >> 1) + 1).astype(F32))
    col = _iota((rows, pps * page_rows), 1)
    own_head = (col & (A_HEADS - 1)) == (_iota((rows, pps * page_rows), 0) >> 1)
    dist = (n_pages - j * pps) * PAGE_SIZE - (col >> 2)

    @pl.when(j == 0)
    def _():
        m_ref[...] = jnp.full((rows, 1), NEG_BIG, F32)
        l_ref[...] = jnp.zeros((rows, 1), F32)
        acc_ref[...] = jnp.zeros((rows, A_V_DIM), F32)

    s = jnp.concatenate([_dot_nt(qm, k_refs[u][...].astype(BF16)) for u in range(pps)], axis=1)
    s = jnp.where(own_head, s - slope * dist.astype(F32), NEG_BIG)
    m = m_ref[...]
    m_new = jnp.maximum(m, jnp.max(s, axis=-1, keepdims=True))
    alpha = jnp.exp(m - m_new)
    p = jnp.exp(s - m_new)
    pb = p.astype(BF16)
    pv = _dot(pb[:, 0:page_rows], v_refs[0][...].astype(BF16))
    for u in range(1, pps):
        pv = pv + _dot(pb[:, u * page_rows:(u + 1) * page_rows], v_refs[u][...].astype(BF16))
    m = m_new
    l = alpha * l_ref[...] + jnp.sum(p, axis=-1, keepdims=True)
    acc = alpha * acc_ref[...] + pv
    m_ref[...], l_ref[...], acc_ref[...] = m, l, acc

    @pl.when(j == pl.num_programs(1) - 1)
    def _():
        s_new = jnp.sum(qm.astype(F32) * kn_ref[0], axis=-1, keepdims=True)
        m_new = jnp.maximum(m, s_new)
        alpha = jnp.exp(m - m_new)
        p_new = jnp.exp(s_new - m_new)
        l_fin = alpha * l + p_new
        acc_fin = alpha * acc + p_new.astype(BF16).astype(F32) * vn_ref[0]
        lam = _lambda_full(lq1_ref[...], lk1_ref[...], lq2_ref[...], lk2_ref[...], lambda_init)
        comp = _iota((rows, 1), 0) & 1
        x = acc_fin * (jnp.where(comp == 0, 1.0, -lam) / l_fin)
        o = x + pltpu.roll(x, shift=rows - 1, axis=0)
        o_ref[0] = _rms_rows(o, gain_ref[...]) * (1.0 - lambda_init)


def _decode_attn(page_table, qb, kb, vb, cache_k, cache_v, layer, lq1, lk1, lq2, lk2, gain,
                 pps, lambda_init):
    db, n_pages = page_table.shape
    n_pool = cache_k.shape[1]
    page_rows = PAGE_SIZE * A_HEADS
    ck = cache_k.reshape(-1, A_QK_DIM)
    cv = cache_v.reshape(-1, A_V_DIM)
    pt = page_table.reshape(db * n_pages)
    rows = 2 * A_HEADS
    tok3 = lambda b, j, pt_ref: (b, 0, 0)
    vec = lambda b, j, pt_ref: (0, 0)

    def page_spec(u):
        return pl.BlockSpec(
            (page_rows, A_QK_DIM),
            lambda b, j, pt_ref: (layer * n_pool + pt_ref[b * n_pages + j * pps + u], 0))

    def per_row(z):
        return jnp.repeat(z.astype(F32).reshape(db, A_HEADS, A_QK_DIM), 2, axis=1)

    grid_spec = pltpu.PrefetchScalarGridSpec(
        num_scalar_prefetch=1,
        grid=(db, n_pages // pps),
        in_specs=[
            pl.BlockSpec((1, rows, A_QK_DIM), tok3),
            pl.BlockSpec((1, rows, A_QK_DIM), tok3),
            pl.BlockSpec((1, rows, A_V_DIM), tok3),
            pl.BlockSpec((1, A_HEAD_DIM), vec),
            pl.BlockSpec((1, A_HEAD_DIM), vec),
            pl.BlockSpec((1, A_HEAD_DIM), vec),
            pl.BlockSpec((1, A_HEAD_DIM), vec),
            pl.BlockSpec((1, A_V_DIM), vec),
        ] + [page_spec(u) for u in range(pps)] + [page_spec(u) for u in range(pps)],
        out_specs=pl.BlockSpec((1, rows, A_V_DIM), tok3),
        scratch_shapes=[pltpu.VMEM((rows, 1), F32), pltpu.VMEM((rows, 1), F32),
                        pltpu.VMEM((rows, A_V_DIM), F32)],
    )
    out = pl.pallas_call(
        functools.partial(_decode_attn_kernel, pps=pps, n_pages=n_pages, lambda_init=lambda_init),
        grid_spec=grid_spec,
        out_shape=jax.ShapeDtypeStruct((db, rows, A_V_DIM), F32),
        compiler_params=pltpu.CompilerParams(
            dimension_semantics=("arbitrary", "arbitrary"), vmem_limit_bytes=V7X_VMEM_LIMIT),
        name="decode_attn",
    )(pt, per_row(qb), per_row(kb), per_row(vb),
      lq1, lk1, lq2, lk2, gain, *([ck] * pps), *([cv] * pps))
    return out[:, 0::2, :].reshape(db, A_WIDTH)


def _rwkv_prep(x, prev, mu, w0, w2e, a0, a2e, g2, k_k, k_a, r_k, ones):
    m = x + (prev - x) * mu
    r = m[:, 0:R_WIDTH]
    k = m[:, R_WIDTH:2 * R_WIDTH]
    v = m[:, 2 * R_WIDTH:3 * R_WIDTH]
    xwa = m[:, 3 * R_WIDTH:3 * R_WIDTH + R_LORA_W + R_LORA_A]
    xg = m[:, 3 * R_WIDTH + R_LORA_W + R_LORA_A:]
    z = w0 + _dot(jnp.tanh(xwa).astype(BF16), w2e)
    lw = (-math.exp(-0.5)) * _sigmoid(z)
    a = _sigmoid(a0 + _dot(xwa.astype(BF16), a2e))
    g = _dot(_sigmoid(xg).astype(BF16), g2)
    kk = k * k_k
    kk = kk * lax.rsqrt(jnp.maximum(_group_sum(kk * kk, ones), 1e-24))
    k = k * (1.0 + (a - 1.0) * k_a)
    bonus = _group_sum(r * k * r_k, ones) * v
    return r, lw, k, v, kk, a, g, bonus


def _rwkv_finish(y, bonus, g, gn_w, gn_b, ones):
    mu = _group_sum(y, ones) * (1.0 / R_HEAD)
    d = y - mu
    var = _group_sum(d * d, ones) * (1.0 / R_HEAD)
    yn = d * lax.rsqrt(var + GN_EPS) * gn_w + gn_b
    return ((yn + bonus) * g).astype(BF16)


def _lane_halves(x, width):
    lane = _iota(x.shape, 1)
    zero = jnp.zeros_like(x)
    return jnp.concatenate([jnp.where(lane < width, x, zero), jnp.where(lane >= width, x, zero)],
                           axis=0)


def _lane_blocks(x, width):
    block = _iota(x.shape, 1) >> int(math.log2(width))
    zero = jnp.zeros_like(x)
    return jnp.concatenate([jnp.where(block == b, x, zero) for b in range(x.shape[1] // width)],
                           axis=0)


def _rwkv_chunk_kernel(x_ref, mu_ref, w0_ref, w2e_ref, a0_ref, a2e_ref, g2_ref, kk_ref, ka_ref,
                       rk_ref, gnw_ref, gnb_ref, ones_ref,
                       y_ref, st_ref, sh_ref, prev_ref, *, chunk, nb):
    c = chunk
    j = pl.program_id(1)

    @pl.when(j == 0)
    def _():
        prev_ref[...] = jnp.zeros_like(prev_ref)
        st_ref[...] = jnp.zeros_like(st_ref)

    ones = ones_ref[...]
    rows = _iota((c, R_SHIFT_W), 0)
    tri = (_iota((c, c), 0) >= _iota((c, c), 1)).astype(BF16)
    sls = [slice(p * LANES, (p + 1) * LANES) for p in range(R_PAIRS)]
    ah_p, rh_p, v_p, bt_p, kt_p, bb_p, kb_p, pe_p, st_idx, fin = ([] for _ in range(10))
    for bi in range(nb):
        x = x_ref[bi]
        prev = jnp.where(rows == 0, prev_ref[bi], pltpu.roll(x, shift=1, axis=0))
        prev_ref[bi] = x[c - 1:c, :]
        sh_ref[bi] = x[c - 1:c, :]
        r, lw, k, v, kk, a, g, bonus = _rwkv_prep(
            x, prev, mu_ref[...], w0_ref[...], w2e_ref[...], a0_ref[...], a2e_ref[...],
            g2_ref[...], kk_ref[...], ka_ref[...], rk_ref[...], ones)
        fin.append((bonus, g))
        l_hi = lw.astype(BF16)
        l_r1 = lw - l_hi.astype(F32)
        l_mid = l_r1.astype(BF16)
        l_lo = (l_r1 - l_mid.astype(F32)).astype(BF16)
        cum = _dot(tri, l_hi) + _dot(tri, l_mid) + _dot(tri, l_lo)
        cum_end = cum[c - 1:c, :]
        e_neg = jnp.exp(-cum)
        e_end = jnp.exp(cum_end - cum)
        kka = kk * a
        ah = -kk * jnp.exp(cum - lw)
        rh = r * jnp.exp(cum)
        bt = kka * e_neg
        kt = k * e_neg
        bb = kka * e_end
        kb = k * e_end
        p_end = jnp.exp(cum_end)
        for p, sl in enumerate(sls):
            for lst, z in ((ah_p, ah), (rh_p, rh), (v_p, v), (bt_p, bt), (kt_p, kt),
                           (bb_p, bb), (kb_p, kb), (pe_p, p_end)):
                lst.append(z[:, sl])
            st_idx.append((bi, p))

    t_idx = _iota((c, 2 * c), 0)
    s_idx = _iota((c, 2 * c), 1) & (c - 1)
    strict = t_idx > s_idx
    incl = t_idx >= s_idx
    eye_pack = (t_idx == s_idx).astype(F32)
    bd_mask = (_iota((LANES, LANES), 0) >> 6) == (_iota((LANES, LANES), 1) >> 6)
    eye_l = _iota((LANES, LANES), 0) == _iota((LANES, LANES), 1)
    zeros_c = jnp.zeros((c, LANES), F32)

    units = range(len(st_idx))
    v_blk = [_lane_halves(z, R_HEAD) for z in v_p]
    a_all = [_mm(jnp.concatenate([ah_p[u], rh_p[u]], axis=0),
                 jnp.concatenate([_lane_halves(bt_p[u], R_HEAD), _lane_halves(kt_p[u], R_HEAD)],
                                 axis=0), _dot_nt) for u in units]
    a_ab = [jnp.where(strict, z[0:c, 0:2 * c], 0.0) for z in a_all]
    a_ak = [jnp.where(strict, z[0:c, 2 * c:4 * c], 0.0) for z in a_all]
    a_rb = [jnp.where(incl, z[c:2 * c, 0:2 * c], 0.0) for z in a_all]
    a_rk = [jnp.where(incl, z[c:2 * c, 2 * c:4 * c], 0.0) for z in a_all]
    w1 = [_mm(a_ak[u], v_blk[u]) for u in units]
    quads = range(len(st_idx) // 2)
    a4 = [jnp.concatenate([a_ab[2 * q], a_ab[2 * q + 1]], axis=1) for q in quads]
    eye4 = jnp.concatenate([eye_pack, eye_pack], axis=1)
    inv4 = [eye4 + z for z in a4]
    bd = [_lane_blocks(z, c) for z in a4]
    pw = [_mm(a4[q], bd[q]) for q in quads]
    n_sq = int(math.log2(c)) - 1
    for step in range(n_sq):
        bd = [_lane_blocks(z, c) for z in pw]
        if step < n_sq - 1:
            both = [_mm(jnp.concatenate([pw[q], inv4[q]], axis=0), bd[q]) for q in quads]
            pw = [z[0:c] for z in both]
            inv4 = [inv4[q] + both[q][c:2 * c] for q in quads]
        else:
            inv4 = [inv4[q] + _mm(inv4[q], bd[q]) for q in quads]
    inv = [inv4[u // 2][:, (u % 2) * 2 * c:(u % 2 + 1) * 2 * c] for u in units]
    t4 = [_mm(inv[u], jnp.concatenate([_lane_halves(w1[u], R_HEAD),
                                       _lane_halves(ah_p[u], R_HEAD)], axis=1)) for u in units]
    wu = [z[:, 0:LANES] for z in t4]
    ah2 = [z[:, LANES:] for z in t4]
    zeros_blk = jnp.zeros((2 * c, LANES), F32)
    t5 = [_mm(jnp.concatenate([a_rb[u], a_rk[u]], axis=1),
              jnp.concatenate([
                  jnp.concatenate([_lane_halves(wu[u], R_HEAD), _lane_halves(ah2[u], R_HEAD)],
                                  axis=1),
                  jnp.concatenate([v_blk[u], zeros_blk], axis=1)], axis=0)) for u in units]
    t6 = [_mm(jnp.concatenate([bb_p[u], kb_p[u]], axis=0),
              jnp.concatenate([jnp.concatenate([wu[u], ah2[u]], axis=1),
                               jnp.concatenate([v_p[u], zeros_c], axis=1)], axis=0), _dot_tn)
          for u in units]
    st = [st_ref[bi, p] for bi, p in st_idx]
    ys = [_mm(rh_p[u] + t5[u][:, LANES:], st[u]) + t5[u][:, 0:LANES] for u in units]
    for u in units:
        h_new = jnp.where(bd_mask, t6[u][:, 0:LANES], 0.0)
        g_x = jnp.where(bd_mask, t6[u][:, LANES:], 0.0)
        p_col = jnp.sum(jnp.where(eye_l, pe_p[u], 0.0), axis=1, keepdims=True)
        st_ref[st_idx[u]] = p_col * st[u] + _mm(g_x, st[u]) + h_new

    for bi in range(nb):
        y = jnp.concatenate(ys[bi * R_PAIRS:(bi + 1) * R_PAIRS], axis=1)
        bonus, g = fin[bi]
        y_ref[bi] = _rwkv_finish(y, bonus, g, gnw_ref[...], gnb_ref[...], ones)


def _rwkv_prompt(rc3, params, chunk, nb):
    b, t, _ = rc3.shape
    const = lambda bi, j: (0, 0)
    specs = [pl.BlockSpec(p.shape, const) for p in params]
    return pl.pallas_call(
        functools.partial(_rwkv_chunk_kernel, chunk=chunk, nb=nb),
        grid=(b // nb, t // chunk),
        in_specs=[pl.BlockSpec((nb, chunk, R_SHIFT_W), lambda bi, j: (bi, j, 0))] + specs,
        out_specs=[
            pl.BlockSpec((nb, chunk, R_WIDTH), lambda bi, j: (bi, j, 0)),
            pl.BlockSpec((nb, R_PAIRS, LANES, LANES), lambda bi, j: (bi, 0, 0, 0)),
            pl.BlockSpec((nb, 1, R_SHIFT_W), lambda bi, j: (bi, 0, 0)),
        ],
        out_shape=(
            jax.ShapeDtypeStruct((b, t, R_WIDTH), BF16),
            jax.ShapeDtypeStruct((b, R_PAIRS, LANES, LANES), F32),
            jax.ShapeDtypeStruct((b, 1, R_SHIFT_W), F32),
        ),
        scratch_shapes=[pltpu.VMEM((nb, 1, R_SHIFT_W), F32)],
        compiler_params=pltpu.CompilerParams(
            dimension_semantics=("arbitrary", "arbitrary"), vmem_limit_bytes=V7X_VMEM_LIMIT),
        name="rwkv_chunk",
    )(rc3, *params)


def _rwkv_step_kernel(x_ref, prev_ref, st_in_ref, mu_ref, w0_ref, w2e_ref, a0_ref, a2e_ref,
                      g2_ref, kk_ref, ka_ref, rk_ref, gnw_ref, gnb_ref, ones_ref,
                      y_ref, st_ref, yrow_ref, *, rows):
    ones = ones_ref[...]
    r, lw, k, v, kk, a, g, bonus = _rwkv_prep(
        x_ref[...], prev_ref[...], mu_ref[...], w0_ref[...], w2e_ref[...], a0_ref[...],
        a2e_ref[...], g2_ref[...], kk_ref[...], ka_ref[...], rk_ref[...], ones)
    vecs = (jnp.exp(lw), kk * a, k, v, kk, r)
    bd_mask = (_iota((LANES, LANES), 0) >> 6) == (_iota((LANES, LANES), 1) >> 6)
    eye_l = _iota((LANES, LANES), 0) == _iota((LANES, LANES), 1)

    def col(row_vec):
        return jnp.sum(jnp.where(eye_l, row_vec, 0.0), axis=1, keepdims=True)

    for bi in range(rows):
        for p in range(R_PAIRS):
            sl = slice(p * LANES, (p + 1) * LANES)
            w_r, b_r, k_r, v_r, kk_r, r_r = (z[bi:bi + 1, sl] for z in vecs)
            st = st_in_ref[bi, p]
            sa = -jnp.sum(col(kk_r) * st, axis=0, keepdims=True)
            upd = jnp.where(bd_mask, col(b_r) * sa + col(k_r) * v_r, 0.0)
            st_new = col(w_r) * st + upd
            st_ref[bi, p] = st_new
            yrow_ref[bi:bi + 1, sl] = jnp.sum(col(r_r) * st_new, axis=0, keepdims=True)
    y_ref[...] = _rwkv_finish(yrow_ref[...], bonus, g, gnw_ref[...], gnb_ref[...], ones)


def _rwkv_step(rc, shift_prev, st_pairs, params, rows):
    db = rc.shape[0]
    const = lambda i: (0, 0)
    specs = [pl.BlockSpec(p.shape, const) for p in params]
    return pl.pallas_call(
        functools.partial(_rwkv_step_kernel, rows=rows),
        grid=(db // rows,),
        in_specs=[
            pl.BlockSpec((rows, R_SHIFT_W), lambda i: (i, 0)),
            pl.BlockSpec((rows, R_SHIFT_W), lambda i: (i, 0)),
            pl.BlockSpec((rows, R_PAIRS, LANES, LANES), lambda i: (i, 0, 0, 0)),
        ] + specs,
        out_specs=[
            pl.BlockSpec((rows, R_WIDTH), lambda i: (i, 0)),
            pl.BlockSpec((rows, R_PAIRS, LANES, LANES), lambda i: (i, 0, 0, 0)),
        ],
        out_shape=(
            jax.ShapeDtypeStruct((db, R_WIDTH), BF16),
            jax.ShapeDtypeStruct((db, R_PAIRS, LANES, LANES), F32),
        ),
        scratch_shapes=[pltpu.VMEM((rows, R_WIDTH), F32)],
        compiler_params=pltpu.CompilerParams(
            dimension_semantics=("arbitrary",), vmem_limit_bytes=V7X_VMEM_LIMIT),
        name="rwkv_step",
    )(rc, shift_prev, st_pairs, *params)


def _state_to_pairs(s):
    b = s.shape[0]
    st = jnp.swapaxes(s, -1, -2).reshape(b, R_PAIRS, 2, R_HEAD, R_HEAD)
    eye2 = jnp.eye(2, dtype=s.dtype)
    return jnp.einsum("bpgij,gh->bpgihj", st, eye2).reshape(b, R_PAIRS, LANES, LANES)


def _pairs_to_state(st):
    b = st.shape[0]
    blocks = st.reshape(b, R_PAIRS, 2, R_HEAD, 2, R_HEAD)
    diag = jnp.stack([blocks[:, :, 0, :, 0, :], blocks[:, :, 1, :, 1, :]], axis=2)
    return jnp.swapaxes(diag.reshape(b, R_HEADS, R_HEAD, R_HEAD), -1, -2)


def _merge_ffn_kernel(x_ref, o_ref, rw_ref, g_ref, wpa_ref, wpb_ref, wout_ref, nf_ref,
                      wg_ref, wu_ref, wd_ref, y_ref, *, d_ff):
    ya = _dot(o_ref[...].astype(BF16), wpa_ref[...])
    yb = _dot(rw_ref[...], wpb_ref[...])
    merged = (g_ref[:, 0:D_MODEL] * ya + g_ref[:, D_MODEL:] * yb).astype(BF16)
    x1 = x_ref[...] + _dot(merged, wout_ref[...])
    hf = _rms_rows(x1, nf_ref[...]).astype(BF16)
    acc = x1
    step = 512
    for c in range(0, d_ff, step):
        n = min(step, d_ff - c)
        gate = _dot(hf, wg_ref[:, c:c + n])
        up = _dot(hf, wu_ref[:, c:c + n])
        act = (gate * _sigmoid(gate) * up).astype(BF16)
        acc = acc + _dot(act, wd_ref[c:c + n, :])
    y_ref[...] = acc


def _merge_ffn(x2, o, rw, gates, wpa, wpb, wout, nf, wg, wu, wd, tm):
    m = x2.shape[0]
    d_ff = wg.shape[1]
    row = lambda i: (i, 0)
    const = lambda i: (0, 0)
    return pl.pallas_call(
        functools.partial(_merge_ffn_kernel, d_ff=d_ff),
        grid=(m // tm,),
        in_specs=[
            pl.BlockSpec((tm, D_MODEL), row),
            pl.BlockSpec((tm, A_WIDTH), row),
            pl.BlockSpec((tm, R_WIDTH), row),
            pl.BlockSpec((tm, 2 * D_MODEL), row),
            pl.BlockSpec(wpa.shape, const),
            pl.BlockSpec(wpb.shape, const),
            pl.BlockSpec(wout.shape, const),
            pl.BlockSpec((1, D_MODEL), const),
            pl.BlockSpec(wg.shape, const),
            pl.BlockSpec(wu.shape, const),
            pl.BlockSpec(wd.shape, const),
        ],
        out_specs=pl.BlockSpec((tm, D_MODEL), row),
        out_shape=jax.ShapeDtypeStruct((m, D_MODEL), F32),
        compiler_params=pltpu.CompilerParams(
            dimension_semantics=("arbitrary",), vmem_limit_bytes=V7X_VMEM_LIMIT),
        name="merge_ffn",
    )(x2, o, rw, gates, wpa, wpb, wout, nf, wg, wu, wd)


def _row(v):
    return v.reshape(1, -1).astype(F32)


def _pick_tile(m, target):
    t = min(m, target)
    while m % t:
        t //= 2
    return t


def kernel(x_prompt, x_sample, cache_k, cache_v, page_table, state_wkv, state_shift, norm_mix, w_in, q_gain, k_gain, lambda_q1, lambda_k1, lambda_q2, lambda_k2, attn_out_gain, w_pa, shift_mu, w0, w2, a0, a2, g2, k_k, k_a, r_k, gn_w, gn_b, w_pb, w_out, norm_ffn, w_gate, w_up, w_down):
    b, t, _ = x_prompt.shape
    db, ds, _ = x_sample.shape
    assert ds == 1, "sample group carries one new token per sequence"
    depth = w_in.shape[0]
    n_pages = page_table.shape[1]
    xp = x_prompt.reshape(b * t, D_MODEL)
    xs = x_sample.reshape(db, D_MODEL)
    head_ones = (jnp.arange(A_WIDTH)[:, None] // A_HEAD_DIM
                 == jnp.arange(A_WIDTH)[None, :] // A_HEAD_DIM).astype(BF16)
    outs = [[] for _ in range(8)]
    tm_p = _pick_tile(b * t, 512)
    tm_s = _pick_tile(db, 512)
    tq = _pick_tile(t, 512)
    chunk = _pick_tile(t, RWKV_CHUNK)
    chunk_nb = _pick_tile(b, 4)
    pps = _pick_tile(n_pages, 16)
    step_rows = _pick_tile(db, 16)
    for l in range(depth):
        lambda_init = 0.8 - 0.6 * math.exp(-0.3 * l)
        w_in_bf = w_in[l].astype(BF16)
        qg = _row(jnp.tile(q_gain[l], A_WIDTH // A_HEAD_DIM))
        kg = _row(jnp.tile(k_gain[l], A_WIDTH // A_HEAD_DIM))
        nm = _row(norm_mix[l])
        lam_vecs = (_row(lambda_q1[l]), _row(lambda_k1[l]), _row(lambda_q2[l]), _row(lambda_k2[l]))
        og = _row(attn_out_gain[l])
        zeros_lora = jnp.zeros((R_LORA_W, R_WIDTH), F32)
        rwkv_params = (
            _row(shift_mu[l]), _row(w0[l]),
            jnp.concatenate([w2[l], zeros_lora], axis=0).astype(BF16),
            _row(a0[l]),
            jnp.concatenate([zeros_lora, a2[l]], axis=0).astype(BF16),
            g2[l].astype(BF16), _row(k_k[l]), _row(k_a[l]), _row(r_k[l]),
            _row(gn_w[l]), _row(gn_b[l]), head_ones)
        ffn_w = (w_pa[l].astype(BF16), w_pb[l].astype(BF16), w_out[l].astype(BF16),
                 _row(norm_ffn[l]), w_gate[l].astype(BF16), w_up[l].astype(BF16),
                 w_down[l].astype(BF16))

        qb, k32, kb, v32, vb, rc, gates = _proj(xp, nm, w_in_bf, qg, kg, head_ones, tm_p)
        o = _prompt_attn(qb, kb, vb, *lam_vecs, og, b, t, tq, lambda_init)
        rw, st_p, sh_p = _rwkv_prompt(rc.reshape(b, t, R_SHIFT_W), rwkv_params, chunk, chunk_nb)
        xp = _merge_ffn(xp, o, rw.reshape(b * t, R_WIDTH), gates, *ffn_w, tm_p)
        outs[0].append(k32.reshape(b, t, A_HEADS, A_QK_DIM))
        outs[1].append(v32.reshape(b, t, A_HEADS, A_V_DIM))
        outs[2].append(_pairs_to_state(st_p))
        outs[3].append(sh_p.reshape(b, R_SHIFT_W))

        qb, k32, kb, v32, vb, rc, gates = _proj(xs, nm, w_in_bf, qg, kg, head_ones, tm_s)
        o = _decode_attn(page_table, qb, kb, vb, cache_k, cache_v, l, *lam_vecs, og,
                         pps, lambda_init)
        rw, st_s = _rwkv_step(rc, state_shift[l], _state_to_pairs(state_wkv[l]), rwkv_params,
                              step_rows)
        xs = _merge_ffn(xs, o, rw, gates, *ffn_w, tm_s)
        outs[4].append(k32.reshape(db, 1, A_HEADS, A_QK_DIM))
        outs[5].append(v32.reshape(db, 1, A_HEADS, A_V_DIM))
        outs[6].append(_pairs_to_state(st_s))
        outs[7].append(rc)
    return (xp.reshape(b, t, D_MODEL), xs.reshape(db, 1, D_MODEL),
            *(jnp.stack(o) for o in outs))
```

```python
import functools
import math

import jax
import jax.numpy as jnp
from jax import lax
from jax.experimental import pallas as pl
from jax.experimental.pallas import tpu as pltpu

F32 = jnp.float32
BF16 = jnp.bfloat16

D_MODEL = 1024
PAGE_SIZE = 128
A_HEADS = 4
A_HEAD_DIM = 64
A_QK_DIM = 2 * A_HEAD_DIM
A_V_DIM = 2 * A_HEAD_DIM
A_WIDTH = A_HEADS * A_V_DIM
R_HEAD = 64
R_HEADS = 8
R_WIDTH = R_HEADS * R_HEAD
R_LORA_W = 64
R_LORA_A = 64
R_LORA_G = 128
R_SHIFT_W = 3 * R_WIDTH + R_LORA_W + R_LORA_A + R_LORA_G
OFF_Q = 0
OFF_K = OFF_Q + A_HEADS * A_QK_DIM
OFF_V = OFF_K + A_HEADS * A_QK_DIM
OFF_R = OFF_V + A_WIDTH
OFF_G = OFF_R + R_SHIFT_W
IN_WIDTH = OFF_G + 2 * D_MODEL
NORM_EPS = 1e-6
GN_EPS = 64e-5
NEG_BIG = -1e30

V7X_VMEM_LIMIT = 56 * 1024 * 1024
LANES = 128
R_PAIRS = R_WIDTH // LANES
RWKV_CHUNK = 64


def _dot(a, b):
    return jnp.dot(a, b, preferred_element_type=F32)


def _dot_nt(a, b):
    return lax.dot_general(a, b, (((1,), (1,)), ((), ())), preferred_element_type=F32)


def _dot_tn(a, b):
    return lax.dot_general(a, b, (((0,), (0,)), ((), ())), preferred_element_type=F32)


def _split2(x):
    hi = x.astype(BF16)
    lo = (x - hi.astype(F32)).astype(BF16)
    return hi, lo


def _mm(a, b, dot=_dot):
    return dot(a.astype(BF16), b.astype(BF16))


def _sigmoid(x):
    return 1.0 / (1.0 + jnp.exp(-x))


def _iota(shape, dim):
    return lax.broadcasted_iota(jnp.int32, shape, dim)


def _group_sum(z, ones_bf):
    return _dot(z.astype(BF16), ones_bf)


def _rms_rows(x, gain_row):
    ms = jnp.mean(x * x, axis=-1, keepdims=True)
    return x * lax.rsqrt(ms + NORM_EPS) * gain_row


def _proj_kernel(x_ref, nm_ref, w_ref, qg_ref, kg_ref, ones_ref,
                 q_ref, k_ref, kb_ref, v_ref, vb_ref, r_ref, g_ref):
    h = _rms_rows(x_ref[...], nm_ref[...]).astype(BF16)
    ones = ones_ref[...]

    def head_norm(z, gain_row):
        ss = _group_sum(z * z, ones)
        return z * lax.rsqrt(ss * (1.0 / A_HEAD_DIM) + NORM_EPS) * gain_row

    q = head_norm(_dot(h, w_ref[:, OFF_Q:OFF_K]), qg_ref[...])
    q_ref[...] = (q * (A_HEAD_DIM ** -0.5)).astype(BF16)
    tm = x_ref.shape[0]
    k = head_norm(_dot(h, w_ref[:, OFF_K:OFF_V]), kg_ref[...])
    kb_ref[...] = k.astype(BF16)
    v = _dot(h, w_ref[:, OFF_V:OFF_R])
    vb_ref[...] = v.astype(BF16)
    for hd in range(A_HEADS):
        sl = slice(hd * A_QK_DIM, (hd + 1) * A_QK_DIM)
        k_ref[pl.ds(hd, tm, stride=A_HEADS), :] = k[:, sl]
        v_ref[pl.ds(hd, tm, stride=A_HEADS), :] = v[:, sl]
    step = 512
    for c in range(0, R_SHIFT_W, step):
        n = min(step, R_SHIFT_W - c)
        r_ref[:, c:c + n] = _dot(h, w_ref[:, OFF_R + c:OFF_R + c + n])
    for c in range(0, 2 * D_MODEL, step):
        g_ref[:, c:c + step] = _sigmoid(_dot(h, w_ref[:, OFF_G + c:OFF_G + c + step]))


def _proj(x2, nm, w_in_bf, qg, kg, ones_bf, tm):
    m = x2.shape[0]
    row = lambda i: (i, 0)
    const = lambda i: (0, 0)
    out_shape = (
        jax.ShapeDtypeStruct((m, A_WIDTH), BF16),
        jax.ShapeDtypeStruct((m * A_HEADS, A_QK_DIM), F32),
        jax.ShapeDtypeStruct((m, A_WIDTH), BF16),
        jax.ShapeDtypeStruct((m * A_HEADS, A_V_DIM), F32),
        jax.ShapeDtypeStruct((m, A_WIDTH), BF16),
        jax.ShapeDtypeStruct((m, R_SHIFT_W), F32),
        jax.ShapeDtypeStruct((m, 2 * D_MODEL), F32),
    )
    return pl.pallas_call(
        _proj_kernel,
        grid=(m // tm,),
        in_specs=[
            pl.BlockSpec((tm, D_MODEL), row),
            pl.BlockSpec((1, D_MODEL), const),
            pl.BlockSpec((D_MODEL, IN_WIDTH), const),
            pl.BlockSpec((1, A_WIDTH), const),
            pl.BlockSpec((1, A_WIDTH), const),
            pl.BlockSpec((A_WIDTH, A_WIDTH), const),
        ],
        out_specs=[
            pl.BlockSpec((tm, A_WIDTH), row),
            pl.BlockSpec((tm * A_HEADS, A_QK_DIM), row),
            pl.BlockSpec((tm, A_WIDTH), row),
            pl.BlockSpec((tm * A_HEADS, A_V_DIM), row),
            pl.BlockSpec((tm, A_WIDTH), row),
            pl.BlockSpec((tm, R_SHIFT_W), row),
            pl.BlockSpec((tm, 2 * D_MODEL), row),
        ],
        out_shape=out_shape,
        compiler_params=pltpu.CompilerParams(
            dimension_semantics=("arbitrary",), vmem_limit_bytes=V7X_VMEM_LIMIT),
        name="proj",
    )(x2, nm, w_in_bf, qg, kg, ones_bf)


def _lambda_full(lq1, lk1, lq2, lk2, lambda_init):
    s1 = jnp.sum(lq1 * lk1, axis=-1, keepdims=True)
    s2 = jnp.sum(lq2 * lk2, axis=-1, keepdims=True)
    return jnp.exp(s1) - jnp.exp(s2) + lambda_init


def _softmax_update(s, m, l, acc, vs):
    m_new = jnp.maximum(m, jnp.max(s, axis=-1, keepdims=True))
    alpha = jnp.exp(m - m_new)
    p = jnp.exp(s - m_new)
    l = alpha * l + jnp.sum(p, axis=-1, keepdims=True)
    acc = alpha * acc + _dot(p.astype(BF16), vs)
    return m_new, l, acc


ATTN_ACC_ROWS = A_V_DIM + 16
ATTN_HEADS_PER_STEP = 1


def _alibi_lanes(pos, slope, keys):
    l4 = _iota(pos.shape, 1) & (A_HEAD_DIM - 1)
    lo = slope * (pos & 255).astype(F32)
    hi = slope * ((pos >> 8) << 8).astype(F32)
    if keys:
        val = jnp.where(l4 < 2, 1.0, jnp.where(l4 == 2, lo, jnp.where(l4 == 3, hi, 0.0)))
    else:
        val = jnp.where(l4 == 0, -lo, jnp.where(l4 == 1, -hi, jnp.where(l4 < 4, 1.0, 0.0)))
    return val.astype(BF16)


def _prompt_attn_kernel(q_ref, k_ref, v_ref, lq1_ref, lk1_ref, lq2_ref, lk2_ref, gain_ref,
                        o_ref, kc_ref, vt_ref, acc_ref, *, tq, lambda_init):
    hg = pl.program_id(1)
    i = pl.program_id(2)
    t = k_ref.shape[0]
    hp = ATTN_HEADS_PER_STEP
    n_st = 2 * hp
    slopes = [jnp.exp2(jnp.full((1, 1), -8.0 / A_HEADS, F32) * (hg * hp + hh + 1).astype(F32))
              for hh in range(hp)]
    head_lanes = [slice(hh * A_QK_DIM, (hh + 1) * A_QK_DIM) for hh in range(hp)]
    pos = _iota((tq, A_QK_DIM), 0)
    low = _iota((tq, A_QK_DIM), 1) < A_HEAD_DIM

    @pl.when(i == 0)
    def _():
        ones_rows = (_iota((ATTN_ACC_ROWS - A_V_DIM, tq), 0) == 0).astype(BF16)
        for hh in range(hp):
            ek = _alibi_lanes(pos, slopes[hh], keys=True)
            for jc in range(t // tq):
                rows = slice(jc * tq, (jc + 1) * tq)
                kt = k_ref[rows, head_lanes[hh]]
                kc_ref[2 * hh, rows, :] = jnp.where(low, kt, ek)
                kc_ref[2 * hh + 1, rows, :] = jnp.where(low, ek, kt)
                vt_ref[hh, jc, 0:A_V_DIM, :] = (
                    v_ref[rows, head_lanes[hh]].astype(F32).T.astype(BF16))
                vt_ref[hh, jc, A_V_DIM:, :] = ones_rows

    qc = []
    for hh in range(hp):
        q = q_ref[:, head_lanes[hh]]
        eq = _alibi_lanes(pos, slopes[hh], keys=False)
        qc += [jnp.where(low, q, eq), jnp.where(low, eq, q)]
    kv_pos = _iota((tq, tq), 0)
    q_pos = _iota((tq, tq), 1)

    def tiles(group, ms):
        s = []
        for j, masked in group:
            rows = pl.ds(pl.multiple_of(j * tq, tq), tq)
            sj = [_dot_nt(kc_ref[st, rows, :], qc[st]) for st in range(n_st)]
            if masked:
                sj = [jnp.where(kv_pos <= q_pos, sc, NEG_BIG) for sc in sj]
            s.append(sj)
        ms = list(ms)
        acc = [acc_ref[st] for st in range(n_st)]
        for (j, _), sj in zip(group, s):
            for st in range(n_st):
                off = slopes[st // 2] * ((j - i) * tq).astype(F32)
                m_rel = ms[st] - off
                m_new = jnp.maximum(m_rel, jnp.max(sj[st], axis=0, keepdims=True))
                p = jnp.exp(sj[st] - m_new).astype(BF16)
                acc[st] = jnp.exp(m_rel - m_new) * acc[st] + _dot(vt_ref[st // 2, j], p)
                ms[st] = m_new + off
        for st in range(n_st):
            acc_ref[st] = acc[st]
        return tuple(ms)

    acc_ref[...] = jnp.zeros_like(acc_ref)
    ms = lax.fori_loop(0, i // 2,
                       lambda pr, ms: tiles([(2 * pr, False), (2 * pr + 1, False)], ms),
                       tuple(jnp.full((1, tq), NEG_BIG, F32) for _ in range(n_st)))
    lax.cond((i & 1) == 1,
             lambda ms: tiles([(i - 1, False), (i, True)], ms),
             lambda ms: tiles([(i, True)], ms), ms)
    lam = _lambda_full(lq1_ref[...], lk1_ref[...], lq2_ref[...], lk2_ref[...], lambda_init)
    for hh in range(hp):
        a0, a1 = acc_ref[2 * hh], acc_ref[2 * hh + 1]
        o_t = (a0[0:A_V_DIM] / a0[A_V_DIM:A_V_DIM + 1]
               - lam * (a1[0:A_V_DIM] / a1[A_V_DIM:A_V_DIM + 1]))
        ms_o = jnp.mean(o_t * o_t, axis=0, keepdims=True)
        o_t = o_t * lax.rsqrt(ms_o + NORM_EPS) * gain_ref[...] * (1.0 - lambda_init)
        o_ref[:, head_lanes[hh]] = o_t.T.astype(BF16)


def _prompt_attn(qb, kb, vb, lq1, lk1, lq2, lk2, gain, b, t, tq, lambda_init):
    nq = t // tq
    hp = ATTN_HEADS_PER_STEP
    vec = lambda bi, h, i: (0, 0)
    return pl.pallas_call(
        functools.partial(_prompt_attn_kernel, tq=tq, lambda_init=lambda_init),
        grid=(b, A_HEADS // hp, nq),
        in_specs=[
            pl.BlockSpec((tq, hp * A_QK_DIM), lambda bi, h, i: (bi * nq + i, h)),
            pl.BlockSpec((t, hp * A_QK_DIM), lambda bi, h, i: (bi, h)),
            pl.BlockSpec((t, hp * A_V_DIM), lambda bi, h, i: (bi, h)),
            pl.BlockSpec((1, A_HEAD_DIM), vec),
            pl.BlockSpec((1, A_HEAD_DIM), vec),
            pl.BlockSpec((1, A_HEAD_DIM), vec),
            pl.BlockSpec((1, A_HEAD_DIM), vec),
            pl.BlockSpec((A_V_DIM, 1), vec),
        ],
        out_specs=pl.BlockSpec((tq, hp * A_V_DIM), lambda bi, h, i: (bi * nq + i, h)),
        out_shape=jax.ShapeDtypeStruct((b * t, A_WIDTH), BF16),
        scratch_shapes=[pltpu.VMEM((2 * hp, t, A_QK_DIM), BF16),
                        pltpu.VMEM((hp, nq, ATTN_ACC_ROWS, tq), BF16),
                        pltpu.VMEM((2 * hp, ATTN_ACC_ROWS, tq), F32)],
        compiler_params=pltpu.CompilerParams(
            dimension_semantics=("arbitrary", "arbitrary", "arbitrary"),
            vmem_limit_bytes=V7X_VMEM_LIMIT),
        name="prompt_attn",
    )(qb, kb, vb, lq1, lk1, lq2, lk2, gain.reshape(A_V_DIM, 1))


def _decode_attn_kernel(pt_ref, q_ref, kn_ref, vn_ref, lq1_ref, lk1_ref, lq2_ref, lk2_ref,
                        gain_ref, *refs, pps, n_pages, lambda_init):
    k_refs = refs[:pps]
    v_refs = refs[pps:2 * pps]
    o_ref = refs[2 * pps]
    m_ref, l_ref, acc_ref = refs[2 * pps + 1:]
    j = pl.program_id(1)
    rows = 2 * A_HEADS
    page_rows = PAGE_SIZE * A_HEADS
    row_q = _iota((rows, A_QK_DIM), 0)
    lane_q = _iota((rows, A_QK_DIM), 1)
    qm = jnp.where((lane_q >> 6) == (row_q & 1), q_ref[0], 0.0).astype(BF16)
    slope = jnp.exp2((-8.0 / A_HEADS) * ((_iota((rows, 1), 0) >> 1) + 1).astype(F32))
    col = _iota((rows, pps * page_rows), 1)
    own_head = (col & (A_HEADS - 1)) == (_iota((rows, pps * page_rows), 0) >> 1)
    dist = (n_pages - j * pps) * PAGE_SIZE - (col >> 2)

    @pl.when(j == 0)
    def _():
        m_ref[...] = jnp.full((rows, 1), NEG_BIG, F32)
        l_ref[...] = jnp.zeros((rows, 1), F32)
        acc_ref[...] = jnp.zeros((rows, A_V_DIM), F32)

    s = jnp.concatenate([_dot_nt(qm, k_refs[u][...].astype(BF16)) for u in range(pps)], axis=1)
    s = jnp.where(own_head, s - slope * dist.astype(F32), NEG_BIG)
    m = m_ref[...]
    m_new = jnp.maximum(m, jnp.max(s, axis=-1, keepdims=True))
    alpha = jnp.exp(m - m_new)
    p = jnp.exp(s - m_new)
    pb = p.astype(BF16)
    pv = _dot(pb[:, 0:page_rows], v_refs[0][...].astype(BF16))
    for u in range(1, pps):
        pv = pv + _dot(pb[:, u * page_rows:(u + 1) * page_rows], v_refs[u][...].astype(BF16))
    m = m_new
    l = alpha * l_ref[...] + jnp.sum(p, axis=-1, keepdims=True)
    acc = alpha * acc_ref[...] + pv
    m_ref[...], l_ref[...], acc_ref[...] = m, l, acc

    @pl.when(j == pl.num_programs(1) - 1)
    def _():
        s_new = jnp.sum(qm.astype(F32) * kn_ref[0], axis=-1, keepdims=True)
        m_new = jnp.maximum(m, s_new)
        alpha = jnp.exp(m - m_new)
        p_new = jnp.exp(s_new - m_new)
        l_fin = alpha * l + p_new
        acc_fin = alpha * acc + p_new.astype(BF16).astype(F32) * vn_ref[0]
        lam = _lambda_full(lq1_ref[...], lk1_ref[...], lq2_ref[...], lk2_ref[...], lambda_init)
        comp = _iota((rows, 1), 0) & 1
        x = acc_fin * (jnp.where(comp == 0, 1.0, -lam) / l_fin)
        o = x + pltpu.roll(x, shift=rows - 1, axis=0)
        o_ref[0] = _rms_rows(o, gain_ref[...]) * (1.0 - lambda_init)


def _decode_attn(page_table, qb, kb, vb, cache_k, cache_v, layer, lq1, lk1, lq2, lk2, gain,
                 pps, lambda_init):
    db, n_pages = page_table.shape
    n_pool = cache_k.shape[1]
    page_rows = PAGE_SIZE * A_HEADS
    ck = cache_k.reshape(-1, A_QK_DIM)
    cv = cache_v.reshape(-1, A_V_DIM)
    pt = page_table.reshape(db * n_pages)
    rows = 2 * A_HEADS
    tok3 = lambda b, j, pt_ref: (b, 0, 0)
    vec = lambda b, j, pt_ref: (0, 0)

    def page_spec(u):
        return pl.BlockSpec(
            (page_rows, A_QK_DIM),
            lambda b, j, pt_ref: (layer * n_pool + pt_ref[b * n_pages + j * pps + u], 0))

    def per_row(z):
        return jnp.repeat(z.astype(F32).reshape(db, A_HEADS, A_QK_DIM), 2, axis=1)

    grid_spec = pltpu.PrefetchScalarGridSpec(
        num_scalar_prefetch=1,
        grid=(db, n_pages // pps),
        in_specs=[
            pl.BlockSpec((1, rows, A_QK_DIM), tok3),
            pl.BlockSpec((1, rows, A_QK_DIM), tok3),
            pl.BlockSpec((1, rows, A_V_DIM), tok3),
            pl.BlockSpec((1, A_HEAD_DIM), vec),
            pl.BlockSpec((1, A_HEAD_DIM), vec),
            pl.BlockSpec((1, A_HEAD_DIM), vec),
            pl.BlockSpec((1, A_HEAD_DIM), vec),
            pl.BlockSpec((1, A_V_DIM), vec),
        ] + [page_spec(u) for u in range(pps)] + [page_spec(u) for u in range(pps)],
        out_specs=pl.BlockSpec((1, rows, A_V_DIM), tok3),
        scratch_shapes=[pltpu.VMEM((rows, 1), F32), pltpu.VMEM((rows, 1), F32),
                        pltpu.VMEM((rows, A_V_DIM), F32)],
    )
    out = pl.pallas_call(
        functools.partial(_decode_attn_kernel, pps=pps, n_pages=n_pages, lambda_init=lambda_init),
        grid_spec=grid_spec,
        out_shape=jax.ShapeDtypeStruct((db, rows, A_V_DIM), F32),
        compiler_params=pltpu.CompilerParams(
            dimension_semantics=("arbitrary", "arbitrary"), vmem_limit_bytes=V7X_VMEM_LIMIT),
        name="decode_attn",
    )(pt, per_row(qb), per_row(kb), per_row(vb),
      lq1, lk1, lq2, lk2, gain, *([ck] * pps), *([cv] * pps))
    return out[:, 0::2, :].reshape(db, A_WIDTH)


def _rwkv_prep(x, prev, mu, w0, w2e, a0, a2e, g2, k_k, k_a, r_k, ones):
    m = x + (prev - x) * mu
    r = m[:, 0:R_WIDTH]
    k = m[:, R_WIDTH:2 * R_WIDTH]
    v = m[:, 2 * R_WIDTH:3 * R_WIDTH]
    xwa = m[:, 3 * R_WIDTH:3 * R_WIDTH + R_LORA_W + R_LORA_A]
    xg = m[:, 3 * R_WIDTH + R_LORA_W + R_LORA_A:]
    z = w0 + _dot(jnp.tanh(xwa).astype(BF16), w2e)
    lw = (-math.exp(-0.5)) * _sigmoid(z)
    a = _sigmoid(a0 + _dot(xwa.astype(BF16), a2e))
    g = _dot(_sigmoid(xg).astype(BF16), g2)
    kk = k * k_k
    kk = kk * lax.rsqrt(jnp.maximum(_group_sum(kk * kk, ones), 1e-24))
    k = k * (1.0 + (a - 1.0) * k_a)
    bonus = _group_sum(r * k * r_k, ones) * v
    return r, lw, k, v, kk, a, g, bonus


def _rwkv_finish(y, bonus, g, gn_w, gn_b, ones):
    mu = _group_sum(y, ones) * (1.0 / R_HEAD)
    d = y - mu
    var = _group_sum(d * d, ones) * (1.0 / R_HEAD)
    yn = d * lax.rsqrt(var + GN_EPS) * gn_w + gn_b
    return ((yn + bonus) * g).astype(BF16)


def _lane_halves(x, width):
    lane = _iota(x.shape, 1)
    zero = jnp.zeros_like(x)
    return jnp.concatenate([jnp.where(lane < width, x, zero), jnp.where(lane >= width, x, zero)],
                           axis=0)


def _lane_blocks(x, width):
    block = _iota(x.shape, 1) >> int(math.log2(width))
    zero = jnp.zeros_like(x)
    return jnp.concatenate([jnp.where(block == b, x, zero) for b in range(x.shape[1] // width)],
                           axis=0)


def _rwkv_chunk_kernel(x_ref, mu_ref, w0_ref, w2e_ref, a0_ref, a2e_ref, g2_ref, kk_ref, ka_ref,
                       rk_ref, gnw_ref, gnb_ref, ones_ref,
                       y_ref, st_ref, sh_ref, prev_ref, *, chunk, nb):
    c = chunk
    j = pl.program_id(1)

    @pl.when(j == 0)
    def _():
        prev_ref[...] = jnp.zeros_like(prev_ref)
        st_ref[...] = jnp.zeros_like(st_ref)

    ones = ones_ref[...]
    rows = _iota((c, R_SHIFT_W), 0)
    tri = (_iota((c, c), 0) >= _iota((c, c), 1)).astype(BF16)
    sls = [slice(p * LANES, (p + 1) * LANES) for p in range(R_PAIRS)]
    ah_p, rh_p, v_p, bt_p, kt_p, bb_p, kb_p, pe_p, st_idx, fin = ([] for _ in range(10))
    for bi in range(nb):
        x = x_ref[bi]
        prev = jnp.where(rows == 0, prev_ref[bi], pltpu.roll(x, shift=1, axis=0))
        prev_ref[bi] = x[c - 1:c, :]
        sh_ref[bi] = x[c - 1:c, :]
        r, lw, k, v, kk, a, g, bonus = _rwkv_prep(
            x, prev, mu_ref[...], w0_ref[...], w2e_ref[...], a0_ref[...], a2e_ref[...],
            g2_ref[...], kk_ref[...], ka_ref[...], rk_ref[...], ones)
        fin.append((bonus, g))
        l_hi = lw.astype(BF16)
        l_r1 = lw - l_hi.astype(F32)
        l_mid = l_r1.astype(BF16)
        l_lo = (l_r1 - l_mid.astype(F32)).astype(BF16)
        cum = _dot(tri, l_hi) + _dot(tri, l_mid) + _dot(tri, l_lo)
        cum_end = cum[c - 1:c, :]
        e_neg = jnp.exp(-cum)
        e_end = jnp.exp(cum_end - cum)
        kka = kk * a
        ah = -kk * jnp.exp(cum - lw)
        rh = r * jnp.exp(cum)
        bt = kka * e_neg
        kt = k * e_neg
        bb = kka * e_end
        kb = k * e_end
        p_end = jnp.exp(cum_end)
        for p, sl in enumerate(sls):
            for lst, z in ((ah_p, ah), (rh_p, rh), (v_p, v), (bt_p, bt), (kt_p, kt),
                           (bb_p, bb), (kb_p, kb), (pe_p, p_end)):
                lst.append(z[:, sl])
            st_idx.append((bi, p))

    t_idx = _iota((c, 2 * c), 0)
    s_idx = _iota((c, 2 * c), 1) & (c - 1)
    strict = t_idx > s_idx
    incl = t_idx >= s_idx
    eye_pack = (t_idx == s_idx).astype(F32)
    bd_mask = (_iota((LANES, LANES), 0) >> 6) == (_iota((LANES, LANES), 1) >> 6)
    eye_l = _iota((LANES, LANES), 0) == _iota((LANES, LANES), 1)
    zeros_c = jnp.zeros((c, LANES), F32)

    units = range(len(st_idx))
    v_blk = [_lane_halves(z, R_HEAD) for z in v_p]
    a_all = [_mm(jnp.concatenate([ah_p[u], rh_p[u]], axis=0),
                 jnp.concatenate([_lane_halves(bt_p[u], R_HEAD), _lane_halves(kt_p[u], R_HEAD)],
                                 axis=0), _dot_nt) for u in units]
    a_ab = [jnp.where(strict, z[0:c, 0:2 * c], 0.0) for z in a_all]
    a_ak = [jnp.where(strict, z[0:c, 2 * c:4 * c], 0.0) for z in a_all]
    a_rb = [jnp.where(incl, z[c:2 * c, 0:2 * c], 0.0) for z in a_all]
    a_rk = [jnp.where(incl, z[c:2 * c, 2 * c:4 * c], 0.0) for z in a_all]
    w1 = [_mm(a_ak[u], v_blk[u]) for u in units]
    quads = range(len(st_idx) // 2)
    a4 = [jnp.concatenate([a_ab[2 * q], a_ab[2 * q + 1]], axis=1) for q in quads]
    eye4 = jnp.concatenate([eye_pack, eye_pack], axis=1)
    inv4 = [eye4 + z for z in a4]
    bd = [_lane_blocks(z, c) for z in a4]
    pw = [_mm(a4[q], bd[q]) for q in quads]
    n_sq = int(math.log2(c)) - 1
    for step in range(n_sq):
        bd = [_lane_blocks(z, c) for z in pw]
        if step < n_sq - 1:
            both = [_mm(jnp.concatenate([pw[q], inv4[q]], axis=0), bd[q]) for q in quads]
            pw = [z[0:c] for z in both]
            inv4 = [inv4[q] + both[q][c:2 * c] for q in quads]
        else:
            inv4 = [inv4[q] + _mm(inv4[q], bd[q]) for q in quads]
    inv = [inv4[u // 2][:, (u % 2) * 2 * c:(u % 2 + 1) * 2 * c] for u in units]
    t4 = [_mm(inv[u], jnp.concatenate([_lane_halves(w1[u], R_HEAD),
                                       _lane_halves(ah_p[u], R_HEAD)], axis=1)) for u in units]
    wu = [z[:, 0:LANES] for z in t4]
    ah2 = [z[:, LANES:] for z in t4]
    zeros_blk = jnp.zeros((2 * c, LANES), F32)
    t5 = [_mm(jnp.concatenate([a_rb[u], a_rk[u]], axis=1),
              jnp.concatenate([
                  jnp.concatenate([_lane_halves(wu[u], R_HEAD), _lane_halves(ah2[u], R_HEAD)],
                                  axis=1),
                  jnp.concatenate([v_blk[u], zeros_blk], axis=1)], axis=0)) for u in units]
    t6 = [_mm(jnp.concatenate([bb_p[u], kb_p[u]], axis=0),
              jnp.concatenate([jnp.concatenate([wu[u], ah2[u]], axis=1),
                               jnp.concatenate([v_p[u], zeros_c], axis=1)], axis=0), _dot_tn)
          for u in units]
    st = [st_ref[bi, p] for bi, p in st_idx]
    ys = [_mm(rh_p[u] + t5[u][:, LANES:], st[u]) + t5[u][:, 0:LANES] for u in units]
    for u in units:
        h_new = jnp.where(bd_mask, t6[u][:, 0:LANES], 0.0)
        g_x = jnp.where(bd_mask, t6[u][:, LANES:], 0.0)
        p_col = jnp.sum(jnp.where(eye_l, pe_p[u], 0.0), axis=1, keepdims=True)
        st_ref[st_idx[u]] = p_col * st[u] + _mm(g_x, st[u]) + h_new

    for bi in range(nb):
        y = jnp.concatenate(ys[bi * R_PAIRS:(bi + 1) * R_PAIRS], axis=1)
        bonus, g = fin[bi]
        y_ref[bi] = _rwkv_finish(y, bonus, g, gnw_ref[...], gnb_ref[...], ones)


def _rwkv_prompt(rc3, params, chunk, nb):
    b, t, _ = rc3.shape
    const = lambda bi, j: (0, 0)
    specs = [pl.BlockSpec(p.shape, const) for p in params]
    return pl.pallas_call(
        functools.partial(_rwkv_chunk_kernel, chunk=chunk, nb=nb),
        grid=(b // nb, t // chunk),
        in_specs=[pl.BlockSpec((nb, chunk, R_SHIFT_W), lambda bi, j: (bi, j, 0))] + specs,
        out_specs=[
            pl.BlockSpec((nb, chunk, R_WIDTH), lambda bi, j: (bi, j, 0)),
            pl.BlockSpec((nb, R_PAIRS, LANES, LANES), lambda bi, j: (bi, 0, 0, 0)),
            pl.BlockSpec((nb, 1, R_SHIFT_W), lambda bi, j: (bi, 0, 0)),
        ],
        out_shape=(
            jax.ShapeDtypeStruct((b, t, R_WIDTH), BF16),
            jax.ShapeDtypeStruct((b, R_PAIRS, LANES, LANES), F32),
            jax.ShapeDtypeStruct((b, 1, R_SHIFT_W), F32),
        ),
        scratch_shapes=[pltpu.VMEM((nb, 1, R_SHIFT_W), F32)],
        compiler_params=pltpu.CompilerParams(
            dimension_semantics=("arbitrary", "arbitrary"), vmem_limit_bytes=V7X_VMEM_LIMIT),
        name="rwkv_chunk",
    )(rc3, *params)


def _rwkv_step_kernel(x_ref, prev_ref, st_in_ref, mu_ref, w0_ref, w2e_ref, a0_ref, a2e_ref,
                      g2_ref, kk_ref, ka_ref, rk_ref, gnw_ref, gnb_ref, ones_ref,
                      y_ref, st_ref, yrow_ref, *, rows):
    ones = ones_ref[...]
    r, lw, k, v, kk, a, g, bonus = _rwkv_prep(
        x_ref[...], prev_ref[...], mu_ref[...], w0_ref[...], w2e_ref[...], a0_ref[...],
        a2e_ref[...], g2_ref[...], kk_ref[...], ka_ref[...], rk_ref[...], ones)
    vecs = (jnp.exp(lw), kk * a, k, v, kk, r)
    eye = _iota((R_HEAD, R_HEAD), 0) == _iota((R_HEAD, R_HEAD), 1)

    heads = range(R_HEADS)
    for bi in range(rows):
        w_r, b_r, k_r, v_r, kk_r, r_r = (
            [z[bi:bi + 1, h * R_HEAD:(h + 1) * R_HEAD] for h in heads] for z in vecs)
        s_old = [st_in_ref[bi, h] for h in heads]
        sa = [-jnp.sum(s_old[h] * kk_r[h], axis=1, keepdims=True) for h in heads]
        v_c = [jnp.sum(jnp.where(eye, v_r[h], 0.0), axis=1, keepdims=True) for h in heads]
        s_new = [s_old[h] * w_r[h] + sa[h] * b_r[h] + v_c[h] * k_r[h] for h in heads]
        for h in heads:
            st_ref[bi, h] = s_new[h]
        y_c = [jnp.sum(s_new[h] * r_r[h], axis=1, keepdims=True) for h in heads]
        y_r = [jnp.sum(jnp.where(eye, y_c[h], 0.0), axis=0, keepdims=True) for h in heads]
        yrow_ref[bi:bi + 1, :] = jnp.concatenate(y_r, axis=1)
    y_ref[...] = _rwkv_finish(yrow_ref[...], bonus, g, gnw_ref[...], gnb_ref[...], ones)


def _rwkv_step(rc, shift_prev, state, params, rows):
    db = rc.shape[0]
    const = lambda i: (0, 0)
    specs = [pl.BlockSpec(p.shape, const) for p in params]
    state_spec = pl.BlockSpec((rows, R_HEADS, R_HEAD, R_HEAD), lambda i: (i, 0, 0, 0))
    return pl.pallas_call(
        functools.partial(_rwkv_step_kernel, rows=rows),
        grid=(db // rows,),
        in_specs=[
            pl.BlockSpec((rows, R_SHIFT_W), lambda i: (i, 0)),
            pl.BlockSpec((rows, R_SHIFT_W), lambda i: (i, 0)),
            state_spec,
        ] + specs,
        out_specs=[pl.BlockSpec((rows, R_WIDTH), lambda i: (i, 0)), state_spec],
        out_shape=(
            jax.ShapeDtypeStruct((db, R_WIDTH), BF16),
            jax.ShapeDtypeStruct((db, R_HEADS, R_HEAD, R_HEAD), F32),
        ),
        scratch_shapes=[pltpu.VMEM((rows, R_WIDTH), F32)],
        compiler_params=pltpu.CompilerParams(
            dimension_semantics=("arbitrary",), vmem_limit_bytes=V7X_VMEM_LIMIT),
        name="rwkv_step",
    )(rc, shift_prev, state, *params)


def _pairs_to_state(st):
    b = st.shape[0]
    blocks = st.reshape(b, R_PAIRS, 2, R_HEAD, 2, R_HEAD)
    diag = jnp.stack([blocks[:, :, 0, :, 0, :], blocks[:, :, 1, :, 1, :]], axis=2)
    return jnp.swapaxes(diag.reshape(b, R_HEADS, R_HEAD, R_HEAD), -1, -2)


def _merge_ffn_kernel(x_ref, o_ref, rw_ref, g_ref, wpa_ref, wpb_ref, wout_ref, nf_ref,
                      wg_ref, wu_ref, wd_ref, y_ref, *, d_ff):
    ya = _dot(o_ref[...].astype(BF16), wpa_ref[...])
    yb = _dot(rw_ref[...], wpb_ref[...])
    merged = (g_ref[:, 0:D_MODEL] * ya + g_ref[:, D_MODEL:] * yb).astype(BF16)
    x1 = x_ref[...] + _dot(merged, wout_ref[...])
    hf = _rms_rows(x1, nf_ref[...]).astype(BF16)
    acc = x1
    step = 512
    for c in range(0, d_ff, step):
        n = min(step, d_ff - c)
        gate = _dot(hf, wg_ref[:, c:c + n])
        up = _dot(hf, wu_ref[:, c:c + n])
        act = (gate * _sigmoid(gate) * up).astype(BF16)
        acc = acc + _dot(act, wd_ref[c:c + n, :])
    y_ref[...] = acc


def _merge_ffn(x2, o, rw, gates, wpa, wpb, wout, nf, wg, wu, wd, tm):
    m = x2.shape[0]
    d_ff = wg.shape[1]
    row = lambda i: (i, 0)
    const = lambda i: (0, 0)
    return pl.pallas_call(
        functools.partial(_merge_ffn_kernel, d_ff=d_ff),
        grid=(m // tm,),
        in_specs=[
            pl.BlockSpec((tm, D_MODEL), row),
            pl.BlockSpec((tm, A_WIDTH), row),
            pl.BlockSpec((tm, R_WIDTH), row),
            pl.BlockSpec((tm, 2 * D_MODEL), row),
            pl.BlockSpec(wpa.shape, const),
            pl.BlockSpec(wpb.shape, const),
            pl.BlockSpec(wout.shape, const),
            pl.BlockSpec((1, D_MODEL), const),
            pl.BlockSpec(wg.shape, const),
            pl.BlockSpec(wu.shape, const),
            pl.BlockSpec(wd.shape, const),
        ],
        out_specs=pl.BlockSpec((tm, D_MODEL), row),
        out_shape=jax.ShapeDtypeStruct((m, D_MODEL), F32),
        compiler_params=pltpu.CompilerParams(
            dimension_semantics=("arbitrary",), vmem_limit_bytes=V7X_VMEM_LIMIT),
        name="merge_ffn",
    )(x2, o, rw, gates, wpa, wpb, wout, nf, wg, wu, wd)


def _row(v):
    return v.reshape(1, -1).astype(F32)


def _pick_tile(m, target):
    t = min(m, target)
    while m % t:
        t //= 2
    return t


def kernel(x_prompt, x_sample, cache_k, cache_v, page_table, state_wkv, state_shift, norm_mix, w_in, q_gain, k_gain, lambda_q1, lambda_k1, lambda_q2, lambda_k2, attn_out_gain, w_pa, shift_mu, w0, w2, a0, a2, g2, k_k, k_a, r_k, gn_w, gn_b, w_pb, w_out, norm_ffn, w_gate, w_up, w_down):
    b, t, _ = x_prompt.shape
    db, ds, _ = x_sample.shape
    assert ds == 1, "sample group carries one new token per sequence"
    depth = w_in.shape[0]
    n_pages = page_table.shape[1]
    xp = x_prompt.reshape(b * t, D_MODEL)
    xs = x_sample.reshape(db, D_MODEL)
    head_ones = (jnp.arange(A_WIDTH)[:, None] // A_HEAD_DIM
                 == jnp.arange(A_WIDTH)[None, :] // A_HEAD_DIM).astype(BF16)
    outs = [[] for _ in range(8)]
    tm_p = _pick_tile(b * t, 512)
    tm_s = _pick_tile(db, 512)
    tq = _pick_tile(t, 512)
    chunk = _pick_tile(t, RWKV_CHUNK)
    chunk_nb = _pick_tile(b, 4)
    pps = _pick_tile(n_pages, 16)
    step_rows = _pick_tile(db, 16)
    for l in range(depth):
        lambda_init = 0.8 - 0.6 * math.exp(-0.3 * l)
        w_in_bf = w_in[l].astype(BF16)
        qg = _row(jnp.tile(q_gain[l], A_WIDTH // A_HEAD_DIM))
        kg = _row(jnp.tile(k_gain[l], A_WIDTH // A_HEAD_DIM))
        nm = _row(norm_mix[l])
        lam_vecs = (_row(lambda_q1[l]), _row(lambda_k1[l]), _row(lambda_q2[l]), _row(lambda_k2[l]))
        og = _row(attn_out_gain[l])
        zeros_lora = jnp.zeros((R_LORA_W, R_WIDTH), F32)
        rwkv_params = (
            _row(shift_mu[l]), _row(w0[l]),
            jnp.concatenate([w2[l], zeros_lora], axis=0).astype(BF16),
            _row(a0[l]),
            jnp.concatenate([zeros_lora, a2[l]], axis=0).astype(BF16),
            g2[l].astype(BF16), _row(k_k[l]), _row(k_a[l]), _row(r_k[l]),
            _row(gn_w[l]), _row(gn_b[l]), head_ones)
        ffn_w = (w_pa[l].astype(BF16), w_pb[l].astype(BF16), w_out[l].astype(BF16),
                 _row(norm_ffn[l]), w_gate[l].astype(BF16), w_up[l].astype(BF16),
                 w_down[l].astype(BF16))

        qb, k32, kb, v32, vb, rc, gates = _proj(xp, nm, w_in_bf, qg, kg, head_ones, tm_p)
        o = _prompt_attn(qb, kb, vb, *lam_vecs, og, b, t, tq, lambda_init)
        rw, st_p, sh_p = _rwkv_prompt(rc.reshape(b, t, R_SHIFT_W), rwkv_params, chunk, chunk_nb)
        xp = _merge_ffn(xp, o, rw.reshape(b * t, R_WIDTH), gates, *ffn_w, tm_p)
        outs[0].append(k32.reshape(b, t, A_HEADS, A_QK_DIM))
        outs[1].append(v32.reshape(b, t, A_HEADS, A_V_DIM))
        outs[2].append(_pairs_to_state(st_p))
        outs[3].append(sh_p.reshape(b, R_SHIFT_W))

        qb, k32, kb, v32, vb, rc, gates = _proj(xs, nm, w_in_bf, qg, kg, head_ones, tm_s)
        o = _decode_attn(page_table, qb, kb, vb, cache_k, cache_v, l, *lam_vecs, og,
                         pps, lambda_init)
        rw, st_s = _rwkv_step(rc, state_shift[l], state_wkv[l], rwkv_params,
                              step_rows)
        xs = _merge_ffn(xs, o, rw, gates, *ffn_w, tm_s)
        outs[4].append(k32.reshape(db, 1, A_HEADS, A_QK_DIM))
        outs[5].append(v32.reshape(db, 1, A_HEADS, A_V_DIM))
        outs[6].append(st_s)
        outs[7].append(rc)
    return (xp.reshape(b, t, D_MODEL), xs.reshape(db, 1, D_MODEL),
            *(jnp.stack(o) for o in outs))
```

```python
import functools
import math

import jax
import jax.numpy as jnp
from jax import lax
from jax.experimental import pallas as pl
from jax.experimental.pallas import tpu as pltpu

F32 = jnp.float32
BF16 = jnp.bfloat16

D_MODEL = 1024
PAGE_SIZE = 128
A_HEADS = 4
A_HEAD_DIM = 64
A_QK_DIM = 2 * A_HEAD_DIM
A_V_DIM = 2 * A_HEAD_DIM
A_WIDTH = A_HEADS * A_V_DIM
R_HEAD = 64
R_HEADS = 8
R_WIDTH = R_HEADS * R_HEAD
R_LORA_W = 64
R_LORA_A = 64
R_LORA_G = 128
R_SHIFT_W = 3 * R_WIDTH + R_LORA_W + R_LORA_A + R_LORA_G
OFF_Q = 0
OFF_K = OFF_Q + A_HEADS * A_QK_DIM
OFF_V = OFF_K + A_HEADS * A_QK_DIM
OFF_R = OFF_V + A_WIDTH
OFF_G = OFF_R + R_SHIFT_W
IN_WIDTH = OFF_G + 2 * D_MODEL
NORM_EPS = 1e-6
GN_EPS = 64e-5
NEG_BIG = -1e30

V7X_VMEM_LIMIT = 56 * 1024 * 1024
LANES = 128
R_PAIRS = R_WIDTH // LANES
RWKV_CHUNK = 64


def _dot(a, b):
    return jnp.dot(a, b, preferred_element_type=F32)


def _dot_nt(a, b):
    return lax.dot_general(a, b, (((1,), (1,)), ((), ())), preferred_element_type=F32)


def _dot_tn(a, b):
    return lax.dot_general(a, b, (((0,), (0,)), ((), ())), preferred_element_type=F32)


def _mm(a, b, dot=_dot):
    return dot(a.astype(BF16), b.astype(BF16))


def _sigmoid(x):
    return 1.0 / (1.0 + jnp.exp(-x))


def _iota(shape, dim):
    return lax.broadcasted_iota(jnp.int32, shape, dim)


def _group_sum(z, ones_bf):
    return _dot(z.astype(BF16), ones_bf)


def _rms_rows(x, gain_row):
    ms = jnp.mean(x * x, axis=-1, keepdims=True)
    return x * lax.rsqrt(ms + NORM_EPS) * gain_row


def _proj_kernel(x_ref, nm_ref, w_ref, qg_ref, kg_ref, ones_ref,
                 q_ref, k_ref, kb_ref, v_ref, vb_ref, r_ref, g_ref):
    h = _rms_rows(x_ref[...], nm_ref[...]).astype(BF16)
    ones = ones_ref[...]

    def head_norm(z, gain_row):
        ss = _group_sum(z * z, ones)
        return z * lax.rsqrt(ss * (1.0 / A_HEAD_DIM) + NORM_EPS) * gain_row

    q = head_norm(_dot(h, w_ref[:, OFF_Q:OFF_K]), qg_ref[...])
    q_ref[...] = (q * (A_HEAD_DIM ** -0.5)).astype(BF16)
    tm = x_ref.shape[0]
    k = head_norm(_dot(h, w_ref[:, OFF_K:OFF_V]), kg_ref[...])
    kb_ref[...] = k.astype(BF16)
    v = _dot(h, w_ref[:, OFF_V:OFF_R])
    vb_ref[...] = v.astype(BF16)
    for hd in range(A_HEADS):
        sl = slice(hd * A_QK_DIM, (hd + 1) * A_QK_DIM)
        k_ref[pl.ds(hd, tm, stride=A_HEADS), :] = k[:, sl]
        v_ref[pl.ds(hd, tm, stride=A_HEADS), :] = v[:, sl]
    step = 512
    for c in range(0, R_SHIFT_W, step):
        n = min(step, R_SHIFT_W - c)
        r_ref[:, c:c + n] = _dot(h, w_ref[:, OFF_R + c:OFF_R + c + n])
    for c in range(0, 2 * D_MODEL, step):
        g_ref[:, c:c + step] = _sigmoid(_dot(h, w_ref[:, OFF_G + c:OFF_G + c + step]))


def _proj(x2, nm, w_in_bf, qg, kg, ones_bf, tm):
    m = x2.shape[0]
    row = lambda i: (i, 0)
    const = lambda i: (0, 0)
    out_shape = (
        jax.ShapeDtypeStruct((m, A_WIDTH), BF16),
        jax.ShapeDtypeStruct((m * A_HEADS, A_QK_DIM), F32),
        jax.ShapeDtypeStruct((m, A_WIDTH), BF16),
        jax.ShapeDtypeStruct((m * A_HEADS, A_V_DIM), F32),
        jax.ShapeDtypeStruct((m, A_WIDTH), BF16),
        jax.ShapeDtypeStruct((m, R_SHIFT_W), F32),
        jax.ShapeDtypeStruct((m, 2 * D_MODEL), F32),
    )
    return pl.pallas_call(
        _proj_kernel,
        grid=(m // tm,),
        in_specs=[
            pl.BlockSpec((tm, D_MODEL), row),
            pl.BlockSpec((1, D_MODEL), const),
            pl.BlockSpec((D_MODEL, IN_WIDTH), const),
            pl.BlockSpec((1, A_WIDTH), const),
            pl.BlockSpec((1, A_WIDTH), const),
            pl.BlockSpec((A_WIDTH, A_WIDTH), const),
        ],
        out_specs=[
            pl.BlockSpec((tm, A_WIDTH), row),
            pl.BlockSpec((tm * A_HEADS, A_QK_DIM), row),
            pl.BlockSpec((tm, A_WIDTH), row),
            pl.BlockSpec((tm * A_HEADS, A_V_DIM), row),
            pl.BlockSpec((tm, A_WIDTH), row),
            pl.BlockSpec((tm, R_SHIFT_W), row),
            pl.BlockSpec((tm, 2 * D_MODEL), row),
        ],
        out_shape=out_shape,
        compiler_params=pltpu.CompilerParams(
            dimension_semantics=("arbitrary",), vmem_limit_bytes=V7X_VMEM_LIMIT),
        name="proj",
    )(x2, nm, w_in_bf, qg, kg, ones_bf)


def _lambda_full(lq1, lk1, lq2, lk2, lambda_init):
    s1 = jnp.sum(lq1 * lk1, axis=-1, keepdims=True)
    s2 = jnp.sum(lq2 * lk2, axis=-1, keepdims=True)
    return jnp.exp(s1) - jnp.exp(s2) + lambda_init


ATTN_ACC_ROWS = A_V_DIM + 16
ATTN_HEADS_PER_STEP = 1


def _alibi_lanes(pos, slope, keys):
    l4 = _iota(pos.shape, 1) & (A_HEAD_DIM - 1)
    lo = slope * (pos & 255).astype(F32)
    hi = slope * ((pos >> 8) << 8).astype(F32)
    if keys:
        val = jnp.where(l4 < 2, 1.0, jnp.where(l4 == 2, lo, jnp.where(l4 == 3, hi, 0.0)))
    else:
        val = jnp.where(l4 == 0, -lo, jnp.where(l4 == 1, -hi, jnp.where(l4 < 4, 1.0, 0.0)))
    return val.astype(BF16)


def _prompt_attn_kernel(q_ref, k_ref, v_ref, lq1_ref, lk1_ref, lq2_ref, lk2_ref, gain_ref,
                        o_ref, kc_ref, vt_ref, acc_ref, *, tq, lambda_init):
    hg = pl.program_id(1)
    t = k_ref.shape[0]
    hp = ATTN_HEADS_PER_STEP
    n_st = 2 * hp
    slopes = [jnp.exp2(jnp.full((1, 1), -8.0 / A_HEADS, F32) * (hg * hp + hh + 1).astype(F32))
              for hh in range(hp)]
    head_lanes = [slice(hh * A_QK_DIM, (hh + 1) * A_QK_DIM) for hh in range(hp)]
    pos = _iota((tq, A_QK_DIM), 0)
    low = _iota((tq, A_QK_DIM), 1) < A_HEAD_DIM

    ones_rows = (_iota((ATTN_ACC_ROWS - A_V_DIM, tq), 0) == 0).astype(BF16)
    for hh in range(hp):
        ek = _alibi_lanes(pos, slopes[hh], keys=True)
        for jc in range(t // tq):
            rows = slice(jc * tq, (jc + 1) * tq)
            kt = k_ref[rows, head_lanes[hh]]
            kc_ref[2 * hh, rows, :] = jnp.where(low, kt, ek)
            kc_ref[2 * hh + 1, rows, :] = jnp.where(low, ek, kt)
            vt_ref[hh, jc, 0:A_V_DIM, :] = v_ref[rows, head_lanes[hh]].astype(F32).T.astype(BF16)
            vt_ref[hh, jc, A_V_DIM:, :] = ones_rows

    eqs = [_alibi_lanes(pos, slopes[hh], keys=False) for hh in range(hp)]
    lam = _lambda_full(lq1_ref[...], lk1_ref[...], lq2_ref[...], lk2_ref[...], lambda_init)
    kv_pos = _iota((tq, tq), 0)
    q_pos = _iota((tq, tq), 1)

    def q_tile(i, carry):
        q_rows = pl.ds(pl.multiple_of(i * tq, tq), tq)
        qc = []
        for hh in range(hp):
            q = q_ref[q_rows, head_lanes[hh]]
            qc += [jnp.where(low, q, eqs[hh]), jnp.where(low, eqs[hh], q)]

        def tiles(group, ms):
            s = []
            for j, masked in group:
                rows = pl.ds(pl.multiple_of(j * tq, tq), tq)
                sj = [_dot_nt(kc_ref[st, rows, :], qc[st]) for st in range(n_st)]
                if masked:
                    sj = [jnp.where(kv_pos <= q_pos, sc, NEG_BIG) for sc in sj]
                s.append(sj)
            ms = list(ms)
            acc = [acc_ref[st] for st in range(n_st)]
            for (j, _), sj in zip(group, s):
                for st in range(n_st):
                    off = slopes[st // 2] * ((j - i) * tq).astype(F32)
                    m_rel = ms[st] - off
                    m_new = jnp.maximum(m_rel, jnp.max(sj[st], axis=0, keepdims=True))
                    p = jnp.exp(sj[st] - m_new).astype(BF16)
                    acc[st] = jnp.exp(m_rel - m_new) * acc[st] + _dot(vt_ref[st // 2, j], p)
                    ms[st] = m_new + off
            for st in range(n_st):
                acc_ref[st] = acc[st]
            return tuple(ms)

        acc_ref[...] = jnp.zeros_like(acc_ref)
        ms = lax.fori_loop(0, i // 2,
                           lambda pr, ms: tiles([(2 * pr, False), (2 * pr + 1, False)], ms),
                           tuple(jnp.full((1, tq), NEG_BIG, F32) for _ in range(n_st)))
        lax.cond((i & 1) == 1,
                 lambda ms: tiles([(i - 1, False), (i, True)], ms),
                 lambda ms: tiles([(i, True)], ms), ms)
        for hh in range(hp):
            a0, a1 = acc_ref[2 * hh], acc_ref[2 * hh + 1]
            o_t = (a0[0:A_V_DIM] / a0[A_V_DIM:A_V_DIM + 1]
                   - lam * (a1[0:A_V_DIM] / a1[A_V_DIM:A_V_DIM + 1]))
            ms_o = jnp.mean(o_t * o_t, axis=0, keepdims=True)
            o_t = o_t * lax.rsqrt(ms_o + NORM_EPS) * gain_ref[...] * (1.0 - lambda_init)
            o_ref[q_rows, head_lanes[hh]] = o_t.T.astype(BF16)
        return carry

    lax.fori_loop(0, t // tq, q_tile, 0)


def _prompt_attn(qb, kb, vb, lq1, lk1, lq2, lk2, gain, b, t, tq, lambda_init):
    nq = t // tq
    hp = ATTN_HEADS_PER_STEP
    vec = lambda bi, h: (0, 0)
    seq_head = lambda bi, h: (bi, h)
    return pl.pallas_call(
        functools.partial(_prompt_attn_kernel, tq=tq, lambda_init=lambda_init),
        grid=(b, A_HEADS // hp),
        in_specs=[
            pl.BlockSpec((t, hp * A_QK_DIM), seq_head),
            pl.BlockSpec((t, hp * A_QK_DIM), seq_head),
            pl.BlockSpec((t, hp * A_V_DIM), seq_head),
            pl.BlockSpec((1, A_HEAD_DIM), vec),
            pl.BlockSpec((1, A_HEAD_DIM), vec),
            pl.BlockSpec((1, A_HEAD_DIM), vec),
            pl.BlockSpec((1, A_HEAD_DIM), vec),
            pl.BlockSpec((A_V_DIM, 1), vec),
        ],
        out_specs=pl.BlockSpec((t, hp * A_V_DIM), seq_head),
        out_shape=jax.ShapeDtypeStruct((b * t, A_WIDTH), BF16),
        scratch_shapes=[pltpu.VMEM((2 * hp, t, A_QK_DIM), BF16),
                        pltpu.VMEM((hp, nq, ATTN_ACC_ROWS, tq), BF16),
                        pltpu.VMEM((2 * hp, ATTN_ACC_ROWS, tq), F32)],
        compiler_params=pltpu.CompilerParams(
            dimension_semantics=("arbitrary", "arbitrary"),
            vmem_limit_bytes=V7X_VMEM_LIMIT),
        name="prompt_attn",
    )(qb, kb, vb, lq1, lk1, lq2, lk2, gain.reshape(A_V_DIM, 1))


def _decode_attn_kernel(pt_ref, q_ref, kn_ref, vn_ref, lq1_ref, lk1_ref, lq2_ref, lk2_ref,
                        gain_ref, *refs, pps, n_pages, lambda_init):
    k_refs = refs[:pps]
    v_refs = refs[pps:2 * pps]
    o_ref = refs[2 * pps]
    m_ref, l_ref, acc_ref = refs[2 * pps + 1:]
    j = pl.program_id(1)
    rows = 2 * A_HEADS
    page_rows = PAGE_SIZE * A_HEADS
    row_q = _iota((rows, A_QK_DIM), 0)
    lane_q = _iota((rows, A_QK_DIM), 1)
    qm = jnp.where((lane_q >> 6) == (row_q & 1), q_ref[0], 0.0).astype(BF16)
    slope = jnp.exp2((-8.0 / A_HEADS) * ((_iota((rows, 1), 0) >> 1) + 1).astype(F32))
    col = _iota((rows, pps * page_rows), 1)
    own_head = (col & (A_HEADS - 1)) == (_iota((rows, pps * page_rows), 0) >> 1)
    dist = (n_pages - j * pps) * PAGE_SIZE - (col >> 2)

    @pl.when(j == 0)
    def _():
        m_ref[...] = jnp.full((rows, 1), NEG_BIG, F32)
        l_ref[...] = jnp.zeros((rows, 1), F32)
        acc_ref[...] = jnp.zeros((rows, A_V_DIM), F32)

    s = jnp.concatenate([_dot_nt(qm, k_refs[u][...].astype(BF16)) for u in range(pps)], axis=1)
    s = jnp.where(own_head, s - slope * dist.astype(F32), NEG_BIG)
    m = m_ref[...]
    m_new = jnp.maximum(m, jnp.max(s, axis=-1, keepdims=True))
    alpha = jnp.exp(m - m_new)
    p = jnp.exp(s - m_new)
    pb = p.astype(BF16)
    pv = _dot(pb[:, 0:page_rows], v_refs[0][...].astype(BF16))
    for u in range(1, pps):
        pv = pv + _dot(pb[:, u * page_rows:(u + 1) * page_rows], v_refs[u][...].astype(BF16))
    m = m_new
    l = alpha * l_ref[...] + jnp.sum(p, axis=-1, keepdims=True)
    acc = alpha * acc_ref[...] + pv
    m_ref[...], l_ref[...], acc_ref[...] = m, l, acc

    @pl.when(j == pl.num_programs(1) - 1)
    def _():
        s_new = jnp.sum(qm.astype(F32) * kn_ref[0], axis=-1, keepdims=True)
        m_new = jnp.maximum(m, s_new)
        alpha = jnp.exp(m - m_new)
        p_new = jnp.exp(s_new - m_new)
        l_fin = alpha * l + p_new
        acc_fin = alpha * acc + p_new.astype(BF16).astype(F32) * vn_ref[0]
        lam = _lambda_full(lq1_ref[...], lk1_ref[...], lq2_ref[...], lk2_ref[...], lambda_init)
        comp = _iota((rows, 1), 0) & 1
        x = acc_fin * (jnp.where(comp == 0, 1.0, -lam) / l_fin)
        o = x + pltpu.roll(x, shift=rows - 1, axis=0)
        o_ref[0] = _rms_rows(o, gain_ref[...]) * (1.0 - lambda_init)


def _decode_attn(page_table, qb, kb, vb, cache_k, cache_v, layer, lq1, lk1, lq2, lk2, gain,
                 pps, lambda_init):
    db, n_pages = page_table.shape
    n_pool = cache_k.shape[1]
    page_rows = PAGE_SIZE * A_HEADS
    ck = cache_k.reshape(-1, A_QK_DIM)
    cv = cache_v.reshape(-1, A_V_DIM)
    pt = page_table.reshape(db * n_pages)
    rows = 2 * A_HEADS
    tok3 = lambda b, j, pt_ref: (b, 0, 0)
    vec = lambda b, j, pt_ref: (0, 0)

    def page_spec(u):
        return pl.BlockSpec(
            (page_rows, A_QK_DIM),
            lambda b, j, pt_ref: (layer * n_pool + pt_ref[b * n_pages + j * pps + u], 0))

    def per_row(z):
        return jnp.repeat(z.astype(F32).reshape(db, A_HEADS, A_QK_DIM), 2, axis=1)

    grid_spec = pltpu.PrefetchScalarGridSpec(
        num_scalar_prefetch=1,
        grid=(db, n_pages // pps),
        in_specs=[
            pl.BlockSpec((1, rows, A_QK_DIM), tok3),
            pl.BlockSpec((1, rows, A_QK_DIM), tok3),
            pl.BlockSpec((1, rows, A_V_DIM), tok3),
            pl.BlockSpec((1, A_HEAD_DIM), vec),
            pl.BlockSpec((1, A_HEAD_DIM), vec),
            pl.BlockSpec((1, A_HEAD_DIM), vec),
            pl.BlockSpec((1, A_HEAD_DIM), vec),
            pl.BlockSpec((1, A_V_DIM), vec),
        ] + [page_spec(u) for u in range(pps)] + [page_spec(u) for u in range(pps)],
        out_specs=pl.BlockSpec((1, rows, A_V_DIM), tok3),
        scratch_shapes=[pltpu.VMEM((rows, 1), F32), pltpu.VMEM((rows, 1), F32),
                        pltpu.VMEM((rows, A_V_DIM), F32)],
    )
    out = pl.pallas_call(
        functools.partial(_decode_attn_kernel, pps=pps, n_pages=n_pages, lambda_init=lambda_init),
        grid_spec=grid_spec,
        out_shape=jax.ShapeDtypeStruct((db, rows, A_V_DIM), F32),
        compiler_params=pltpu.CompilerParams(
            dimension_semantics=("arbitrary", "arbitrary"), vmem_limit_bytes=V7X_VMEM_LIMIT),
        name="decode_attn",
    )(pt, per_row(qb), per_row(kb), per_row(vb),
      lq1, lk1, lq2, lk2, gain, *([ck] * pps), *([cv] * pps))
    return out[:, 0::2, :].reshape(db, A_WIDTH)


def _rwkv_prep(x, prev, mu, w0, w2e, a0, a2e, g2, k_k, k_a, r_k, ones):
    m = x + (prev - x) * mu
    r = m[:, 0:R_WIDTH]
    k = m[:, R_WIDTH:2 * R_WIDTH]
    v = m[:, 2 * R_WIDTH:3 * R_WIDTH]
    xwa = m[:, 3 * R_WIDTH:3 * R_WIDTH + R_LORA_W + R_LORA_A]
    xg = m[:, 3 * R_WIDTH + R_LORA_W + R_LORA_A:]
    z = w0 + _dot(jnp.tanh(xwa).astype(BF16), w2e)
    lw = (-math.exp(-0.5)) * _sigmoid(z)
    a = _sigmoid(a0 + _dot(xwa.astype(BF16), a2e))
    g = _dot(_sigmoid(xg).astype(BF16), g2)
    kk = k * k_k
    kk = kk * lax.rsqrt(jnp.maximum(_group_sum(kk * kk, ones), 1e-24))
    k = k * (1.0 + (a - 1.0) * k_a)
    bonus = _group_sum(r * k * r_k, ones) * v
    return r, lw, k, v, kk, a, g, bonus


def _rwkv_finish(y, bonus, g, gn_w, gn_b, ones):
    mu = _group_sum(y, ones) * (1.0 / R_HEAD)
    d = y - mu
    var = _group_sum(d * d, ones) * (1.0 / R_HEAD)
    yn = d * lax.rsqrt(var + GN_EPS) * gn_w + gn_b
    return ((yn + bonus) * g).astype(BF16)


def _lane_halves(x, width):
    lane = _iota(x.shape, 1)
    zero = jnp.zeros_like(x)
    return jnp.concatenate([jnp.where(lane < width, x, zero), jnp.where(lane >= width, x, zero)],
                           axis=0)


def _lane_blocks(x, width):
    block = _iota(x.shape, 1) >> int(math.log2(width))
    zero = jnp.zeros_like(x)
    return jnp.concatenate([jnp.where(block == b, x, zero) for b in range(x.shape[1] // width)],
                           axis=0)


def _rwkv_chunk_kernel(x_ref, mu_ref, w0_ref, w2e_ref, a0_ref, a2e_ref, g2_ref, kk_ref, ka_ref,
                       rk_ref, gnw_ref, gnb_ref, ones_ref,
                       y_ref, st_ref, sh_ref, prev_ref, *, chunk, nb):
    c = chunk
    j = pl.program_id(1)

    @pl.when(j == 0)
    def _():
        prev_ref[...] = jnp.zeros_like(prev_ref)
        st_ref[...] = jnp.zeros_like(st_ref)

    ones = ones_ref[...]
    rows = _iota((c, R_SHIFT_W), 0)
    tri = (_iota((c, c), 0) >= _iota((c, c), 1)).astype(BF16)
    sls = [slice(p * LANES, (p + 1) * LANES) for p in range(R_PAIRS)]
    ah_p, rh_p, v_p, bt_p, kt_p, bb_p, kb_p, pe_p, st_idx, fin = ([] for _ in range(10))
    for bi in range(nb):
        x = x_ref[bi]
        prev = jnp.where(rows == 0, prev_ref[bi], pltpu.roll(x, shift=1, axis=0))
        prev_ref[bi] = x[c - 1:c, :]
        sh_ref[bi] = x[c - 1:c, :]
        r, lw, k, v, kk, a, g, bonus = _rwkv_prep(
            x, prev, mu_ref[...], w0_ref[...], w2e_ref[...], a0_ref[...], a2e_ref[...],
            g2_ref[...], kk_ref[...], ka_ref[...], rk_ref[...], ones)
        fin.append((bonus, g))
        l_hi = lw.astype(BF16)
        l_r1 = lw - l_hi.astype(F32)
        l_mid = l_r1.astype(BF16)
        l_lo = (l_r1 - l_mid.astype(F32)).astype(BF16)
        cum = _dot(tri, l_hi) + _dot(tri, l_mid) + _dot(tri, l_lo)
        cum_end = cum[c - 1:c, :]
        e_neg = jnp.exp(-cum)
        e_end = jnp.exp(cum_end - cum)
        kka = kk * a
        ah = -kk * jnp.exp(cum - lw)
        rh = r * jnp.exp(cum)
        bt = kka * e_neg
        kt = k * e_neg
        bb = kka * e_end
        kb = k * e_end
        p_end = jnp.exp(cum_end)
        for p, sl in enumerate(sls):
            for lst, z in ((ah_p, ah), (rh_p, rh), (v_p, v), (bt_p, bt), (kt_p, kt),
                           (bb_p, bb), (kb_p, kb), (pe_p, p_end)):
                lst.append(z[:, sl])
            st_idx.append((bi, p))

    t_idx = _iota((c, 2 * c), 0)
    s_idx = _iota((c, 2 * c), 1) & (c - 1)
    strict = t_idx > s_idx
    incl = t_idx >= s_idx
    eye_pack = (t_idx == s_idx).astype(F32)
    bd_mask = (_iota((LANES, LANES), 0) >> 6) == (_iota((LANES, LANES), 1) >> 6)
    eye_l = _iota((LANES, LANES), 0) == _iota((LANES, LANES), 1)
    zeros_c = jnp.zeros((c, LANES), F32)

    units = range(len(st_idx))
    v_blk = [_lane_halves(z, R_HEAD) for z in v_p]
    a_all = [_mm(jnp.concatenate([ah_p[u], rh_p[u]], axis=0),
                 jnp.concatenate([_lane_halves(bt_p[u], R_HEAD), _lane_halves(kt_p[u], R_HEAD)],
                                 axis=0), _dot_nt) for u in units]
    a_ab = [jnp.where(strict, z[0:c, 0:2 * c], 0.0) for z in a_all]
    a_ak = [jnp.where(strict, z[0:c, 2 * c:4 * c], 0.0) for z in a_all]
    a_rb = [jnp.where(incl, z[c:2 * c, 0:2 * c], 0.0) for z in a_all]
    a_rk = [jnp.where(incl, z[c:2 * c, 2 * c:4 * c], 0.0) for z in a_all]
    w1 = [_mm(a_ak[u], v_blk[u]) for u in units]
    quads = range(len(st_idx) // 2)
    a4 = [jnp.concatenate([a_ab[2 * q], a_ab[2 * q + 1]], axis=1) for q in quads]
    eye4 = jnp.concatenate([eye_pack, eye_pack], axis=1)
    inv4 = [eye4 + z for z in a4]
    bd = [_lane_blocks(z, c) for z in a4]
    pw = [_mm(a4[q], bd[q]) for q in quads]
    n_sq = int(math.log2(c)) - 1
    for step in range(n_sq):
        bd = [_lane_blocks(z, c) for z in pw]
        if step < n_sq - 1:
            both = [_mm(jnp.concatenate([pw[q], inv4[q]], axis=0), bd[q]) for q in quads]
            pw = [z[0:c] for z in both]
            inv4 = [inv4[q] + both[q][c:2 * c] for q in quads]
        else:
            inv4 = [inv4[q] + _mm(inv4[q], bd[q]) for q in quads]
    inv = [inv4[u // 2][:, (u % 2) * 2 * c:(u % 2 + 1) * 2 * c] for u in units]
    t4 = [_mm(inv[u], jnp.concatenate([_lane_halves(w1[u], R_HEAD),
                                       _lane_halves(ah_p[u], R_HEAD)], axis=1)) for u in units]
    wu = [z[:, 0:LANES] for z in t4]
    ah2 = [z[:, LANES:] for z in t4]
    zeros_blk = jnp.zeros((2 * c, LANES), F32)
    t5 = [_mm(jnp.concatenate([a_rb[u], a_rk[u]], axis=1),
              jnp.concatenate([
                  jnp.concatenate([_lane_halves(wu[u], R_HEAD), _lane_halves(ah2[u], R_HEAD)],
                                  axis=1),
                  jnp.concatenate([v_blk[u], zeros_blk], axis=1)], axis=0)) for u in units]
    t6 = [_mm(jnp.concatenate([bb_p[u], kb_p[u]], axis=0),
              jnp.concatenate([jnp.concatenate([wu[u], ah2[u]], axis=1),
                               jnp.concatenate([v_p[u], zeros_c], axis=1)], axis=0), _dot_tn)
          for u in units]
    st = [st_ref[bi, p] for bi, p in st_idx]
    ys = [_mm(rh_p[u] + t5[u][:, LANES:], st[u]) + t5[u][:, 0:LANES] for u in units]
    for u in units:
        h_new = jnp.where(bd_mask, t6[u][:, 0:LANES], 0.0)
        g_x = jnp.where(bd_mask, t6[u][:, LANES:], 0.0)
        p_col = jnp.sum(jnp.where(eye_l, pe_p[u], 0.0), axis=1, keepdims=True)
        st_ref[st_idx[u]] = p_col * st[u] + _mm(g_x, st[u]) + h_new

    for bi in range(nb):
        y = jnp.concatenate(ys[bi * R_PAIRS:(bi + 1) * R_PAIRS], axis=1)
        bonus, g = fin[bi]
        y_ref[bi] = _rwkv_finish(y, bonus, g, gnw_ref[...], gnb_ref[...], ones)


def _rwkv_prompt(rc3, params, chunk, nb):
    b, t, _ = rc3.shape
    const = lambda bi, j: (0, 0)
    specs = [pl.BlockSpec(p.shape, const) for p in params]
    return pl.pallas_call(
        functools.partial(_rwkv_chunk_kernel, chunk=chunk, nb=nb),
        grid=(b // nb, t // chunk),
        in_specs=[pl.BlockSpec((nb, chunk, R_SHIFT_W), lambda bi, j: (bi, j, 0))] + specs,
        out_specs=[
            pl.BlockSpec((nb, chunk, R_WIDTH), lambda bi, j: (bi, j, 0)),
            pl.BlockSpec((nb, R_PAIRS, LANES, LANES), lambda bi, j: (bi, 0, 0, 0)),
            pl.BlockSpec((nb, 1, R_SHIFT_W), lambda bi, j: (bi, 0, 0)),
        ],
        out_shape=(
            jax.ShapeDtypeStruct((b, t, R_WIDTH), BF16),
            jax.ShapeDtypeStruct((b, R_PAIRS, LANES, LANES), F32),
            jax.ShapeDtypeStruct((b, 1, R_SHIFT_W), F32),
        ),
        scratch_shapes=[pltpu.VMEM((nb, 1, R_SHIFT_W), F32)],
        compiler_params=pltpu.CompilerParams(
            dimension_semantics=("arbitrary", "arbitrary"), vmem_limit_bytes=V7X_VMEM_LIMIT),
        name="rwkv_chunk",
    )(rc3, *params)


def _rwkv_step_kernel(x_ref, prev_ref, st_in_ref, mu_ref, w0_ref, w2e_ref, a0_ref, a2e_ref,
                      g2_ref, kk_ref, ka_ref, rk_ref, gnw_ref, gnb_ref, ones_ref,
                      y_ref, st_ref, yrow_ref, *, rows):
    ones = ones_ref[...]
    r, lw, k, v, kk, a, g, bonus = _rwkv_prep(
        x_ref[...], prev_ref[...], mu_ref[...], w0_ref[...], w2e_ref[...], a0_ref[...],
        a2e_ref[...], g2_ref[...], kk_ref[...], ka_ref[...], rk_ref[...], ones)
    vecs = (jnp.exp(lw), kk * a, k, v, kk, r)
    eye = _iota((R_HEAD, R_HEAD), 0) == _iota((R_HEAD, R_HEAD), 1)

    group = math.gcd(rows, 4)
    for g0 in range(0, rows, group):
        units = [(bi, h) for bi in range(g0, g0 + group) for h in range(R_HEADS)]
        w_r, b_r, k_r, v_r, kk_r, r_r = (
            [z[bi:bi + 1, h * R_HEAD:(h + 1) * R_HEAD] for bi, h in units] for z in vecs)
        ids = range(len(units))
        s_old = [st_in_ref[bi, h] for bi, h in units]
        sa = [-jnp.sum(s_old[u] * kk_r[u], axis=1, keepdims=True) for u in ids]
        v_c = [jnp.sum(jnp.where(eye, v_r[u], 0.0), axis=1, keepdims=True) for u in ids]
        s_new = [s_old[u] * w_r[u] + sa[u] * b_r[u] + v_c[u] * k_r[u] for u in ids]
        for u, (bi, h) in enumerate(units):
            st_ref[bi, h] = s_new[u]
        y_c = [jnp.sum(s_new[u] * r_r[u], axis=1, keepdims=True) for u in ids]
        y_r = [jnp.sum(jnp.where(eye, y_c[u], 0.0), axis=0, keepdims=True) for u in ids]
        for n, bi in enumerate(range(g0, g0 + group)):
            yrow_ref[bi:bi + 1, :] = jnp.concatenate(y_r[n * R_HEADS:(n + 1) * R_HEADS], axis=1)
    y_ref[...] = _rwkv_finish(yrow_ref[...], bonus, g, gnw_ref[...], gnb_ref[...], ones)


def _rwkv_step(rc, shift_prev, state, params, rows):
    db = rc.shape[0]
    const = lambda i: (0, 0)
    specs = [pl.BlockSpec(p.shape, const) for p in params]
    state_spec = pl.BlockSpec((rows, R_HEADS, R_HEAD, R_HEAD), lambda i: (i, 0, 0, 0))
    return pl.pallas_call(
        functools.partial(_rwkv_step_kernel, rows=rows),
        grid=(db // rows,),
        in_specs=[
            pl.BlockSpec((rows, R_SHIFT_W), lambda i: (i, 0)),
            pl.BlockSpec((rows, R_SHIFT_W), lambda i: (i, 0)),
            state_spec,
        ] + specs,
        out_specs=[pl.BlockSpec((rows, R_WIDTH), lambda i: (i, 0)), state_spec],
        out_shape=(
            jax.ShapeDtypeStruct((db, R_WIDTH), BF16),
            jax.ShapeDtypeStruct((db, R_HEADS, R_HEAD, R_HEAD), F32),
        ),
        scratch_shapes=[pltpu.VMEM((rows, R_WIDTH), F32)],
        compiler_params=pltpu.CompilerParams(
            dimension_semantics=("arbitrary",), vmem_limit_bytes=V7X_VMEM_LIMIT),
        name="rwkv_step",
    )(rc, shift_prev, state, *params)


def _pairs_to_state(st):
    b = st.shape[0]
    blocks = st.reshape(b, R_PAIRS, 2, R_HEAD, 2, R_HEAD)
    diag = jnp.stack([blocks[:, :, 0, :, 0, :], blocks[:, :, 1, :, 1, :]], axis=2)
    return jnp.swapaxes(diag.reshape(b, R_HEADS, R_HEAD, R_HEAD), -1, -2)


def _merge_ffn_kernel(x_ref, o_ref, rw_ref, g_ref, wpa_ref, wpb_ref, wout_ref, nf_ref,
                      wg_ref, wu_ref, wd_ref, y_ref, *, d_ff):
    ya = _dot(o_ref[...].astype(BF16), wpa_ref[...])
    yb = _dot(rw_ref[...], wpb_ref[...])
    merged = (g_ref[:, 0:D_MODEL] * ya + g_ref[:, D_MODEL:] * yb).astype(BF16)
    x1 = x_ref[...] + _dot(merged, wout_ref[...])
    hf = _rms_rows(x1, nf_ref[...]).astype(BF16)
    acc = x1
    step = 512
    for c in range(0, d_ff, step):
        n = min(step, d_ff - c)
        gate = _dot(hf, wg_ref[:, c:c + n])
        up = _dot(hf, wu_ref[:, c:c + n])
        act = (gate * _sigmoid(gate) * up).astype(BF16)
        acc = acc + _dot(act, wd_ref[c:c + n, :])
    y_ref[...] = acc


def _merge_ffn(x2, o, rw, gates, wpa, wpb, wout, nf, wg, wu, wd, tm):
    m = x2.shape[0]
    d_ff = wg.shape[1]
    row = lambda i: (i, 0)
    const = lambda i: (0, 0)
    return pl.pallas_call(
        functools.partial(_merge_ffn_kernel, d_ff=d_ff),
        grid=(m // tm,),
        in_specs=[
            pl.BlockSpec((tm, D_MODEL), row),
            pl.BlockSpec((tm, A_WIDTH), row),
            pl.BlockSpec((tm, R_WIDTH), row),
            pl.BlockSpec((tm, 2 * D_MODEL), row),
            pl.BlockSpec(wpa.shape, const),
            pl.BlockSpec(wpb.shape, const),
            pl.BlockSpec(wout.shape, const),
            pl.BlockSpec((1, D_MODEL), const),
            pl.BlockSpec(wg.shape, const),
            pl.BlockSpec(wu.shape, const),
            pl.BlockSpec(wd.shape, const),
        ],
        out_specs=pl.BlockSpec((tm, D_MODEL), row),
        out_shape=jax.ShapeDtypeStruct((m, D_MODEL), F32),
        compiler_params=pltpu.CompilerParams(
            dimension_semantics=("arbitrary",), vmem_limit_bytes=V7X_VMEM_LIMIT),
        name="merge_ffn",
    )(x2, o, rw, gates, wpa, wpb, wout, nf, wg, wu, wd)


def _row(v):
    return v.reshape(1, -1).astype(F32)


def _pick_tile(m, target):
    t = min(m, target)
    while m % t:
        t //= 2
    return t


def kernel(x_prompt, x_sample, cache_k, cache_v, page_table, state_wkv, state_shift, norm_mix, w_in, q_gain, k_gain, lambda_q1, lambda_k1, lambda_q2, lambda_k2, attn_out_gain, w_pa, shift_mu, w0, w2, a0, a2, g2, k_k, k_a, r_k, gn_w, gn_b, w_pb, w_out, norm_ffn, w_gate, w_up, w_down):
    b, t, _ = x_prompt.shape
    db, ds, _ = x_sample.shape
    assert ds == 1, "sample group carries one new token per sequence"
    depth = w_in.shape[0]
    n_pages = page_table.shape[1]
    xp = x_prompt.reshape(b * t, D_MODEL)
    xs = x_sample.reshape(db, D_MODEL)
    head_ones = (jnp.arange(A_WIDTH)[:, None] // A_HEAD_DIM
                 == jnp.arange(A_WIDTH)[None, :] // A_HEAD_DIM).astype(BF16)
    outs = [[] for _ in range(8)]
    tm_p = _pick_tile(b * t, 512)
    tm_s = _pick_tile(db, 512)
    tq = _pick_tile(t, 512)
    chunk = _pick_tile(t, RWKV_CHUNK)
    chunk_nb = _pick_tile(b, 4)
    pps = _pick_tile(n_pages, 16)
    step_rows = _pick_tile(db, 16)
    for l in range(depth):
        lambda_init = 0.8 - 0.6 * math.exp(-0.3 * l)
        w_in_bf = w_in[l].astype(BF16)
        qg = _row(jnp.tile(q_gain[l], A_WIDTH // A_HEAD_DIM))
        kg = _row(jnp.tile(k_gain[l], A_WIDTH // A_HEAD_DIM))
        nm = _row(norm_mix[l])
        lam_vecs = (_row(lambda_q1[l]), _row(lambda_k1[l]), _row(lambda_q2[l]), _row(lambda_k2[l]))
        og = _row(attn_out_gain[l])
        zeros_lora = jnp.zeros((R_LORA_W, R_WIDTH), F32)
        rwkv_params = (
            _row(shift_mu[l]), _row(w0[l]),
            jnp.concatenate([w2[l], zeros_lora], axis=0).astype(BF16),
            _row(a0[l]),
            jnp.concatenate([zeros_lora, a2[l]], axis=0).astype(BF16),
            g2[l].astype(BF16), _row(k_k[l]), _row(k_a[l]), _row(r_k[l]),
            _row(gn_w[l]), _row(gn_b[l]), head_ones)
        ffn_w = (w_pa[l].astype(BF16), w_pb[l].astype(BF16), w_out[l].astype(BF16),
                 _row(norm_ffn[l]), w_gate[l].astype(BF16), w_up[l].astype(BF16),
                 w_down[l].astype(BF16))

        qb, k32, kb, v32, vb, rc, gates = _proj(xp, nm, w_in_bf, qg, kg, head_ones, tm_p)
        o = _prompt_attn(qb, kb, vb, *lam_vecs, og, b, t, tq, lambda_init)
        rw, st_p, sh_p = _rwkv_prompt(rc.reshape(b, t, R_SHIFT_W), rwkv_params, chunk, chunk_nb)
        xp = _merge_ffn(xp, o, rw.reshape(b * t, R_WIDTH), gates, *ffn_w, tm_p)
        outs[0].append(k32.reshape(b, t, A_HEADS, A_QK_DIM))
        outs[1].append(v32.reshape(b, t, A_HEADS, A_V_DIM))
        outs[2].append(_pairs_to_state(st_p))
        outs[3].append(sh_p.reshape(b, R_SHIFT_W))

        qb, k32, kb, v32, vb, rc, gates = _proj(xs, nm, w_in_bf, qg, kg, head_ones, tm_s)
        o = _decode_attn(page_table, qb, kb, vb, cache_k, cache_v, l, *lam_vecs, og,
                         pps, lambda_init)
        rw, st_s = _rwkv_step(rc, state_shift[l], state_wkv[l], rwkv_params,
                              step_rows)
        xs = _merge_ffn(xs, o, rw, gates, *ffn_w, tm_s)
        outs[4].append(k32.reshape(db, 1, A_HEADS, A_QK_DIM))
        outs[5].append(v32.reshape(db, 1, A_HEADS, A_V_DIM))
        outs[6].append(st_s)
        outs[7].append(rc)
    return (xp.reshape(b, t, D_MODEL), xs.reshape(db, 1, D_MODEL),
            *(jnp.stack(o) for o in outs))
```

```python
import functools
import math

import jax
import jax.numpy as jnp
from jax import lax
from jax.experimental import pallas as pl
from jax.experimental.pallas import tpu as pltpu

F32 = jnp.float32
BF16 = jnp.bfloat16

D_MODEL = 1024
PAGE_SIZE = 128
A_HEADS = 4
A_HEAD_DIM = 64
A_QK_DIM = 2 * A_HEAD_DIM
A_V_DIM = 2 * A_HEAD_DIM
A_WIDTH = A_HEADS * A_V_DIM
R_HEAD = 64
R_HEADS = 8
R_WIDTH = R_HEADS * R_HEAD
R_LORA_W = 64
R_LORA_A = 64
R_LORA_G = 128
R_SHIFT_W = 3 * R_WIDTH + R_LORA_W + R_LORA_A + R_LORA_G
OFF_Q = 0
OFF_K = OFF_Q + A_HEADS * A_QK_DIM
OFF_V = OFF_K + A_HEADS * A_QK_DIM
OFF_R = OFF_V + A_WIDTH
OFF_G = OFF_R + R_SHIFT_W
IN_WIDTH = OFF_G + 2 * D_MODEL
NORM_EPS = 1e-6
GN_EPS = 64e-5
NEG_BIG = -1e30

V7X_VMEM_LIMIT = 56 * 1024 * 1024
LANES = 128
R_PAIRS = R_WIDTH // LANES
RWKV_CHUNK = 64
RWKV_CHUNKS_PER_STEP = 4


def _dot(a, b):
    return jnp.dot(a, b, preferred_element_type=F32)


def _dot_nt(a, b):
    return lax.dot_general(a, b, (((1,), (1,)), ((), ())), preferred_element_type=F32)


def _dot_tn(a, b):
    return lax.dot_general(a, b, (((0,), (0,)), ((), ())), preferred_element_type=F32)


def _mm(a, b, dot=_dot):
    return dot(a.astype(BF16), b.astype(BF16))


def _sigmoid(x):
    return 1.0 / (1.0 + jnp.exp(-x))


def _iota(shape, dim):
    return lax.broadcasted_iota(jnp.int32, shape, dim)


def _group_sum(z, ones_bf):
    return _dot(z.astype(BF16), ones_bf)


def _rms_rows(x, gain_row):
    ms = jnp.mean(x * x, axis=-1, keepdims=True)
    return x * lax.rsqrt(ms + NORM_EPS) * gain_row


def _proj_kernel(x_ref, nm_ref, w_ref, qg_ref, kg_ref, ones_ref,
                 q_ref, k_ref, kb_ref, v_ref, vb_ref, r_ref, g_ref):
    h = _rms_rows(x_ref[...], nm_ref[...]).astype(BF16)
    ones = ones_ref[...]

    def head_norm(z, gain_row):
        ss = _group_sum(z * z, ones)
        return z * lax.rsqrt(ss * (1.0 / A_HEAD_DIM) + NORM_EPS) * gain_row

    q = head_norm(_dot(h, w_ref[:, OFF_Q:OFF_K]), qg_ref[...])
    q_ref[...] = (q * (A_HEAD_DIM ** -0.5)).astype(BF16)
    tm = x_ref.shape[0]
    k = head_norm(_dot(h, w_ref[:, OFF_K:OFF_V]), kg_ref[...])
    kb_ref[...] = k.astype(BF16)
    v = _dot(h, w_ref[:, OFF_V:OFF_R])
    vb_ref[...] = v.astype(BF16)
    for hd in range(A_HEADS):
        sl = slice(hd * A_QK_DIM, (hd + 1) * A_QK_DIM)
        k_ref[pl.ds(hd, tm, stride=A_HEADS), :] = k[:, sl]
        v_ref[pl.ds(hd, tm, stride=A_HEADS), :] = v[:, sl]
    step = 512
    for c in range(0, R_SHIFT_W, step):
        n = min(step, R_SHIFT_W - c)
        r_ref[:, c:c + n] = _dot(h, w_ref[:, OFF_R + c:OFF_R + c + n])
    for c in range(0, 2 * D_MODEL, step):
        g_ref[:, c:c + step] = _sigmoid(_dot(h, w_ref[:, OFF_G + c:OFF_G + c + step]))


def _proj(x2, nm, w_in_bf, qg, kg, ones_bf, tm):
    m = x2.shape[0]
    row = lambda i: (i, 0)
    const = lambda i: (0, 0)
    out_shape = (
        jax.ShapeDtypeStruct((m, A_WIDTH), BF16),
        jax.ShapeDtypeStruct((m * A_HEADS, A_QK_DIM), F32),
        jax.ShapeDtypeStruct((m, A_WIDTH), BF16),
        jax.ShapeDtypeStruct((m * A_HEADS, A_V_DIM), F32),
        jax.ShapeDtypeStruct((m, A_WIDTH), BF16),
        jax.ShapeDtypeStruct((m, R_SHIFT_W), F32),
        jax.ShapeDtypeStruct((m, 2 * D_MODEL), F32),
    )
    return pl.pallas_call(
        _proj_kernel,
        grid=(m // tm,),
        in_specs=[
            pl.BlockSpec((tm, D_MODEL), row),
            pl.BlockSpec((1, D_MODEL), const),
            pl.BlockSpec((D_MODEL, IN_WIDTH), const),
            pl.BlockSpec((1, A_WIDTH), const),
            pl.BlockSpec((1, A_WIDTH), const),
            pl.BlockSpec((A_WIDTH, A_WIDTH), const),
        ],
        out_specs=[
            pl.BlockSpec((tm, A_WIDTH), row),
            pl.BlockSpec((tm * A_HEADS, A_QK_DIM), row),
            pl.BlockSpec((tm, A_WIDTH), row),
            pl.BlockSpec((tm * A_HEADS, A_V_DIM), row),
            pl.BlockSpec((tm, A_WIDTH), row),
            pl.BlockSpec((tm, R_SHIFT_W), row),
            pl.BlockSpec((tm, 2 * D_MODEL), row),
        ],
        out_shape=out_shape,
        compiler_params=pltpu.CompilerParams(
            dimension_semantics=("arbitrary",), vmem_limit_bytes=V7X_VMEM_LIMIT),
        name="proj",
    )(x2, nm, w_in_bf, qg, kg, ones_bf)


def _lambda_full(lq1, lk1, lq2, lk2, lambda_init):
    s1 = jnp.sum(lq1 * lk1, axis=-1, keepdims=True)
    s2 = jnp.sum(lq2 * lk2, axis=-1, keepdims=True)
    return jnp.exp(s1) - jnp.exp(s2) + lambda_init


ATTN_ACC_ROWS = A_V_DIM + 16
ATTN_HEADS_PER_STEP = 1


def _alibi_lanes(pos, slope, keys):
    l4 = _iota(pos.shape, 1) & (A_HEAD_DIM - 1)
    lo = slope * (pos & 255).astype(F32)
    hi = slope * ((pos >> 8) << 8).astype(F32)
    if keys:
        val = jnp.where(l4 < 2, 1.0, jnp.where(l4 == 2, lo, jnp.where(l4 == 3, hi, 0.0)))
    else:
        val = jnp.where(l4 == 0, -lo, jnp.where(l4 == 1, -hi, jnp.where(l4 < 4, 1.0, 0.0)))
    return val.astype(BF16)


def _prompt_attn_kernel(q_ref, k_ref, v_ref, lq1_ref, lk1_ref, lq2_ref, lk2_ref, gain_ref,
                        o_ref, kc_ref, vt_ref, acc_ref, *, tq, lambda_init):
    hg = pl.program_id(1)
    t = k_ref.shape[0]
    hp = ATTN_HEADS_PER_STEP
    n_st = 2 * hp
    slopes = [jnp.exp2(jnp.full((1, 1), -8.0 / A_HEADS, F32) * (hg * hp + hh + 1).astype(F32))
              for hh in range(hp)]
    head_lanes = [slice(hh * A_QK_DIM, (hh + 1) * A_QK_DIM) for hh in range(hp)]
    pos = _iota((tq, A_QK_DIM), 0)
    low = _iota((tq, A_QK_DIM), 1) < A_HEAD_DIM

    ones_rows = (_iota((ATTN_ACC_ROWS - A_V_DIM, tq), 0) == 0).astype(BF16)
    for hh in range(hp):
        ek = _alibi_lanes(pos, slopes[hh], keys=True)
        for jc in range(t // tq):
            rows = slice(jc * tq, (jc + 1) * tq)
            kt = k_ref[rows, head_lanes[hh]]
            kc_ref[2 * hh, rows, :] = jnp.where(low, kt, ek)
            kc_ref[2 * hh + 1, rows, :] = jnp.where(low, ek, kt)
            vt_ref[hh, jc, 0:A_V_DIM, :] = v_ref[rows, head_lanes[hh]].astype(F32).T.astype(BF16)
            vt_ref[hh, jc, A_V_DIM:, :] = ones_rows

    eqs = [_alibi_lanes(pos, slopes[hh], keys=False) for hh in range(hp)]
    lam = _lambda_full(lq1_ref[...], lk1_ref[...], lq2_ref[...], lk2_ref[...], lambda_init)
    kv_pos = _iota((tq, tq), 0)
    q_pos = _iota((tq, tq), 1)

    def q_tile(i, carry):
        i = jnp.asarray(i, jnp.int32)
        q_rows = pl.ds(pl.multiple_of(i * tq, tq), tq)
        qc = []
        for hh in range(hp):
            q = q_ref[q_rows, head_lanes[hh]]
            qc += [jnp.where(low, q, eqs[hh]), jnp.where(low, eqs[hh], q)]

        def tiles(group, ms):
            s = []
            for j, masked in group:
                rows = pl.ds(pl.multiple_of(jnp.asarray(j * tq, jnp.int32), tq), tq)
                sj = [_dot_nt(kc_ref[st, rows, :], qc[st]) for st in range(n_st)]
                if masked:
                    sj = [jnp.where(kv_pos <= q_pos, sc, NEG_BIG) for sc in sj]
                s.append(sj)
            ms = list(ms)
            acc = [acc_ref[st] for st in range(n_st)]
            for (j, _), sj in zip(group, s):
                for st in range(n_st):
                    off = slopes[st // 2] * jnp.asarray((j - i) * tq, F32)
                    m_rel = ms[st] - off
                    m_new = jnp.maximum(m_rel, jnp.max(sj[st], axis=0, keepdims=True))
                    p = jnp.exp(sj[st] - m_new).astype(BF16)
                    acc[st] = jnp.exp(m_rel - m_new) * acc[st] + _dot(vt_ref[st // 2, j], p)
                    ms[st] = m_new + off
            for st in range(n_st):
                acc_ref[st] = acc[st]
            return tuple(ms)

        acc_ref[...] = jnp.zeros_like(acc_ref)
        ms = lax.fori_loop(0, i // 2,
                           lambda pr, ms: tiles([(2 * pr, False), (2 * pr + 1, False)], ms),
                           tuple(jnp.full((1, tq), NEG_BIG, F32) for _ in range(n_st)))
        lax.cond((i & 1) == 1,
                 lambda ms: tiles([(i - 1, False), (i, True)], ms),
                 lambda ms: tiles([(i, True)], ms), ms)
        for hh in range(hp):
            a0, a1 = acc_ref[2 * hh], acc_ref[2 * hh + 1]
            o_t = (a0[0:A_V_DIM] / a0[A_V_DIM:A_V_DIM + 1]
                   - lam * (a1[0:A_V_DIM] / a1[A_V_DIM:A_V_DIM + 1]))
            ms_o = jnp.mean(o_t * o_t, axis=0, keepdims=True)
            o_t = o_t * lax.rsqrt(ms_o + NORM_EPS) * gain_ref[...] * (1.0 - lambda_init)
            o_ref[q_rows, head_lanes[hh]] = o_t.T.astype(BF16)
        return carry

    lax.fori_loop(0, t // tq, q_tile, 0)


def _prompt_attn(qb, kb, vb, lq1, lk1, lq2, lk2, gain, b, t, tq, lambda_init):
    nq = t // tq
    hp = ATTN_HEADS_PER_STEP
    vec = lambda bi, h: (0, 0)
    seq_head = lambda bi, h: (bi, h)
    return pl.pallas_call(
        functools.partial(_prompt_attn_kernel, tq=tq, lambda_init=lambda_init),
        grid=(b, A_HEADS // hp),
        in_specs=[
            pl.BlockSpec((t, hp * A_QK_DIM), seq_head),
            pl.BlockSpec((t, hp * A_QK_DIM), seq_head),
            pl.BlockSpec((t, hp * A_V_DIM), seq_head),
            pl.BlockSpec((1, A_HEAD_DIM), vec),
            pl.BlockSpec((1, A_HEAD_DIM), vec),
            pl.BlockSpec((1, A_HEAD_DIM), vec),
            pl.BlockSpec((1, A_HEAD_DIM), vec),
            pl.BlockSpec((A_V_DIM, 1), vec),
        ],
        out_specs=pl.BlockSpec((t, hp * A_V_DIM), seq_head),
        out_shape=jax.ShapeDtypeStruct((b * t, A_WIDTH), BF16),
        scratch_shapes=[pltpu.VMEM((2 * hp, t, A_QK_DIM), BF16),
                        pltpu.VMEM((hp, nq, ATTN_ACC_ROWS, tq), BF16),
                        pltpu.VMEM((2 * hp, ATTN_ACC_ROWS, tq), F32)],
        compiler_params=pltpu.CompilerParams(
            dimension_semantics=("arbitrary", "arbitrary"),
            vmem_limit_bytes=V7X_VMEM_LIMIT),
        name="prompt_attn",
    )(qb, kb, vb, lq1, lk1, lq2, lk2, gain.reshape(A_V_DIM, 1))


def _decode_attn_kernel(pt_ref, q_ref, kn_ref, vn_ref, lq1_ref, lk1_ref, lq2_ref, lk2_ref,
                        gain_ref, *refs, pps, n_pages, lambda_init):
    k_refs = refs[:pps]
    v_refs = refs[pps:2 * pps]
    o_ref = refs[2 * pps]
    m_ref, l_ref, acc_ref = refs[2 * pps + 1:]
    j = pl.program_id(1)
    rows = 2 * A_HEADS
    page_rows = PAGE_SIZE * A_HEADS
    row_q = _iota((rows, A_QK_DIM), 0)
    lane_q = _iota((rows, A_QK_DIM), 1)
    qm = jnp.where((lane_q >> 6) == (row_q & 1), q_ref[0], 0.0).astype(BF16)
    slope = jnp.exp2((-8.0 / A_HEADS) * ((_iota((rows, 1), 0) >> 1) + 1).astype(F32))
    col = _iota((rows, pps * page_rows), 1)
    own_head = (col & (A_HEADS - 1)) == (_iota((rows, pps * page_rows), 0) >> 1)
    dist = (n_pages - j * pps) * PAGE_SIZE - (col >> 2)

    @pl.when(j == 0)
    def _():
        m_ref[...] = jnp.full((rows, 1), NEG_BIG, F32)
        l_ref[...] = jnp.zeros((rows, 1), F32)
        acc_ref[...] = jnp.zeros((rows, A_V_DIM), F32)

    s = jnp.concatenate([_dot_nt(qm, k_refs[u][...].astype(BF16)) for u in range(pps)], axis=1)
    s = jnp.where(own_head, s - slope * dist.astype(F32), NEG_BIG)
    m = m_ref[...]
    m_new = jnp.maximum(m, jnp.max(s, axis=-1, keepdims=True))
    alpha = jnp.exp(m - m_new)
    p = jnp.exp(s - m_new)
    pb = p.astype(BF16)
    pv = _dot(pb[:, 0:page_rows], v_refs[0][...].astype(BF16))
    for u in range(1, pps):
        pv = pv + _dot(pb[:, u * page_rows:(u + 1) * page_rows], v_refs[u][...].astype(BF16))
    m = m_new
    l = alpha * l_ref[...] + jnp.sum(p, axis=-1, keepdims=True)
    acc = alpha * acc_ref[...] + pv
    m_ref[...], l_ref[...], acc_ref[...] = m, l, acc

    @pl.when(j == pl.num_programs(1) - 1)
    def _():
        s_new = jnp.sum(qm.astype(F32) * kn_ref[0], axis=-1, keepdims=True)
        m_new = jnp.maximum(m, s_new)
        alpha = jnp.exp(m - m_new)
        p_new = jnp.exp(s_new - m_new)
        l_fin = alpha * l + p_new
        acc_fin = alpha * acc + p_new.astype(BF16).astype(F32) * vn_ref[0]
        lam = _lambda_full(lq1_ref[...], lk1_ref[...], lq2_ref[...], lk2_ref[...], lambda_init)
        comp = _iota((rows, 1), 0) & 1
        x = acc_fin * (jnp.where(comp == 0, 1.0, -lam) / l_fin)
        o = x + pltpu.roll(x, shift=rows - 1, axis=0)
        o_ref[0] = _rms_rows(o, gain_ref[...]) * (1.0 - lambda_init)


def _decode_attn(page_table, qb, kb, vb, cache_k, cache_v, layer, lq1, lk1, lq2, lk2, gain,
                 pps, lambda_init):
    db, n_pages = page_table.shape
    n_pool = cache_k.shape[1]
    page_rows = PAGE_SIZE * A_HEADS
    ck = cache_k.reshape(-1, A_QK_DIM)
    cv = cache_v.reshape(-1, A_V_DIM)
    pt = page_table.reshape(db * n_pages)
    rows = 2 * A_HEADS
    tok3 = lambda b, j, pt_ref: (b, 0, 0)
    vec = lambda b, j, pt_ref: (0, 0)

    def page_spec(u):
        return pl.BlockSpec(
            (page_rows, A_QK_DIM),
            lambda b, j, pt_ref: (layer * n_pool + pt_ref[b * n_pages + j * pps + u], 0))

    def per_row(z):
        return jnp.repeat(z.astype(F32).reshape(db, A_HEADS, A_QK_DIM), 2, axis=1)

    grid_spec = pltpu.PrefetchScalarGridSpec(
        num_scalar_prefetch=1,
        grid=(db, n_pages // pps),
        in_specs=[
            pl.BlockSpec((1, rows, A_QK_DIM), tok3),
            pl.BlockSpec((1, rows, A_QK_DIM), tok3),
            pl.BlockSpec((1, rows, A_V_DIM), tok3),
            pl.BlockSpec((1, A_HEAD_DIM), vec),
            pl.BlockSpec((1, A_HEAD_DIM), vec),
            pl.BlockSpec((1, A_HEAD_DIM), vec),
            pl.BlockSpec((1, A_HEAD_DIM), vec),
            pl.BlockSpec((1, A_V_DIM), vec),
        ] + [page_spec(u) for u in range(pps)] + [page_spec(u) for u in range(pps)],
        out_specs=pl.BlockSpec((1, rows, A_V_DIM), tok3),
        scratch_shapes=[pltpu.VMEM((rows, 1), F32), pltpu.VMEM((rows, 1), F32),
                        pltpu.VMEM((rows, A_V_DIM), F32)],
    )
    out = pl.pallas_call(
        functools.partial(_decode_attn_kernel, pps=pps, n_pages=n_pages, lambda_init=lambda_init),
        grid_spec=grid_spec,
        out_shape=jax.ShapeDtypeStruct((db, rows, A_V_DIM), F32),
        compiler_params=pltpu.CompilerParams(
            dimension_semantics=("arbitrary", "arbitrary"), vmem_limit_bytes=V7X_VMEM_LIMIT),
        name="decode_attn",
    )(pt, per_row(qb), per_row(kb), per_row(vb),
      lq1, lk1, lq2, lk2, gain, *([ck] * pps), *([cv] * pps))
    return out[:, 0::2, :].reshape(db, A_WIDTH)


def _rwkv_prep(x, prev, mu, w0, w2e, a0, a2e, g2, k_k, k_a, r_k, ones):
    m = x + (prev - x) * mu
    r = m[:, 0:R_WIDTH]
    k = m[:, R_WIDTH:2 * R_WIDTH]
    v = m[:, 2 * R_WIDTH:3 * R_WIDTH]
    xwa = m[:, 3 * R_WIDTH:3 * R_WIDTH + R_LORA_W + R_LORA_A]
    xg = m[:, 3 * R_WIDTH + R_LORA_W + R_LORA_A:]
    z = w0 + _dot(jnp.tanh(xwa).astype(BF16), w2e)
    lw = (-math.exp(-0.5)) * _sigmoid(z)
    a = _sigmoid(a0 + _dot(xwa.astype(BF16), a2e))
    g = _dot(_sigmoid(xg).astype(BF16), g2)
    kk = k * k_k
    kk = kk * lax.rsqrt(jnp.maximum(_group_sum(kk * kk, ones), 1e-24))
    k = k * (1.0 + (a - 1.0) * k_a)
    bonus = _group_sum(r * k * r_k, ones) * v
    return r, lw, k, v, kk, a, g, bonus


def _rwkv_finish(y, bonus, g, gn_w, gn_b, ones):
    mu = _group_sum(y, ones) * (1.0 / R_HEAD)
    d = y - mu
    var = _group_sum(d * d, ones) * (1.0 / R_HEAD)
    yn = d * lax.rsqrt(var + GN_EPS) * gn_w + gn_b
    return ((yn + bonus) * g).astype(BF16)


def _lane_halves(x, width):
    lane = _iota(x.shape, 1)
    zero = jnp.zeros_like(x)
    return jnp.concatenate([jnp.where(lane < width, x, zero), jnp.where(lane >= width, x, zero)],
                           axis=0)


def _lane_blocks(x, width):
    block = _iota(x.shape, 1) >> int(math.log2(width))
    zero = jnp.zeros_like(x)
    return jnp.concatenate([jnp.where(block == b, x, zero) for b in range(x.shape[1] // width)],
                           axis=0)


def _rwkv_chunk_kernel(x_ref, *refs, chunk, nb):
    st_ref, prev_ref = refs[-3], refs[-1]

    @pl.when(pl.program_id(1) == 0)
    def _():
        prev_ref[...] = jnp.zeros_like(prev_ref)
        st_ref[...] = jnp.zeros_like(st_ref)

    def body(k, carry):
        _rwkv_one_chunk(jnp.asarray(k, jnp.int32), x_ref, *refs, chunk=chunk, nb=nb)
        return carry

    lax.fori_loop(0, x_ref.shape[1] // chunk, body, 0)


def _rwkv_one_chunk(k, x_ref, mu_ref, w0_ref, w2e_ref, a0_ref, a2e_ref, g2_ref, kk_ref, ka_ref,
                    rk_ref, gnw_ref, gnb_ref, ones_ref,
                    y_ref, st_ref, sh_ref, prev_ref, *, chunk, nb):
    c = chunk
    chunk_rows = pl.ds(pl.multiple_of(k * c, c), c)
    ones = ones_ref[...]
    rows = _iota((c, R_SHIFT_W), 0)
    tri = (_iota((c, c), 0) >= _iota((c, c), 1)).astype(BF16)
    sls = [slice(p * LANES, (p + 1) * LANES) for p in range(R_PAIRS)]
    ah_p, rh_p, v_p, bt_p, kt_p, bb_p, kb_p, pe_p, st_idx, fin = ([] for _ in range(10))
    for bi in range(nb):
        x = x_ref[bi, chunk_rows, :]
        prev = jnp.where(rows == 0, prev_ref[bi], pltpu.roll(x, shift=1, axis=0))
        prev_ref[bi] = x[c - 1:c, :]
        sh_ref[bi] = x[c - 1:c, :]
        r, lw, k, v, kk, a, g, bonus = _rwkv_prep(
            x, prev, mu_ref[...], w0_ref[...], w2e_ref[...], a0_ref[...], a2e_ref[...],
            g2_ref[...], kk_ref[...], ka_ref[...], rk_ref[...], ones)
        fin.append((bonus, g))
        l_hi = lw.astype(BF16)
        l_r1 = lw - l_hi.astype(F32)
        l_mid = l_r1.astype(BF16)
        l_lo = (l_r1 - l_mid.astype(F32)).astype(BF16)
        cum = _dot(tri, l_hi) + _dot(tri, l_mid) + _dot(tri, l_lo)
        cum_end = cum[c - 1:c, :]
        e_neg = jnp.exp(-cum)
        e_end = jnp.exp(cum_end - cum)
        kka = kk * a
        ah = -kk * jnp.exp(cum - lw)
        rh = r * jnp.exp(cum)
        bt = kka * e_neg
        kt = k * e_neg
        bb = kka * e_end
        kb = k * e_end
        p_end = jnp.exp(cum_end)
        for p, sl in enumerate(sls):
            for lst, z in ((ah_p, ah), (rh_p, rh), (v_p, v), (bt_p, bt), (kt_p, kt),
                           (bb_p, bb), (kb_p, kb), (pe_p, p_end)):
                lst.append(z[:, sl])
            st_idx.append((bi, p))

    t_idx = _iota((c, 2 * c), 0)
    s_idx = _iota((c, 2 * c), 1) & (c - 1)
    strict = t_idx > s_idx
    incl = t_idx >= s_idx
    eye_pack = (t_idx == s_idx).astype(F32)
    bd_mask = (_iota((LANES, LANES), 0) >> 6) == (_iota((LANES, LANES), 1) >> 6)
    eye_l = _iota((LANES, LANES), 0) == _iota((LANES, LANES), 1)
    zeros_c = jnp.zeros((c, LANES), F32)

    units = range(len(st_idx))
    v_blk = [_lane_halves(z, R_HEAD) for z in v_p]
    a_all = [_mm(jnp.concatenate([ah_p[u], rh_p[u]], axis=0),
                 jnp.concatenate([_lane_halves(bt_p[u], R_HEAD), _lane_halves(kt_p[u], R_HEAD)],
                                 axis=0), _dot_nt) for u in units]
    a_ab = [jnp.where(strict, z[0:c, 0:2 * c], 0.0) for z in a_all]
    a_ak = [jnp.where(strict, z[0:c, 2 * c:4 * c], 0.0) for z in a_all]
    a_rb = [jnp.where(incl, z[c:2 * c, 0:2 * c], 0.0) for z in a_all]
    a_rk = [jnp.where(incl, z[c:2 * c, 2 * c:4 * c], 0.0) for z in a_all]
    w1 = [_mm(a_ak[u], v_blk[u]) for u in units]
    quads = range(len(st_idx) // 2)
    a4 = [jnp.concatenate([a_ab[2 * q], a_ab[2 * q + 1]], axis=1) for q in quads]
    eye4 = jnp.concatenate([eye_pack, eye_pack], axis=1)
    inv4 = [eye4 + z for z in a4]
    bd = [_lane_blocks(z, c) for z in a4]
    pw = [_mm(a4[q], bd[q]) for q in quads]
    n_sq = int(math.log2(c)) - 1
    for step in range(n_sq):
        bd = [_lane_blocks(z, c) for z in pw]
        if step < n_sq - 1:
            both = [_mm(jnp.concatenate([pw[q], inv4[q]], axis=0), bd[q]) for q in quads]
            pw = [z[0:c] for z in both]
            inv4 = [inv4[q] + both[q][c:2 * c] for q in quads]
        else:
            inv4 = [inv4[q] + _mm(inv4[q], bd[q]) for q in quads]
    inv = [inv4[u // 2][:, (u % 2) * 2 * c:(u % 2 + 1) * 2 * c] for u in units]
    t4 = [_mm(inv[u], jnp.concatenate([_lane_halves(w1[u], R_HEAD),
                                       _lane_halves(ah_p[u], R_HEAD)], axis=1)) for u in units]
    wu = [z[:, 0:LANES] for z in t4]
    ah2 = [z[:, LANES:] for z in t4]
    zeros_blk = jnp.zeros((2 * c, LANES), F32)
    t5 = [_mm(jnp.concatenate([a_rb[u], a_rk[u]], axis=1),
              jnp.concatenate([
                  jnp.concatenate([_lane_halves(wu[u], R_HEAD), _lane_halves(ah2[u], R_HEAD)],
                                  axis=1),
                  jnp.concatenate([v_blk[u], zeros_blk], axis=1)], axis=0)) for u in units]
    t6 = [_mm(jnp.concatenate([bb_p[u], kb_p[u]], axis=0),
              jnp.concatenate([jnp.concatenate([wu[u], ah2[u]], axis=1),
                               jnp.concatenate([v_p[u], zeros_c], axis=1)], axis=0), _dot_tn)
          for u in units]
    st = [st_ref[bi, p] for bi, p in st_idx]
    ys = [_mm(rh_p[u] + t5[u][:, LANES:], st[u]) + t5[u][:, 0:LANES] for u in units]
    for u in units:
        h_new = jnp.where(bd_mask, t6[u][:, 0:LANES], 0.0)
        g_x = jnp.where(bd_mask, t6[u][:, LANES:], 0.0)
        p_col = jnp.sum(jnp.where(eye_l, pe_p[u], 0.0), axis=1, keepdims=True)
        st_ref[st_idx[u]] = p_col * st[u] + _mm(g_x, st[u]) + h_new

    for bi in range(nb):
        y = jnp.concatenate(ys[bi * R_PAIRS:(bi + 1) * R_PAIRS], axis=1)
        bonus, g = fin[bi]
        y_ref[bi, chunk_rows, :] = _rwkv_finish(y, bonus, g, gnw_ref[...], gnb_ref[...], ones)


def _rwkv_prompt(rc3, params, chunk, nb):
    b, t, _ = rc3.shape
    rows = _pick_tile(t, RWKV_CHUNKS_PER_STEP * chunk)
    const = lambda bi, j: (0, 0)
    specs = [pl.BlockSpec(p.shape, const) for p in params]
    return pl.pallas_call(
        functools.partial(_rwkv_chunk_kernel, chunk=chunk, nb=nb),
        grid=(b // nb, t // rows),
        in_specs=[pl.BlockSpec((nb, rows, R_SHIFT_W), lambda bi, j: (bi, j, 0))] + specs,
        out_specs=[
            pl.BlockSpec((nb, rows, R_WIDTH), lambda bi, j: (bi, j, 0)),
            pl.BlockSpec((nb, R_PAIRS, LANES, LANES), lambda bi, j: (bi, 0, 0, 0)),
            pl.BlockSpec((nb, 1, R_SHIFT_W), lambda bi, j: (bi, 0, 0)),
        ],
        out_shape=(
            jax.ShapeDtypeStruct((b, t, R_WIDTH), BF16),
            jax.ShapeDtypeStruct((b, R_PAIRS, LANES, LANES), F32),
            jax.ShapeDtypeStruct((b, 1, R_SHIFT_W), F32),
        ),
        scratch_shapes=[pltpu.VMEM((nb, 1, R_SHIFT_W), F32)],
        compiler_params=pltpu.CompilerParams(
            dimension_semantics=("arbitrary", "arbitrary"), vmem_limit_bytes=V7X_VMEM_LIMIT),
        name="rwkv_chunk",
    )(rc3, *params)


def _rwkv_step_kernel(x_ref, prev_ref, st_in_ref, mu_ref, w0_ref, w2e_ref, a0_ref, a2e_ref,
                      g2_ref, kk_ref, ka_ref, rk_ref, gnw_ref, gnb_ref, ones_ref,
                      y_ref, st_ref, yrow_ref, *, rows):
    ones = ones_ref[...]
    r, lw, k, v, kk, a, g, bonus = _rwkv_prep(
        x_ref[...], prev_ref[...], mu_ref[...], w0_ref[...], w2e_ref[...], a0_ref[...],
        a2e_ref[...], g2_ref[...], kk_ref[...], ka_ref[...], rk_ref[...], ones)
    vecs = (jnp.exp(lw), kk * a, k, v, kk, r)
    eye = _iota((R_HEAD, R_HEAD), 0) == _iota((R_HEAD, R_HEAD), 1)

    group = math.gcd(rows, 4)
    for g0 in range(0, rows, group):
        units = [(bi, h) for bi in range(g0, g0 + group) for h in range(R_HEADS)]
        w_r, b_r, k_r, v_r, kk_r, r_r = (
            [z[bi:bi + 1, h * R_HEAD:(h + 1) * R_HEAD] for bi, h in units] for z in vecs)
        ids = range(len(units))
        s_old = [st_in_ref[bi, h] for bi, h in units]
        sa = [-jnp.sum(s_old[u] * kk_r[u], axis=1, keepdims=True) for u in ids]
        v_c = [jnp.sum(jnp.where(eye, v_r[u], 0.0), axis=1, keepdims=True) for u in ids]
        s_new = [s_old[u] * w_r[u] + sa[u] * b_r[u] + v_c[u] * k_r[u] for u in ids]
        for u, (bi, h) in enumerate(units):
            st_ref[bi, h] = s_new[u]
        y_c = [jnp.sum(s_new[u] * r_r[u], axis=1, keepdims=True) for u in ids]
        y_r = [jnp.sum(jnp.where(eye, y_c[u], 0.0), axis=0, keepdims=True) for u in ids]
        for n, bi in enumerate(range(g0, g0 + group)):
            yrow_ref[bi:bi + 1, :] = jnp.concatenate(y_r[n * R_HEADS:(n + 1) * R_HEADS], axis=1)
    y_ref[...] = _rwkv_finish(yrow_ref[...], bonus, g, gnw_ref[...], gnb_ref[...], ones)


def _rwkv_step(rc, shift_prev, state, params, rows):
    db = rc.shape[0]
    const = lambda i: (0, 0)
    specs = [pl.BlockSpec(p.shape, const) for p in params]
    state_spec = pl.BlockSpec((rows, R_HEADS, R_HEAD, R_HEAD), lambda i: (i, 0, 0, 0))
    return pl.pallas_call(
        functools.partial(_rwkv_step_kernel, rows=rows),
        grid=(db // rows,),
        in_specs=[
            pl.BlockSpec((rows, R_SHIFT_W), lambda i: (i, 0)),
            pl.BlockSpec((rows, R_SHIFT_W), lambda i: (i, 0)),
            state_spec,
        ] + specs,
        out_specs=[pl.BlockSpec((rows, R_WIDTH), lambda i: (i, 0)), state_spec],
        out_shape=(
            jax.ShapeDtypeStruct((db, R_WIDTH), BF16),
            jax.ShapeDtypeStruct((db, R_HEADS, R_HEAD, R_HEAD), F32),
        ),
        scratch_shapes=[pltpu.VMEM((rows, R_WIDTH), F32)],
        compiler_params=pltpu.CompilerParams(
            dimension_semantics=("arbitrary",), vmem_limit_bytes=V7X_VMEM_LIMIT),
        name="rwkv_step",
    )(rc, shift_prev, state, *params)


def _pairs_to_state(st):
    b = st.shape[0]
    blocks = st.reshape(b, R_PAIRS, 2, R_HEAD, 2, R_HEAD)
    diag = jnp.stack([blocks[:, :, 0, :, 0, :], blocks[:, :, 1, :, 1, :]], axis=2)
    return jnp.swapaxes(diag.reshape(b, R_HEADS, R_HEAD, R_HEAD), -1, -2)


def _merge_ffn_kernel(x_ref, o_ref, rw_ref, g_ref, wpa_ref, wpb_ref, wout_ref, nf_ref,
                      wg_ref, wu_ref, wd_ref, y_ref, *, d_ff):
    ya = _dot(o_ref[...].astype(BF16), wpa_ref[...])
    yb = _dot(rw_ref[...], wpb_ref[...])
    merged = (g_ref[:, 0:D_MODEL] * ya + g_ref[:, D_MODEL:] * yb).astype(BF16)
    x1 = x_ref[...] + _dot(merged, wout_ref[...])
    hf = _rms_rows(x1, nf_ref[...]).astype(BF16)
    acc = x1
    step = 512
    for c in range(0, d_ff, step):
        n = min(step, d_ff - c)
        gate = _dot(hf, wg_ref[:, c:c + n])
        up = _dot(hf, wu_ref[:, c:c + n])
        act = (gate * _sigmoid(gate) * up).astype(BF16)
        acc = acc + _dot(act, wd_ref[c:c + n, :])
    y_ref[...] = acc


def _merge_ffn(x2, o, rw, gates, wpa, wpb, wout, nf, wg, wu, wd, tm):
    m = x2.shape[0]
    d_ff = wg.shape[1]
    row = lambda i: (i, 0)
    const = lambda i: (0, 0)
    return pl.pallas_call(
        functools.partial(_merge_ffn_kernel, d_ff=d_ff),
        grid=(m // tm,),
        in_specs=[
            pl.BlockSpec((tm, D_MODEL), row),
            pl.BlockSpec((tm, A_WIDTH), row),
            pl.BlockSpec((tm, R_WIDTH), row),
            pl.BlockSpec((tm, 2 * D_MODEL), row),
            pl.BlockSpec(wpa.shape, const),
            pl.BlockSpec(wpb.shape, const),
            pl.BlockSpec(wout.shape, const),
            pl.BlockSpec((1, D_MODEL), const),
            pl.BlockSpec(wg.shape, const),
            pl.BlockSpec(wu.shape, const),
            pl.BlockSpec(wd.shape, const),
        ],
        out_specs=pl.BlockSpec((tm, D_MODEL), row),
        out_shape=jax.ShapeDtypeStruct((m, D_MODEL), F32),
        compiler_params=pltpu.CompilerParams(
            dimension_semantics=("arbitrary",), vmem_limit_bytes=V7X_VMEM_LIMIT),
        name="merge_ffn",
    )(x2, o, rw, gates, wpa, wpb, wout, nf, wg, wu, wd)


def _row(v):
    return v.reshape(1, -1).astype(F32)


def _pick_tile(m, target):
    t = min(m, target)
    while m % t:
        t //= 2
    return t


def kernel(x_prompt, x_sample, cache_k, cache_v, page_table, state_wkv, state_shift, norm_mix, w_in, q_gain, k_gain, lambda_q1, lambda_k1, lambda_q2, lambda_k2, attn_out_gain, w_pa, shift_mu, w0, w2, a0, a2, g2, k_k, k_a, r_k, gn_w, gn_b, w_pb, w_out, norm_ffn, w_gate, w_up, w_down):
    b, t, _ = x_prompt.shape
    db, ds, _ = x_sample.shape
    assert ds == 1, "sample group carries one new token per sequence"
    depth = w_in.shape[0]
    n_pages = page_table.shape[1]
    xp = x_prompt.reshape(b * t, D_MODEL)
    xs = x_sample.reshape(db, D_MODEL)
    head_ones = (jnp.arange(A_WIDTH)[:, None] // A_HEAD_DIM
                 == jnp.arange(A_WIDTH)[None, :] // A_HEAD_DIM).astype(BF16)
    outs = [[] for _ in range(8)]
    tm_p = _pick_tile(b * t, 512)
    tm_s = _pick_tile(db, 512)
    tq = _pick_tile(t, 512)
    chunk = _pick_tile(t, RWKV_CHUNK)
    chunk_nb = _pick_tile(b, 4)
    pps = _pick_tile(n_pages, 16)
    step_rows = _pick_tile(db, 16)
    for l in range(depth):
        lambda_init = 0.8 - 0.6 * math.exp(-0.3 * l)
        w_in_bf = w_in[l].astype(BF16)
        qg = _row(jnp.tile(q_gain[l], A_WIDTH // A_HEAD_DIM))
        kg = _row(jnp.tile(k_gain[l], A_WIDTH // A_HEAD_DIM))
        nm = _row(norm_mix[l])
        lam_vecs = (_row(lambda_q1[l]), _row(lambda_k1[l]), _row(lambda_q2[l]), _row(lambda_k2[l]))
        og = _row(attn_out_gain[l])
        zeros_lora = jnp.zeros((R_LORA_W, R_WIDTH), F32)
        rwkv_params = (
            _row(shift_mu[l]), _row(w0[l]),
            jnp.concatenate([w2[l], zeros_lora], axis=0).astype(BF16),
            _row(a0[l]),
            jnp.concatenate([zeros_lora, a2[l]], axis=0).astype(BF16),
            g2[l].astype(BF16), _row(k_k[l]), _row(k_a[l]), _row(r_k[l]),
            _row(gn_w[l]), _row(gn_b[l]), head_ones)
        ffn_w = (w_pa[l].astype(BF16), w_pb[l].astype(BF16), w_out[l].astype(BF16),
                 _row(norm_ffn[l]), w_gate[l].astype(BF16), w_up[l].astype(BF16),
                 w_down[l].astype(BF16))

        qb, k32, kb, v32, vb, rc, gates = _proj(xp, nm, w_in_bf, qg, kg, head_ones, tm_p)
        o = _prompt_attn(qb, kb, vb, *lam_vecs, og, b, t, tq, lambda_init)
        rw, st_p, sh_p = _rwkv_prompt(rc.reshape(b, t, R_SHIFT_W), rwkv_params, chunk, chunk_nb)
        xp = _merge_ffn(xp, o, rw.reshape(b * t, R_WIDTH), gates, *ffn_w, tm_p)
        outs[0].append(k32.reshape(b, t, A_HEADS, A_QK_DIM))
        outs[1].append(v32.reshape(b, t, A_HEADS, A_V_DIM))
        outs[2].append(_pairs_to_state(st_p))
        outs[3].append(sh_p.reshape(b, R_SHIFT_W))

        qb, k32, kb, v32, vb, rc, gates = _proj(xs, nm, w_in_bf, qg, kg, head_ones, tm_s)
        o = _decode_attn(page_table, qb, kb, vb, cache_k, cache_v, l, *lam_vecs, og,
                         pps, lambda_init)
        rw, st_s = _rwkv_step(rc, state_shift[l], state_wkv[l], rwkv_params,
                              step_rows)
        xs = _merge_ffn(xs, o, rw, gates, *ffn_w, tm_s)
        outs[4].append(k32.reshape(db, 1, A_HEADS, A_QK_DIM))
        outs[5].append(v32.reshape(db, 1, A_HEADS, A_V_DIM))
        outs[6].append(st_s)
        outs[7].append(rc)
    return (xp.reshape(b, t, D_MODEL), xs.reshape(db, 1, D_MODEL),
            *(jnp.stack(o) for o in outs))
```

```python
import functools
import math

import jax
import jax.numpy as jnp
from jax import lax
from jax.experimental import pallas as pl
from jax.experimental.pallas import tpu as pltpu

F32 = jnp.float32
BF16 = jnp.bfloat16

D_MODEL = 1024
PAGE_SIZE = 128
A_HEADS = 4
A_HEAD_DIM = 64
A_QK_DIM = 2 * A_HEAD_DIM
A_V_DIM = 2 * A_HEAD_DIM
A_WIDTH = A_HEADS * A_V_DIM
R_HEAD = 64
R_HEADS = 8
R_WIDTH = R_HEADS * R_HEAD
R_LORA_W = 64
R_LORA_A = 64
R_LORA_G = 128
R_SHIFT_W = 3 * R_WIDTH + R_LORA_W + R_LORA_A + R_LORA_G
OFF_Q = 0
OFF_K = OFF_Q + A_HEADS * A_QK_DIM
OFF_V = OFF_K + A_HEADS * A_QK_DIM
OFF_R = OFF_V + A_WIDTH
OFF_G = OFF_R + R_SHIFT_W
IN_WIDTH = OFF_G + 2 * D_MODEL
NORM_EPS = 1e-6
GN_EPS = 64e-5
NEG_BIG = -1e30

V7X_VMEM_LIMIT = 56 * 1024 * 1024
LANES = 128
R_PAIRS = R_WIDTH // LANES
RWKV_CHUNK = 64
RWKV_CHUNKS_PER_STEP = 4


def _dot(a, b):
    return jnp.dot(a, b, preferred_element_type=F32)


def _dot_nt(a, b):
    return lax.dot_general(a, b, (((1,), (1,)), ((), ())), preferred_element_type=F32)


def _dot_tn(a, b):
    return lax.dot_general(a, b, (((0,), (0,)), ((), ())), preferred_element_type=F32)


def _mm(a, b, dot=_dot):
    return dot(a.astype(BF16), b.astype(BF16))


def _sigmoid(x):
    return 1.0 / (1.0 + jnp.exp(-x))


def _iota(shape, dim):
    return lax.broadcasted_iota(jnp.int32, shape, dim)


def _group_sum(z, ones_bf):
    return _dot(z.astype(BF16), ones_bf)


def _rms_rows(x, gain_row):
    ms = jnp.mean(x * x, axis=-1, keepdims=True)
    return x * lax.rsqrt(ms + NORM_EPS) * gain_row


def _proj_kernel(x_ref, nm_ref, w_ref, qg_ref, kg_ref, ones_ref,
                 q_ref, k_ref, kb_ref, v_ref, vb_ref, r_ref, g_ref):
    h = _rms_rows(x_ref[...], nm_ref[...]).astype(BF16)
    ones = ones_ref[...]

    def head_norm(z, gain_row):
        ss = _group_sum(z * z, ones)
        return z * lax.rsqrt(ss * (1.0 / A_HEAD_DIM) + NORM_EPS) * gain_row

    q = head_norm(_dot(h, w_ref[:, OFF_Q:OFF_K]), qg_ref[...])
    q_ref[...] = (q * (A_HEAD_DIM ** -0.5)).astype(BF16)
    tm = x_ref.shape[0]
    k = head_norm(_dot(h, w_ref[:, OFF_K:OFF_V]), kg_ref[...])
    kb_ref[...] = k.astype(BF16)
    v = _dot(h, w_ref[:, OFF_V:OFF_R])
    vb_ref[...] = v.astype(BF16)
    for hd in range(A_HEADS):
        sl = slice(hd * A_QK_DIM, (hd + 1) * A_QK_DIM)
        k_ref[pl.ds(hd, tm, stride=A_HEADS), :] = k[:, sl]
        v_ref[pl.ds(hd, tm, stride=A_HEADS), :] = v[:, sl]
    step = 512
    for c in range(0, R_SHIFT_W, step):
        n = min(step, R_SHIFT_W - c)
        r_ref[:, c:c + n] = _dot(h, w_ref[:, OFF_R + c:OFF_R + c + n])
    for c in range(0, 2 * D_MODEL, step):
        g_ref[:, c:c + step] = _sigmoid(_dot(h, w_ref[:, OFF_G + c:OFF_G + c + step]))


def _proj(x2, nm, w_in_bf, qg, kg, ones_bf, tm):
    m = x2.shape[0]
    row = lambda i: (i, 0)
    const = lambda i: (0, 0)
    out_shape = (
        jax.ShapeDtypeStruct((m, A_WIDTH), BF16),
        jax.ShapeDtypeStruct((m * A_HEADS, A_QK_DIM), F32),
        jax.ShapeDtypeStruct((m, A_WIDTH), BF16),
        jax.ShapeDtypeStruct((m * A_HEADS, A_V_DIM), F32),
        jax.ShapeDtypeStruct((m, A_WIDTH), BF16),
        jax.ShapeDtypeStruct((m, R_SHIFT_W), F32),
        jax.ShapeDtypeStruct((m, 2 * D_MODEL), F32),
    )
    return pl.pallas_call(
        _proj_kernel,
        grid=(m // tm,),
        in_specs=[
            pl.BlockSpec((tm, D_MODEL), row),
            pl.BlockSpec((1, D_MODEL), const),
            pl.BlockSpec((D_MODEL, IN_WIDTH), const),
            pl.BlockSpec((1, A_WIDTH), const),
            pl.BlockSpec((1, A_WIDTH), const),
            pl.BlockSpec((A_WIDTH, A_WIDTH), const),
        ],
        out_specs=[
            pl.BlockSpec((tm, A_WIDTH), row),
            pl.BlockSpec((tm * A_HEADS, A_QK_DIM), row),
            pl.BlockSpec((tm, A_WIDTH), row),
            pl.BlockSpec((tm * A_HEADS, A_V_DIM), row),
            pl.BlockSpec((tm, A_WIDTH), row),
            pl.BlockSpec((tm, R_SHIFT_W), row),
            pl.BlockSpec((tm, 2 * D_MODEL), row),
        ],
        out_shape=out_shape,
        compiler_params=pltpu.CompilerParams(
            dimension_semantics=("arbitrary",), vmem_limit_bytes=V7X_VMEM_LIMIT),
        name="proj",
    )(x2, nm, w_in_bf, qg, kg, ones_bf)


def _lambda_full(lq1, lk1, lq2, lk2, lambda_init):
    s1 = jnp.sum(lq1 * lk1, axis=-1, keepdims=True)
    s2 = jnp.sum(lq2 * lk2, axis=-1, keepdims=True)
    return jnp.exp(s1) - jnp.exp(s2) + lambda_init


ATTN_ACC_ROWS = A_V_DIM + 16
ATTN_HEADS_PER_STEP = 1


def _alibi_lanes(pos, slope, keys):
    l4 = _iota(pos.shape, 1) & (A_HEAD_DIM - 1)
    lo = slope * (pos & 255).astype(F32)
    hi = slope * ((pos >> 8) << 8).astype(F32)
    if keys:
        val = jnp.where(l4 < 2, 1.0, jnp.where(l4 == 2, lo, jnp.where(l4 == 3, hi, 0.0)))
    else:
        val = jnp.where(l4 == 0, -lo, jnp.where(l4 == 1, -hi, jnp.where(l4 < 4, 1.0, 0.0)))
    return val.astype(BF16)


def _prompt_attn_kernel(q_ref, k_ref, v_ref, lq1_ref, lk1_ref, lq2_ref, lk2_ref, gain_ref,
                        o_ref, kc_ref, vt_ref, acc_ref, *, tq, lambda_init):
    hg = pl.program_id(1)
    t = k_ref.shape[0]
    hp = ATTN_HEADS_PER_STEP
    n_st = 2 * hp
    slopes = [jnp.exp2(jnp.full((1, 1), -8.0 / A_HEADS, F32) * (hg * hp + hh + 1).astype(F32))
              for hh in range(hp)]
    head_lanes = [slice(hh * A_QK_DIM, (hh + 1) * A_QK_DIM) for hh in range(hp)]
    pos = _iota((tq, A_QK_DIM), 0)
    low = _iota((tq, A_QK_DIM), 1) < A_HEAD_DIM

    ones_rows = (_iota((ATTN_ACC_ROWS - A_V_DIM, tq), 0) == 0).astype(BF16)
    for hh in range(hp):
        ek = _alibi_lanes(pos, slopes[hh], keys=True)
        for jc in range(t // tq):
            rows = slice(jc * tq, (jc + 1) * tq)
            kt = k_ref[rows, head_lanes[hh]]
            kc_ref[2 * hh, rows, :] = jnp.where(low, kt, ek)
            kc_ref[2 * hh + 1, rows, :] = jnp.where(low, ek, kt)
            vt_ref[hh, jc, 0:A_V_DIM, :] = v_ref[rows, head_lanes[hh]].astype(F32).T.astype(BF16)
            vt_ref[hh, jc, A_V_DIM:, :] = ones_rows

    eqs = [_alibi_lanes(pos, slopes[hh], keys=False) for hh in range(hp)]
    lam = _lambda_full(lq1_ref[...], lk1_ref[...], lq2_ref[...], lk2_ref[...], lambda_init)
    kv_pos = _iota((tq, tq), 0)
    q_pos = _iota((tq, tq), 1)

    def q_tile(i, carry):
        i = jnp.asarray(i, jnp.int32)
        q_rows = pl.ds(pl.multiple_of(i * tq, tq), tq)
        qc = []
        for hh in range(hp):
            q = q_ref[q_rows, head_lanes[hh]]
            qc += [jnp.where(low, q, eqs[hh]), jnp.where(low, eqs[hh], q)]

        def tiles(group, ms):
            s = []
            for j, masked in group:
                rows = pl.ds(pl.multiple_of(jnp.asarray(j * tq, jnp.int32), tq), tq)
                sj = [_dot_nt(kc_ref[st, rows, :], qc[st]) for st in range(n_st)]
                if masked:
                    sj = [jnp.where(kv_pos <= q_pos, sc, NEG_BIG) for sc in sj]
                s.append(sj)
            ms = list(ms)
            acc = [acc_ref[st] for st in range(n_st)]
            for (j, _), sj in zip(group, s):
                for st in range(n_st):
                    off = slopes[st // 2] * jnp.asarray((j - i) * tq, F32)
                    m_rel = ms[st] - off
                    m_new = jnp.maximum(m_rel, jnp.max(sj[st], axis=0, keepdims=True))
                    p = jnp.exp(sj[st] - m_new).astype(BF16)
                    acc[st] = jnp.exp(m_rel - m_new) * acc[st] + _dot(vt_ref[st // 2, j], p)
                    ms[st] = m_new + off
            for st in range(n_st):
                acc_ref[st] = acc[st]
            return tuple(ms)

        acc_ref[...] = jnp.zeros_like(acc_ref)
        ms = lax.fori_loop(0, i // 2,
                           lambda pr, ms: tiles([(2 * pr, False), (2 * pr + 1, False)], ms),
                           tuple(jnp.full((1, tq), NEG_BIG, F32) for _ in range(n_st)))
        lax.cond((i & 1) == 1,
                 lambda ms: tiles([(i - 1, False), (i, True)], ms),
                 lambda ms: tiles([(i, True)], ms), ms)
        for hh in range(hp):
            a0, a1 = acc_ref[2 * hh], acc_ref[2 * hh + 1]
            o_t = (a0[0:A_V_DIM] / a0[A_V_DIM:A_V_DIM + 1]
                   - lam * (a1[0:A_V_DIM] / a1[A_V_DIM:A_V_DIM + 1]))
            ms_o = jnp.mean(o_t * o_t, axis=0, keepdims=True)
            o_t = o_t * lax.rsqrt(ms_o + NORM_EPS) * gain_ref[...] * (1.0 - lambda_init)
            o_ref[q_rows, head_lanes[hh]] = o_t.T.astype(BF16)
        return carry

    lax.fori_loop(0, t // tq, q_tile, 0)


def _prompt_attn(qb, kb, vb, lq1, lk1, lq2, lk2, gain, b, t, tq, lambda_init):
    nq = t // tq
    hp = ATTN_HEADS_PER_STEP
    vec = lambda bi, h: (0, 0)
    seq_head = lambda bi, h: (bi, h)
    return pl.pallas_call(
        functools.partial(_prompt_attn_kernel, tq=tq, lambda_init=lambda_init),
        grid=(b, A_HEADS // hp),
        in_specs=[
            pl.BlockSpec((t, hp * A_QK_DIM), seq_head),
            pl.BlockSpec((t, hp * A_QK_DIM), seq_head),
            pl.BlockSpec((t, hp * A_V_DIM), seq_head),
            pl.BlockSpec((1, A_HEAD_DIM), vec),
            pl.BlockSpec((1, A_HEAD_DIM), vec),
            pl.BlockSpec((1, A_HEAD_DIM), vec),
            pl.BlockSpec((1, A_HEAD_DIM), vec),
            pl.BlockSpec((A_V_DIM, 1), vec),
        ],
        out_specs=pl.BlockSpec((t, hp * A_V_DIM), seq_head),
        out_shape=jax.ShapeDtypeStruct((b * t, A_WIDTH), BF16),
        scratch_shapes=[pltpu.VMEM((2 * hp, t, A_QK_DIM), BF16),
                        pltpu.VMEM((hp, nq, ATTN_ACC_ROWS, tq), BF16),
                        pltpu.VMEM((2 * hp, ATTN_ACC_ROWS, tq), F32)],
        compiler_params=pltpu.CompilerParams(
            dimension_semantics=("arbitrary", "arbitrary"),
            vmem_limit_bytes=V7X_VMEM_LIMIT),
        name="prompt_attn",
    )(qb, kb, vb, lq1, lk1, lq2, lk2, gain.reshape(A_V_DIM, 1))


def _decode_attn_kernel(pt_ref, q_ref, kn_ref, vn_ref, lq1_ref, lk1_ref, lq2_ref, lk2_ref,
                        gain_ref, *refs, pps, n_pages, lambda_init):
    k_refs = refs[:pps]
    v_refs = refs[pps:2 * pps]
    o_ref = refs[2 * pps]
    m_ref, l_ref, acc_ref = refs[2 * pps + 1:]
    j = pl.program_id(1)
    rows = 2 * A_HEADS
    page_rows = PAGE_SIZE * A_HEADS
    row_q = _iota((rows, A_QK_DIM), 0)
    lane_q = _iota((rows, A_QK_DIM), 1)
    qm = jnp.where((lane_q >> 6) == (row_q & 1), q_ref[0], 0.0).astype(BF16)
    slope = jnp.exp2((-8.0 / A_HEADS) * ((_iota((rows, 1), 0) >> 1) + 1).astype(F32))
    col = _iota((rows, pps * page_rows), 1)
    own_head = (col & (A_HEADS - 1)) == (_iota((rows, pps * page_rows), 0) >> 1)
    dist = (n_pages - j * pps) * PAGE_SIZE - (col >> 2)

    @pl.when(j == 0)
    def _():
        m_ref[...] = jnp.full((rows, 1), NEG_BIG, F32)
        l_ref[...] = jnp.zeros((rows, 1), F32)
        acc_ref[...] = jnp.zeros((rows, A_V_DIM), F32)

    s = jnp.concatenate([_dot_nt(qm, k_refs[u][...].astype(BF16)) for u in range(pps)], axis=1)
    s = jnp.where(own_head, s - slope * dist.astype(F32), NEG_BIG)
    m = m_ref[...]
    m_new = jnp.maximum(m, jnp.max(s, axis=-1, keepdims=True))
    alpha = jnp.exp(m - m_new)
    p = jnp.exp(s - m_new)
    pb = p.astype(BF16)
    pv = _dot(pb[:, 0:page_rows], v_refs[0][...].astype(BF16))
    for u in range(1, pps):
        pv = pv + _dot(pb[:, u * page_rows:(u + 1) * page_rows], v_refs[u][...].astype(BF16))
    m = m_new
    l = alpha * l_ref[...] + jnp.sum(p, axis=-1, keepdims=True)
    acc = alpha * acc_ref[...] + pv
    m_ref[...], l_ref[...], acc_ref[...] = m, l, acc

    @pl.when(j == pl.num_programs(1) - 1)
    def _():
        s_new = jnp.sum(qm.astype(F32) * kn_ref[0], axis=-1, keepdims=True)
        m_new = jnp.maximum(m, s_new)
        alpha = jnp.exp(m - m_new)
        p_new = jnp.exp(s_new - m_new)
        l_fin = alpha * l + p_new
        acc_fin = alpha * acc + p_new.astype(BF16).astype(F32) * vn_ref[0]
        lam = _lambda_full(lq1_ref[...], lk1_ref[...], lq2_ref[...], lk2_ref[...], lambda_init)
        comp = _iota((rows, 1), 0) & 1
        x = acc_fin * (jnp.where(comp == 0, 1.0, -lam) / l_fin)
        o = x + pltpu.roll(x, shift=rows - 1, axis=0)
        o_ref[0] = _rms_rows(o, gain_ref[...]) * (1.0 - lambda_init)


def _decode_attn(page_table, qb, kb, vb, cache_k, cache_v, layer, lq1, lk1, lq2, lk2, gain,
                 pps, lambda_init):
    db, n_pages = page_table.shape
    n_pool = cache_k.shape[1]
    page_rows = PAGE_SIZE * A_HEADS
    ck = cache_k.reshape(-1, A_QK_DIM)
    cv = cache_v.reshape(-1, A_V_DIM)
    pt = page_table.reshape(db * n_pages)
    rows = 2 * A_HEADS
    tok3 = lambda b, j, pt_ref: (b, 0, 0)
    vec = lambda b, j, pt_ref: (0, 0)

    def page_spec(u):
        return pl.BlockSpec(
            (page_rows, A_QK_DIM),
            lambda b, j, pt_ref: (layer * n_pool + pt_ref[b * n_pages + j * pps + u], 0))

    def per_row(z):
        return jnp.repeat(z.astype(F32).reshape(db, A_HEADS, A_QK_DIM), 2, axis=1)

    grid_spec = pltpu.PrefetchScalarGridSpec(
        num_scalar_prefetch=1,
        grid=(db, n_pages // pps),
        in_specs=[
            pl.BlockSpec((1, rows, A_QK_DIM), tok3),
            pl.BlockSpec((1, rows, A_QK_DIM), tok3),
            pl.BlockSpec((1, rows, A_V_DIM), tok3),
            pl.BlockSpec((1, A_HEAD_DIM), vec),
            pl.BlockSpec((1, A_HEAD_DIM), vec),
            pl.BlockSpec((1, A_HEAD_DIM), vec),
            pl.BlockSpec((1, A_HEAD_DIM), vec),
            pl.BlockSpec((1, A_V_DIM), vec),
        ] + [page_spec(u) for u in range(pps)] + [page_spec(u) for u in range(pps)],
        out_specs=pl.BlockSpec((1, rows, A_V_DIM), tok3),
        scratch_shapes=[pltpu.VMEM((rows, 1), F32), pltpu.VMEM((rows, 1), F32),
                        pltpu.VMEM((rows, A_V_DIM), F32)],
    )
    out = pl.pallas_call(
        functools.partial(_decode_attn_kernel, pps=pps, n_pages=n_pages, lambda_init=lambda_init),
        grid_spec=grid_spec,
        out_shape=jax.ShapeDtypeStruct((db, rows, A_V_DIM), F32),
        compiler_params=pltpu.CompilerParams(
            dimension_semantics=("arbitrary", "arbitrary"), vmem_limit_bytes=V7X_VMEM_LIMIT),
        name="decode_attn",
    )(pt, per_row(qb), per_row(kb), per_row(vb),
      lq1, lk1, lq2, lk2, gain, *([ck] * pps), *([cv] * pps))
    return out[:, 0::2, :].reshape(db, A_WIDTH)


def _rwkv_prep(x, prev, mu, w0, w2e, a0, a2e, g2, k_k, k_a, r_k, ones):
    m = x + (prev - x) * mu
    r = m[:, 0:R_WIDTH]
    k = m[:, R_WIDTH:2 * R_WIDTH]
    v = m[:, 2 * R_WIDTH:3 * R_WIDTH]
    xwa = m[:, 3 * R_WIDTH:3 * R_WIDTH + R_LORA_W + R_LORA_A]
    xg = m[:, 3 * R_WIDTH + R_LORA_W + R_LORA_A:]
    z = w0 + _dot(jnp.tanh(xwa).astype(BF16), w2e)
    lw = (-math.exp(-0.5)) * _sigmoid(z)
    a = _sigmoid(a0 + _dot(xwa.astype(BF16), a2e))
    g = _dot(_sigmoid(xg).astype(BF16), g2)
    kk = k * k_k
    kk = kk * lax.rsqrt(jnp.maximum(_group_sum(kk * kk, ones), 1e-24))
    k = k * (1.0 + (a - 1.0) * k_a)
    bonus = _group_sum(r * k * r_k, ones) * v
    return r, lw, k, v, kk, a, g, bonus


def _rwkv_finish(y, bonus, g, gn_w, gn_b, ones):
    mu = _group_sum(y, ones) * (1.0 / R_HEAD)
    d = y - mu
    var = _group_sum(d * d, ones) * (1.0 / R_HEAD)
    yn = d * lax.rsqrt(var + GN_EPS) * gn_w + gn_b
    return ((yn + bonus) * g).astype(BF16)


def _lane_halves(x, width):
    lane = _iota(x.shape, 1)
    zero = jnp.zeros_like(x)
    return jnp.concatenate([jnp.where(lane < width, x, zero), jnp.where(lane >= width, x, zero)],
                           axis=0)


def _lane_blocks(x, width):
    block = _iota(x.shape, 1) >> int(math.log2(width))
    zero = jnp.zeros_like(x)
    return jnp.concatenate([jnp.where(block == b, x, zero) for b in range(x.shape[1] // width)],
                           axis=0)


def _rwkv_chunk_kernel(x_ref, *refs, chunk, nb):
    st_ref, prev_ref = refs[-3], refs[-1]

    @pl.when(pl.program_id(1) == 0)
    def _():
        prev_ref[...] = jnp.zeros_like(prev_ref)
        st_ref[...] = jnp.zeros_like(st_ref)

    def body(k, carry):
        _rwkv_one_chunk(jnp.asarray(k, jnp.int32), x_ref, *refs, chunk=chunk, nb=nb)
        return carry

    lax.fori_loop(0, x_ref.shape[1] // chunk, body, 0)


def _rwkv_one_chunk(k, x_ref, mu_ref, w0_ref, w2e_ref, a0_ref, a2e_ref, g2_ref, kk_ref, ka_ref,
                    rk_ref, gnw_ref, gnb_ref, ones_ref,
                    y_ref, st_ref, sh_ref, prev_ref, *, chunk, nb):
    c = chunk
    chunk_rows = pl.ds(pl.multiple_of(k * c, c), c)
    ones = ones_ref[...]
    rows = _iota((c, R_SHIFT_W), 0)
    tri = (_iota((c, c), 0) >= _iota((c, c), 1)).astype(BF16)
    sls = [slice(p * LANES, (p + 1) * LANES) for p in range(R_PAIRS)]
    ah_p, rh_p, v_p, bt_p, kt_p, bb_p, kb_p, pe_p, st_idx, fin = ([] for _ in range(10))
    for bi in range(nb):
        x = x_ref[bi, chunk_rows, :]
        prev = jnp.where(rows == 0, prev_ref[bi], pltpu.roll(x, shift=1, axis=0))
        prev_ref[bi] = x[c - 1:c, :]
        sh_ref[bi] = x[c - 1:c, :]
        r, lw, k, v, kk, a, g, bonus = _rwkv_prep(
            x, prev, mu_ref[...], w0_ref[...], w2e_ref[...], a0_ref[...], a2e_ref[...],
            g2_ref[...], kk_ref[...], ka_ref[...], rk_ref[...], ones)
        fin.append((bonus, g))
        l_hi = lw.astype(BF16)
        l_r1 = lw - l_hi.astype(F32)
        l_mid = l_r1.astype(BF16)
        l_lo = (l_r1 - l_mid.astype(F32)).astype(BF16)
        cum = _dot(tri, l_hi) + _dot(tri, l_mid) + _dot(tri, l_lo)
        cum_end = cum[c - 1:c, :]
        e_neg = jnp.exp(-cum)
        e_end = jnp.exp(cum_end - cum)
        kka = kk * a
        ah = -kk * jnp.exp(cum - lw)
        rh = r * jnp.exp(cum)
        bt = kka * e_neg
        kt = k * e_neg
        bb = kka * e_end
        kb = k * e_end
        p_end = jnp.exp(cum_end)
        for p, sl in enumerate(sls):
            for lst, z in ((ah_p, ah), (rh_p, rh), (v_p, v), (bt_p, bt), (kt_p, kt),
                           (bb_p, bb), (kb_p, kb), (pe_p, p_end)):
                lst.append(z[:, sl])
            st_idx.append((bi, p))

    t_idx = _iota((c, 2 * c), 0)
    s_idx = _iota((c, 2 * c), 1) & (c - 1)
    strict = t_idx > s_idx
    incl = t_idx >= s_idx
    eye_pack = (t_idx == s_idx).astype(F32)
    bd_mask = (_iota((LANES, LANES), 0) >> 6) == (_iota((LANES, LANES), 1) >> 6)
    eye_l = _iota((LANES, LANES), 0) == _iota((LANES, LANES), 1)
    zeros_c = jnp.zeros((c, LANES), F32)

    units = range(len(st_idx))
    v_blk = [_lane_halves(z, R_HEAD) for z in v_p]
    a_all = [_mm(jnp.concatenate([ah_p[u], rh_p[u]], axis=0),
                 jnp.concatenate([_lane_halves(bt_p[u], R_HEAD), _lane_halves(kt_p[u], R_HEAD)],
                                 axis=0), _dot_nt) for u in units]
    a_ab = [jnp.where(strict, z[0:c, 0:2 * c], 0.0) for z in a_all]
    a_ak = [jnp.where(strict, z[0:c, 2 * c:4 * c], 0.0) for z in a_all]
    a_rb = [jnp.where(incl, z[c:2 * c, 0:2 * c], 0.0) for z in a_all]
    a_rk = [jnp.where(incl, z[c:2 * c, 2 * c:4 * c], 0.0) for z in a_all]
    w1 = [_mm(a_ak[u], v_blk[u]) for u in units]
    quads = range(len(st_idx) // 2)
    a4 = [jnp.concatenate([a_ab[2 * q], a_ab[2 * q + 1]], axis=1) for q in quads]
    eye4 = jnp.concatenate([eye_pack, eye_pack], axis=1)
    inv4 = [eye4 + z for z in a4]
    bd = [_lane_blocks(z, c) for z in a4]
    pw = [_mm(a4[q], bd[q]) for q in quads]
    n_sq = int(math.log2(c)) - 1
    for step in range(n_sq):
        bd = [_lane_blocks(z, c) for z in pw]
        if step < n_sq - 1:
            both = [_mm(jnp.concatenate([pw[q], inv4[q]], axis=0), bd[q]) for q in quads]
            pw = [z[0:c] for z in both]
            inv4 = [inv4[q] + both[q][c:2 * c] for q in quads]
        else:
            inv4 = [inv4[q] + _mm(inv4[q], bd[q]) for q in quads]
    inv = [inv4[u // 2][:, (u % 2) * 2 * c:(u % 2 + 1) * 2 * c] for u in units]
    t4 = [_mm(inv[u], jnp.concatenate([_lane_halves(w1[u], R_HEAD),
                                       _lane_halves(ah_p[u], R_HEAD)], axis=1)) for u in units]
    wu = [z[:, 0:LANES] for z in t4]
    ah2 = [z[:, LANES:] for z in t4]
    zeros_blk = jnp.zeros((2 * c, LANES), F32)
    t5 = [_mm(jnp.concatenate([a_rb[u], a_rk[u]], axis=1),
              jnp.concatenate([
                  jnp.concatenate([_lane_halves(wu[u], R_HEAD), _lane_halves(ah2[u], R_HEAD)],
                                  axis=1),
                  jnp.concatenate([v_blk[u], zeros_blk], axis=1)], axis=0)) for u in units]
    t6 = [_mm(jnp.concatenate([bb_p[u], kb_p[u]], axis=0),
              jnp.concatenate([jnp.concatenate([wu[u], ah2[u]], axis=1),
                               jnp.concatenate([v_p[u], zeros_c], axis=1)], axis=0), _dot_tn)
          for u in units]
    st = [st_ref[bi, p] for bi, p in st_idx]
    ys = [_mm(rh_p[u] + t5[u][:, LANES:], st[u]) + t5[u][:, 0:LANES] for u in units]
    for u in units:
        h_new = jnp.where(bd_mask, t6[u][:, 0:LANES], 0.0)
        g_x = jnp.where(bd_mask, t6[u][:, LANES:], 0.0)
        p_col = jnp.sum(jnp.where(eye_l, pe_p[u], 0.0), axis=1, keepdims=True)
        st_ref[st_idx[u]] = p_col * st[u] + _mm(g_x, st[u]) + h_new

    for bi in range(nb):
        y = jnp.concatenate(ys[bi * R_PAIRS:(bi + 1) * R_PAIRS], axis=1)
        bonus, g = fin[bi]
        y_ref[bi, chunk_rows, :] = _rwkv_finish(y, bonus, g, gnw_ref[...], gnb_ref[...], ones)


def _rwkv_prompt(rc3, params, chunk, nb):
    b, t, _ = rc3.shape
    rows = _pick_tile(t, RWKV_CHUNKS_PER_STEP * chunk)
    const = lambda bi, j: (0, 0)
    specs = [pl.BlockSpec(p.shape, const) for p in params]
    return pl.pallas_call(
        functools.partial(_rwkv_chunk_kernel, chunk=chunk, nb=nb),
        grid=(b // nb, t // rows),
        in_specs=[pl.BlockSpec((nb, rows, R_SHIFT_W), lambda bi, j: (bi, j, 0))] + specs,
        out_specs=[
            pl.BlockSpec((nb, rows, R_WIDTH), lambda bi, j: (bi, j, 0)),
            pl.BlockSpec((nb, R_PAIRS, LANES, LANES), lambda bi, j: (bi, 0, 0, 0)),
            pl.BlockSpec((nb, 1, R_SHIFT_W), lambda bi, j: (bi, 0, 0)),
        ],
        out_shape=(
            jax.ShapeDtypeStruct((b, t, R_WIDTH), BF16),
            jax.ShapeDtypeStruct((b, R_PAIRS, LANES, LANES), F32),
            jax.ShapeDtypeStruct((b, 1, R_SHIFT_W), F32),
        ),
        scratch_shapes=[pltpu.VMEM((nb, 1, R_SHIFT_W), F32)],
        compiler_params=pltpu.CompilerParams(
            dimension_semantics=("arbitrary", "arbitrary"), vmem_limit_bytes=V7X_VMEM_LIMIT),
        name="rwkv_chunk",
    )(rc3, *params)


def _rwkv_step_kernel(x_ref, prev_ref, st_in_ref, mu_ref, w0_ref, w2e_ref, a0_ref, a2e_ref,
                      g2_ref, kk_ref, ka_ref, rk_ref, gnw_ref, gnb_ref, ones_ref,
                      y_ref, st_ref, yrow_ref, *, rows):
    ones = ones_ref[...]
    r, lw, k, v, kk, a, g, bonus = _rwkv_prep(
        x_ref[...], prev_ref[...], mu_ref[...], w0_ref[...], w2e_ref[...], a0_ref[...],
        a2e_ref[...], g2_ref[...], kk_ref[...], ka_ref[...], rk_ref[...], ones)
    vecs = (jnp.exp(lw), kk * a, k, v, kk, r)
    row8 = _iota((8, R_HEAD), 0)

    def rows8(*row_vecs):
        out = jnp.zeros((8, R_HEAD), F32)
        for n, vec in enumerate(row_vecs):
            out = jnp.where(row8 == n, vec, out)
        return out

    def hi_lo(x):
        hi = x.astype(BF16).astype(F32)
        return hi, x - hi

    def vec_mat(vec_row, s):
        return _mm(jnp.broadcast_to(vec_row, (8, R_HEAD)), s, _dot_nt)[0:1]

    group = math.gcd(rows, 4)
    for g0 in range(0, rows, group):
        units = [(bi, h) for bi in range(g0, g0 + group) for h in range(R_HEADS)]
        w_r, b_r, k_r, v_r, kk_r, r_r = (
            [z[bi:bi + 1, h * R_HEAD:(h + 1) * R_HEAD] for bi, h in units] for z in vecs)
        ids = range(len(units))
        s_old = [st_in_ref[bi, h] for bi, h in units]
        sa = [-vec_mat(kk_r[u], s_old[u]) for u in ids]
        upd = []
        for u in ids:
            (sa_h, sa_l), (b_h, b_l) = hi_lo(sa[u]), hi_lo(b_r[u])
            (v_h, v_l), (k_h, k_l) = hi_lo(v_r[u]), hi_lo(k_r[u])
            upd.append(_mm(rows8(sa_h, sa_h, sa_l, v_h, v_h, v_l),
                           rows8(b_h, b_l, b_h, k_h, k_l, k_h), _dot_tn))
        s_new = [s_old[u] * w_r[u] + upd[u] for u in ids]
        for u, (bi, h) in enumerate(units):
            st_ref[bi, h] = s_new[u]
        y_r = [vec_mat(r_r[u], s_new[u]) for u in ids]
        for n, bi in enumerate(range(g0, g0 + group)):
            yrow_ref[bi:bi + 1, :] = jnp.concatenate(y_r[n * R_HEADS:(n + 1) * R_HEADS], axis=1)
    y_ref[...] = _rwkv_finish(yrow_ref[...], bonus, g, gnw_ref[...], gnb_ref[...], ones)


def _rwkv_step(rc, shift_prev, state, params, rows):
    db = rc.shape[0]
    const = lambda i: (0, 0)
    specs = [pl.BlockSpec(p.shape, const) for p in params]
    state_spec = pl.BlockSpec((rows, R_HEADS, R_HEAD, R_HEAD), lambda i: (i, 0, 0, 0))
    return pl.pallas_call(
        functools.partial(_rwkv_step_kernel, rows=rows),
        grid=(db // rows,),
        in_specs=[
            pl.BlockSpec((rows, R_SHIFT_W), lambda i: (i, 0)),
            pl.BlockSpec((rows, R_SHIFT_W), lambda i: (i, 0)),
            state_spec,
        ] + specs,
        out_specs=[pl.BlockSpec((rows, R_WIDTH), lambda i: (i, 0)), state_spec],
        out_shape=(
            jax.ShapeDtypeStruct((db, R_WIDTH), BF16),
            jax.ShapeDtypeStruct((db, R_HEADS, R_HEAD, R_HEAD), F32),
        ),
        scratch_shapes=[pltpu.VMEM((rows, R_WIDTH), F32)],
        compiler_params=pltpu.CompilerParams(
            dimension_semantics=("arbitrary",), vmem_limit_bytes=V7X_VMEM_LIMIT),
        name="rwkv_step",
    )(rc, shift_prev, state, *params)


def _pairs_to_state(st):
    b = st.shape[0]
    blocks = st.reshape(b, R_PAIRS, 2, R_HEAD, 2, R_HEAD)
    diag = jnp.stack([blocks[:, :, 0, :, 0, :], blocks[:, :, 1, :, 1, :]], axis=2)
    return jnp.swapaxes(diag.reshape(b, R_HEADS, R_HEAD, R_HEAD), -1, -2)


def _merge_ffn_kernel(x_ref, o_ref, rw_ref, g_ref, wpa_ref, wpb_ref, wout_ref, nf_ref,
                      wg_ref, wu_ref, wd_ref, y_ref, *, d_ff):
    ya = _dot(o_ref[...].astype(BF16), wpa_ref[...])
    yb = _dot(rw_ref[...], wpb_ref[...])
    merged = (g_ref[:, 0:D_MODEL] * ya + g_ref[:, D_MODEL:] * yb).astype(BF16)
    x1 = x_ref[...] + _dot(merged, wout_ref[...])
    hf = _rms_rows(x1, nf_ref[...]).astype(BF16)
    acc = x1
    step = 512
    for c in range(0, d_ff, step):
        n = min(step, d_ff - c)
        gate = _dot(hf, wg_ref[:, c:c + n])
        up = _dot(hf, wu_ref[:, c:c + n])
        act = (gate * _sigmoid(gate) * up).astype(BF16)
        acc = acc + _dot(act, wd_ref[c:c + n, :])
    y_ref[...] = acc


def _merge_ffn(x2, o, rw, gates, wpa, wpb, wout, nf, wg, wu, wd, tm):
    m = x2.shape[0]
    d_ff = wg.shape[1]
    row = lambda i: (i, 0)
    const = lambda i: (0, 0)
    return pl.pallas_call(
        functools.partial(_merge_ffn_kernel, d_ff=d_ff),
        grid=(m // tm,),
        in_specs=[
            pl.BlockSpec((tm, D_MODEL), row),
            pl.BlockSpec((tm, A_WIDTH), row),
            pl.BlockSpec((tm, R_WIDTH), row),
            pl.BlockSpec((tm, 2 * D_MODEL), row),
            pl.BlockSpec(wpa.shape, const),
            pl.BlockSpec(wpb.shape, const),
            pl.BlockSpec(wout.shape, const),
            pl.BlockSpec((1, D_MODEL), const),
            pl.BlockSpec(wg.shape, const),
            pl.BlockSpec(wu.shape, const),
            pl.BlockSpec(wd.shape, const),
        ],
        out_specs=pl.BlockSpec((tm, D_MODEL), row),
        out_shape=jax.ShapeDtypeStruct((m, D_MODEL), F32),
        compiler_params=pltpu.CompilerParams(
            dimension_semantics=("arbitrary",), vmem_limit_bytes=V7X_VMEM_LIMIT),
        name="merge_ffn",
    )(x2, o, rw, gates, wpa, wpb, wout, nf, wg, wu, wd)


def _row(v):
    return v.reshape(1, -1).astype(F32)


def _pick_tile(m, target):
    t = min(m, target)
    while m % t:
        t //= 2
    return t


def kernel(x_prompt, x_sample, cache_k, cache_v, page_table, state_wkv, state_shift, norm_mix, w_in, q_gain, k_gain, lambda_q1, lambda_k1, lambda_q2, lambda_k2, attn_out_gain, w_pa, shift_mu, w0, w2, a0, a2, g2, k_k, k_a, r_k, gn_w, gn_b, w_pb, w_out, norm_ffn, w_gate, w_up, w_down):
    b, t, _ = x_prompt.shape
    db, ds, _ = x_sample.shape
    assert ds == 1, "sample group carries one new token per sequence"
    depth = w_in.shape[0]
    n_pages = page_table.shape[1]
    xp = x_prompt.reshape(b * t, D_MODEL)
    xs = x_sample.reshape(db, D_MODEL)
    head_ones = (jnp.arange(A_WIDTH)[:, None] // A_HEAD_DIM
                 == jnp.arange(A_WIDTH)[None, :] // A_HEAD_DIM).astype(BF16)
    outs = [[] for _ in range(8)]
    tm_p = _pick_tile(b * t, 512)
    tm_s = _pick_tile(db, 512)
    tq = _pick_tile(t, 512)
    chunk = _pick_tile(t, RWKV_CHUNK)
    chunk_nb = _pick_tile(b, 4)
    pps = _pick_tile(n_pages, 16)
    step_rows = _pick_tile(db, 16)
    for l in range(depth):
        lambda_init = 0.8 - 0.6 * math.exp(-0.3 * l)
        w_in_bf = w_in[l].astype(BF16)
        qg = _row(jnp.tile(q_gain[l], A_WIDTH // A_HEAD_DIM))
        kg = _row(jnp.tile(k_gain[l], A_WIDTH // A_HEAD_DIM))
        nm = _row(norm_mix[l])
        lam_vecs = (_row(lambda_q1[l]), _row(lambda_k1[l]), _row(lambda_q2[l]), _row(lambda_k2[l]))
        og = _row(attn_out_gain[l])
        zeros_lora = jnp.zeros((R_LORA_W, R_WIDTH), F32)
        rwkv_params = (
            _row(shift_mu[l]), _row(w0[l]),
            jnp.concatenate([w2[l], zeros_lora], axis=0).astype(BF16),
            _row(a0[l]),
            jnp.concatenate([zeros_lora, a2[l]], axis=0).astype(BF16),
            g2[l].astype(BF16), _row(k_k[l]), _row(k_a[l]), _row(r_k[l]),
            _row(gn_w[l]), _row(gn_b[l]), head_ones)
        ffn_w = (w_pa[l].astype(BF16), w_pb[l].astype(BF16), w_out[l].astype(BF16),
                 _row(norm_ffn[l]), w_gate[l].astype(BF16), w_up[l].astype(BF16),
                 w_down[l].astype(BF16))

        qb, k32, kb, v32, vb, rc, gates = _proj(xp, nm, w_in_bf, qg, kg, head_ones, tm_p)
        o = _prompt_attn(qb, kb, vb, *lam_vecs, og, b, t, tq, lambda_init)
        rw, st_p, sh_p = _rwkv_prompt(rc.reshape(b, t, R_SHIFT_W), rwkv_params, chunk, chunk_nb)
        xp = _merge_ffn(xp, o, rw.reshape(b * t, R_WIDTH), gates, *ffn_w, tm_p)
        outs[0].append(k32.reshape(b, t, A_HEADS, A_QK_DIM))
        outs[1].append(v32.reshape(b, t, A_HEADS, A_V_DIM))
        outs[2].append(_pairs_to_state(st_p))
        outs[3].append(sh_p.reshape(b, R_SHIFT_W))

        qb, k32, kb, v32, vb, rc, gates = _proj(xs, nm, w_in_bf, qg, kg, head_ones, tm_s)
        o = _decode_attn(page_table, qb, kb, vb, cache_k, cache_v, l, *lam_vecs, og,
                         pps, lambda_init)
        rw, st_s = _rwkv_step(rc, state_shift[l], state_wkv[l], rwkv_params,
                              step_rows)
        xs = _merge_ffn(xs, o, rw, gates, *ffn_w, tm_s)
        outs[4].append(k32.reshape(db, 1, A_HEADS, A_QK_DIM))
        outs[5].append(v32.reshape(db, 1, A_HEADS, A_V_DIM))
        outs[6].append(st_s)
        outs[7].append(rc)
    return (xp.reshape(b, t, D_MODEL), xs.reshape(db, 1, D_MODEL),
            *(jnp.stack(o) for o in outs))
```

```python
import functools
import math

import jax
import jax.numpy as jnp
from jax import lax
from jax.experimental import pallas as pl
from jax.experimental.pallas import tpu as pltpu

F32 = jnp.float32
BF16 = jnp.bfloat16

D_MODEL = 1024
PAGE_SIZE = 128
A_HEADS = 4
A_HEAD_DIM = 64
A_QK_DIM = 2 * A_HEAD_DIM
A_V_DIM = 2 * A_HEAD_DIM
A_WIDTH = A_HEADS * A_V_DIM
R_HEAD = 64
R_HEADS = 8
R_WIDTH = R_HEADS * R_HEAD
R_LORA_W = 64
R_LORA_A = 64
R_LORA_G = 128
R_SHIFT_W = 3 * R_WIDTH + R_LORA_W + R_LORA_A + R_LORA_G
OFF_Q = 0
OFF_K = OFF_Q + A_HEADS * A_QK_DIM
OFF_V = OFF_K + A_HEADS * A_QK_DIM
OFF_R = OFF_V + A_WIDTH
OFF_G = OFF_R + R_SHIFT_W
IN_WIDTH = OFF_G + 2 * D_MODEL
NORM_EPS = 1e-6
GN_EPS = 64e-5
NEG_BIG = -1e30

V7X_VMEM_LIMIT = 56 * 1024 * 1024
LANES = 128
R_PAIRS = R_WIDTH // LANES
RWKV_CHUNK = 64
RWKV_CHUNKS_PER_STEP = 4


def _dot(a, b):
    return jnp.dot(a, b, preferred_element_type=F32)


def _dot_nt(a, b):
    return lax.dot_general(a, b, (((1,), (1,)), ((), ())), preferred_element_type=F32)


def _dot_tn(a, b):
    return lax.dot_general(a, b, (((0,), (0,)), ((), ())), preferred_element_type=F32)


def _mm(a, b, dot=_dot):
    return dot(a.astype(BF16), b.astype(BF16))


def _sigmoid(x):
    return 1.0 / (1.0 + jnp.exp(-x))


def _iota(shape, dim):
    return lax.broadcasted_iota(jnp.int32, shape, dim)


def _group_sum(z, ones_bf):
    return _dot(z.astype(BF16), ones_bf)


def _rms_rows(x, gain_row):
    ms = jnp.mean(x * x, axis=-1, keepdims=True)
    return x * lax.rsqrt(ms + NORM_EPS) * gain_row


def _proj_kernel(x_ref, nm_ref, w_ref, qg_ref, kg_ref, ones_ref,
                 q_ref, k_ref, kb_ref, v_ref, vb_ref, r_ref, g_ref):
    h = _rms_rows(x_ref[...], nm_ref[...]).astype(BF16)
    ones = ones_ref[...]

    def head_norm(z, gain_row):
        ss = _group_sum(z * z, ones)
        return z * lax.rsqrt(ss * (1.0 / A_HEAD_DIM) + NORM_EPS) * gain_row

    q = head_norm(_dot(h, w_ref[:, OFF_Q:OFF_K]), qg_ref[...])
    q_ref[...] = (q * (A_HEAD_DIM ** -0.5)).astype(BF16)
    tm = x_ref.shape[0]
    k = head_norm(_dot(h, w_ref[:, OFF_K:OFF_V]), kg_ref[...])
    kb_ref[...] = k.astype(BF16)
    v = _dot(h, w_ref[:, OFF_V:OFF_R])
    vb_ref[...] = v.astype(BF16)
    for hd in range(A_HEADS):
        sl = slice(hd * A_QK_DIM, (hd + 1) * A_QK_DIM)
        k_ref[pl.ds(hd, tm, stride=A_HEADS), :] = k[:, sl]
        v_ref[pl.ds(hd, tm, stride=A_HEADS), :] = v[:, sl]
    step = 512
    for c in range(0, R_SHIFT_W, step):
        n = min(step, R_SHIFT_W - c)
        r_ref[:, c:c + n] = _dot(h, w_ref[:, OFF_R + c:OFF_R + c + n])
    for c in range(0, 2 * D_MODEL, step):
        g_ref[:, c:c + step] = _sigmoid(_dot(h, w_ref[:, OFF_G + c:OFF_G + c + step]))


def _proj(x2, nm, w_in_bf, qg, kg, ones_bf, tm):
    m = x2.shape[0]
    row = lambda i: (i, 0)
    const = lambda i: (0, 0)
    out_shape = (
        jax.ShapeDtypeStruct((m, A_WIDTH), BF16),
        jax.ShapeDtypeStruct((m * A_HEADS, A_QK_DIM), F32),
        jax.ShapeDtypeStruct((m, A_WIDTH), BF16),
        jax.ShapeDtypeStruct((m * A_HEADS, A_V_DIM), F32),
        jax.ShapeDtypeStruct((m, A_WIDTH), BF16),
        jax.ShapeDtypeStruct((m, R_SHIFT_W), F32),
        jax.ShapeDtypeStruct((m, 2 * D_MODEL), F32),
    )
    return pl.pallas_call(
        _proj_kernel,
        grid=(m // tm,),
        in_specs=[
            pl.BlockSpec((tm, D_MODEL), row),
            pl.BlockSpec((1, D_MODEL), const),
            pl.BlockSpec((D_MODEL, IN_WIDTH), const),
            pl.BlockSpec((1, A_WIDTH), const),
            pl.BlockSpec((1, A_WIDTH), const),
            pl.BlockSpec((A_WIDTH, A_WIDTH), const),
        ],
        out_specs=[
            pl.BlockSpec((tm, A_WIDTH), row),
            pl.BlockSpec((tm * A_HEADS, A_QK_DIM), row),
            pl.BlockSpec((tm, A_WIDTH), row),
            pl.BlockSpec((tm * A_HEADS, A_V_DIM), row),
            pl.BlockSpec((tm, A_WIDTH), row),
            pl.BlockSpec((tm, R_SHIFT_W), row),
            pl.BlockSpec((tm, 2 * D_MODEL), row),
        ],
        out_shape=out_shape,
        compiler_params=pltpu.CompilerParams(
            dimension_semantics=("arbitrary",), vmem_limit_bytes=V7X_VMEM_LIMIT),
        name="proj",
    )(x2, nm, w_in_bf, qg, kg, ones_bf)


def _lambda_full(lq1, lk1, lq2, lk2, lambda_init):
    s1 = jnp.sum(lq1 * lk1, axis=-1, keepdims=True)
    s2 = jnp.sum(lq2 * lk2, axis=-1, keepdims=True)
    return jnp.exp(s1) - jnp.exp(s2) + lambda_init


ATTN_ACC_ROWS = A_V_DIM + 16
ATTN_HEADS_PER_STEP = 1


def _alibi_lanes(pos, slope, keys):
    l4 = _iota(pos.shape, 1) & (A_HEAD_DIM - 1)
    lo = slope * (pos & 255).astype(F32)
    hi = slope * ((pos >> 8) << 8).astype(F32)
    if keys:
        val = jnp.where(l4 < 2, 1.0, jnp.where(l4 == 2, lo, jnp.where(l4 == 3, hi, 0.0)))
    else:
        val = jnp.where(l4 == 0, -lo, jnp.where(l4 == 1, -hi, jnp.where(l4 < 4, 1.0, 0.0)))
    return val.astype(BF16)


def _prompt_attn_kernel(q_ref, k_ref, v_ref, lq1_ref, lk1_ref, lq2_ref, lk2_ref, gain_ref,
                        o_ref, kc_ref, vt_ref, acc_ref, *, tq, lambda_init):
    hg = pl.program_id(1)
    t = k_ref.shape[0]
    hp = ATTN_HEADS_PER_STEP
    n_st = 2 * hp
    slopes = [jnp.exp2(jnp.full((1, 1), -8.0 / A_HEADS, F32) * (hg * hp + hh + 1).astype(F32))
              for hh in range(hp)]
    head_lanes = [slice(hh * A_QK_DIM, (hh + 1) * A_QK_DIM) for hh in range(hp)]
    pos = _iota((tq, A_QK_DIM), 0)
    low = _iota((tq, A_QK_DIM), 1) < A_HEAD_DIM

    ones_rows = (_iota((ATTN_ACC_ROWS - A_V_DIM, tq), 0) == 0).astype(BF16)
    for hh in range(hp):
        ek = _alibi_lanes(pos, slopes[hh], keys=True)
        for jc in range(t // tq):
            rows = slice(jc * tq, (jc + 1) * tq)
            kt = k_ref[rows, head_lanes[hh]]
            kc_ref[2 * hh, rows, :] = jnp.where(low, kt, ek)
            kc_ref[2 * hh + 1, rows, :] = jnp.where(low, ek, kt)
            vt_ref[hh, jc, 0:A_V_DIM, :] = v_ref[rows, head_lanes[hh]].astype(F32).T.astype(BF16)
            vt_ref[hh, jc, A_V_DIM:, :] = ones_rows

    eqs = [_alibi_lanes(pos, slopes[hh], keys=False) for hh in range(hp)]
    lam = _lambda_full(lq1_ref[...], lk1_ref[...], lq2_ref[...], lk2_ref[...], lambda_init)
    kv_pos = _iota((tq, tq), 0)
    q_pos = _iota((tq, tq), 1)

    def q_tile(i, carry):
        i = jnp.asarray(i, jnp.int32)
        q_rows = pl.ds(pl.multiple_of(i * tq, tq), tq)
        qc = []
        for hh in range(hp):
            q = q_ref[q_rows, head_lanes[hh]]
            qc += [jnp.where(low, q, eqs[hh]), jnp.where(low, eqs[hh], q)]

        def tiles(group, ms):
            s = []
            for j, masked in group:
                rows = pl.ds(pl.multiple_of(jnp.asarray(j * tq, jnp.int32), tq), tq)
                sj = [_dot_nt(kc_ref[st, rows, :], qc[st]) for st in range(n_st)]
                if masked:
                    sj = [jnp.where(kv_pos <= q_pos, sc, NEG_BIG) for sc in sj]
                s.append(sj)
            ms = list(ms)
            acc = [acc_ref[st] for st in range(n_st)]
            for (j, _), sj in zip(group, s):
                for st in range(n_st):
                    off = slopes[st // 2] * jnp.asarray((j - i) * tq, F32)
                    m_rel = ms[st] - off
                    m_new = jnp.maximum(m_rel, jnp.max(sj[st], axis=0, keepdims=True))
                    p = jnp.exp(sj[st] - m_new).astype(BF16)
                    acc[st] = jnp.exp(m_rel - m_new) * acc[st] + _dot(vt_ref[st // 2, j], p)
                    ms[st] = m_new + off
            for st in range(n_st):
                acc_ref[st] = acc[st]
            return tuple(ms)

        acc_ref[...] = jnp.zeros_like(acc_ref)
        ms = lax.fori_loop(0, i // 2,
                           lambda pr, ms: tiles([(2 * pr, False), (2 * pr + 1, False)], ms),
                           tuple(jnp.full((1, tq), NEG_BIG, F32) for _ in range(n_st)))
        lax.cond((i & 1) == 1,
                 lambda ms: tiles([(i - 1, False), (i, True)], ms),
                 lambda ms: tiles([(i, True)], ms), ms)
        for hh in range(hp):
            a0, a1 = acc_ref[2 * hh], acc_ref[2 * hh + 1]
            o_t = (a0[0:A_V_DIM] / a0[A_V_DIM:A_V_DIM + 1]
                   - lam * (a1[0:A_V_DIM] / a1[A_V_DIM:A_V_DIM + 1]))
            ms_o = jnp.mean(o_t * o_t, axis=0, keepdims=True)
            o_t = o_t * lax.rsqrt(ms_o + NORM_EPS) * gain_ref[...] * (1.0 - lambda_init)
            o_ref[q_rows, head_lanes[hh]] = o_t.T.astype(BF16)
        return carry

    lax.fori_loop(0, t // tq, q_tile, 0)


def _prompt_attn(qb, kb, vb, lq1, lk1, lq2, lk2, gain, b, t, tq, lambda_init):
    nq = t // tq
    hp = ATTN_HEADS_PER_STEP
    vec = lambda bi, h: (0, 0)
    seq_head = lambda bi, h: (bi, h)
    return pl.pallas_call(
        functools.partial(_prompt_attn_kernel, tq=tq, lambda_init=lambda_init),
        grid=(b, A_HEADS // hp),
        in_specs=[
            pl.BlockSpec((t, hp * A_QK_DIM), seq_head),
            pl.BlockSpec((t, hp * A_QK_DIM), seq_head),
            pl.BlockSpec((t, hp * A_V_DIM), seq_head),
            pl.BlockSpec((1, A_HEAD_DIM), vec),
            pl.BlockSpec((1, A_HEAD_DIM), vec),
            pl.BlockSpec((1, A_HEAD_DIM), vec),
            pl.BlockSpec((1, A_HEAD_DIM), vec),
            pl.BlockSpec((A_V_DIM, 1), vec),
        ],
        out_specs=pl.BlockSpec((t, hp * A_V_DIM), seq_head),
        out_shape=jax.ShapeDtypeStruct((b * t, A_WIDTH), BF16),
        scratch_shapes=[pltpu.VMEM((2 * hp, t, A_QK_DIM), BF16),
                        pltpu.VMEM((hp, nq, ATTN_ACC_ROWS, tq), BF16),
                        pltpu.VMEM((2 * hp, ATTN_ACC_ROWS, tq), F32)],
        compiler_params=pltpu.CompilerParams(
            dimension_semantics=("arbitrary", "arbitrary"),
            vmem_limit_bytes=V7X_VMEM_LIMIT),
        name="prompt_attn",
    )(qb, kb, vb, lq1, lk1, lq2, lk2, gain.reshape(A_V_DIM, 1))


def _decode_attn_kernel(pt_ref, q_ref, kn_ref, vn_ref, lq1_ref, lk1_ref, lq2_ref, lk2_ref,
                        gain_ref, *refs, pps, n_pages, lambda_init):
    k_refs = refs[:pps]
    v_refs = refs[pps:2 * pps]
    o_ref = refs[2 * pps]
    m_ref, l_ref, acc_ref = refs[2 * pps + 1:]
    j = pl.program_id(1)
    rows = 2 * A_HEADS
    page_rows = PAGE_SIZE * A_HEADS
    row_q = _iota((rows, A_QK_DIM), 0)
    lane_q = _iota((rows, A_QK_DIM), 1)
    qm = jnp.where((lane_q >> 6) == (row_q & 1), q_ref[0], 0.0).astype(BF16)
    slope = jnp.exp2((-8.0 / A_HEADS) * ((_iota((rows, 1), 0) >> 1) + 1).astype(F32))
    col = _iota((rows, pps * page_rows), 1)
    own_head = (col & (A_HEADS - 1)) == (_iota((rows, pps * page_rows), 0) >> 1)
    dist = (n_pages - j * pps) * PAGE_SIZE - (col >> 2)

    @pl.when(j == 0)
    def _():
        m_ref[...] = jnp.full((rows, 1), NEG_BIG, F32)
        l_ref[...] = jnp.zeros((rows, 1), F32)
        acc_ref[...] = jnp.zeros((rows, A_V_DIM), F32)

    s = jnp.concatenate([_dot_nt(qm, k_refs[u][...].astype(BF16)) for u in range(pps)], axis=1)
    s = jnp.where(own_head, s - slope * dist.astype(F32), NEG_BIG)
    m = m_ref[...]
    m_new = jnp.maximum(m, jnp.max(s, axis=-1, keepdims=True))
    alpha = jnp.exp(m - m_new)
    p = jnp.exp(s - m_new)
    pb = p.astype(BF16)
    pv = _dot(pb[:, 0:page_rows], v_refs[0][...].astype(BF16))
    for u in range(1, pps):
        pv = pv + _dot(pb[:, u * page_rows:(u + 1) * page_rows], v_refs[u][...].astype(BF16))
    m = m_new
    l = alpha * l_ref[...] + jnp.sum(p, axis=-1, keepdims=True)
    acc = alpha * acc_ref[...] + pv
    m_ref[...], l_ref[...], acc_ref[...] = m, l, acc

    @pl.when(j == pl.num_programs(1) - 1)
    def _():
        s_new = jnp.sum(qm.astype(F32) * kn_ref[0], axis=-1, keepdims=True)
        m_new = jnp.maximum(m, s_new)
        alpha = jnp.exp(m - m_new)
        p_new = jnp.exp(s_new - m_new)
        l_fin = alpha * l + p_new
        acc_fin = alpha * acc + p_new.astype(BF16).astype(F32) * vn_ref[0]
        lam = _lambda_full(lq1_ref[...], lk1_ref[...], lq2_ref[...], lk2_ref[...], lambda_init)
        comp = _iota((rows, 1), 0) & 1
        x = acc_fin * (jnp.where(comp == 0, 1.0, -lam) / l_fin)
        o = x + pltpu.roll(x, shift=rows - 1, axis=0)
        o_ref[0] = _rms_rows(o, gain_ref[...]) * (1.0 - lambda_init)


def _decode_attn(page_table, qb, kb, vb, cache_k, cache_v, layer, lq1, lk1, lq2, lk2, gain,
                 pps, lambda_init):
    db, n_pages = page_table.shape
    n_pool = cache_k.shape[1]
    page_rows = PAGE_SIZE * A_HEADS
    ck = cache_k.reshape(-1, A_QK_DIM)
    cv = cache_v.reshape(-1, A_V_DIM)
    pt = page_table.reshape(db * n_pages)
    rows = 2 * A_HEADS
    tok3 = lambda b, j, pt_ref: (b, 0, 0)
    vec = lambda b, j, pt_ref: (0, 0)

    def page_spec(u):
        return pl.BlockSpec(
            (page_rows, A_QK_DIM),
            lambda b, j, pt_ref: (layer * n_pool + pt_ref[b * n_pages + j * pps + u], 0))

    def per_row(z):
        return jnp.repeat(z.astype(F32).reshape(db, A_HEADS, A_QK_DIM), 2, axis=1)

    grid_spec = pltpu.PrefetchScalarGridSpec(
        num_scalar_prefetch=1,
        grid=(db, n_pages // pps),
        in_specs=[
            pl.BlockSpec((1, rows, A_QK_DIM), tok3),
            pl.BlockSpec((1, rows, A_QK_DIM), tok3),
            pl.BlockSpec((1, rows, A_V_DIM), tok3),
            pl.BlockSpec((1, A_HEAD_DIM), vec),
            pl.BlockSpec((1, A_HEAD_DIM), vec),
            pl.BlockSpec((1, A_HEAD_DIM), vec),
            pl.BlockSpec((1, A_HEAD_DIM), vec),
            pl.BlockSpec((1, A_V_DIM), vec),
        ] + [page_spec(u) for u in range(pps)] + [page_spec(u) for u in range(pps)],
        out_specs=pl.BlockSpec((1, rows, A_V_DIM), tok3),
        scratch_shapes=[pltpu.VMEM((rows, 1), F32), pltpu.VMEM((rows, 1), F32),
                        pltpu.VMEM((rows, A_V_DIM), F32)],
    )
    out = pl.pallas_call(
        functools.partial(_decode_attn_kernel, pps=pps, n_pages=n_pages, lambda_init=lambda_init),
        grid_spec=grid_spec,
        out_shape=jax.ShapeDtypeStruct((db, rows, A_V_DIM), F32),
        compiler_params=pltpu.CompilerParams(
            dimension_semantics=("arbitrary", "arbitrary"), vmem_limit_bytes=V7X_VMEM_LIMIT),
        name="decode_attn",
    )(pt, per_row(qb), per_row(kb), per_row(vb),
      lq1, lk1, lq2, lk2, gain, *([ck] * pps), *([cv] * pps))
    return out[:, 0::2, :].reshape(db, A_WIDTH)


def _rwkv_prep(x, prev, mu, w0, w2e, a0, a2e, g2, k_k, k_a, r_k, ones):
    m = x + (prev - x) * mu
    r = m[:, 0:R_WIDTH]
    k = m[:, R_WIDTH:2 * R_WIDTH]
    v = m[:, 2 * R_WIDTH:3 * R_WIDTH]
    xwa = m[:, 3 * R_WIDTH:3 * R_WIDTH + R_LORA_W + R_LORA_A]
    xg = m[:, 3 * R_WIDTH + R_LORA_W + R_LORA_A:]
    z = w0 + _dot(jnp.tanh(xwa).astype(BF16), w2e)
    lw = (-math.exp(-0.5)) * _sigmoid(z)
    a = _sigmoid(a0 + _dot(xwa.astype(BF16), a2e))
    g = _dot(_sigmoid(xg).astype(BF16), g2)
    kk = k * k_k
    kk = kk * lax.rsqrt(jnp.maximum(_group_sum(kk * kk, ones), 1e-24))
    k = k * (1.0 + (a - 1.0) * k_a)
    bonus = _group_sum(r * k * r_k, ones) * v
    return r, lw, k, v, kk, a, g, bonus


def _rwkv_finish(y, bonus, g, gn_w, gn_b, ones):
    mu = _group_sum(y, ones) * (1.0 / R_HEAD)
    d = y - mu
    var = _group_sum(d * d, ones) * (1.0 / R_HEAD)
    yn = d * lax.rsqrt(var + GN_EPS) * gn_w + gn_b
    return ((yn + bonus) * g).astype(BF16)


def _lane_halves(x, width):
    lane = _iota(x.shape, 1)
    zero = jnp.zeros_like(x)
    return jnp.concatenate([jnp.where(lane < width, x, zero), jnp.where(lane >= width, x, zero)],
                           axis=0)


def _lane_blocks(x, width):
    block = _iota(x.shape, 1) >> int(math.log2(width))
    zero = jnp.zeros_like(x)
    return jnp.concatenate([jnp.where(block == b, x, zero) for b in range(x.shape[1] // width)],
                           axis=0)


def _rwkv_chunk_kernel(x_ref, *refs, chunk, nb):
    st_ref, prev_ref = refs[-3], refs[-1]

    @pl.when(pl.program_id(1) == 0)
    def _():
        prev_ref[...] = jnp.zeros_like(prev_ref)
        st_ref[...] = jnp.zeros_like(st_ref)

    def body(k, carry):
        _rwkv_one_chunk(jnp.asarray(k, jnp.int32), x_ref, *refs, chunk=chunk, nb=nb)
        return carry

    lax.fori_loop(0, x_ref.shape[1] // chunk, body, 0)


def _rwkv_one_chunk(k, x_ref, mu_ref, w0_ref, w2e_ref, a0_ref, a2e_ref, g2_ref, kk_ref, ka_ref,
                    rk_ref, gnw_ref, gnb_ref, ones_ref,
                    y_ref, st_ref, sh_ref, prev_ref, *, chunk, nb):
    c = chunk
    chunk_rows = pl.ds(pl.multiple_of(k * c, c), c)
    ones = ones_ref[...]
    rows = _iota((c, R_SHIFT_W), 0)
    tri = (_iota((c, c), 0) >= _iota((c, c), 1)).astype(BF16)
    sls = [slice(p * LANES, (p + 1) * LANES) for p in range(R_PAIRS)]
    ah_p, rh_p, v_p, bt_p, kt_p, bb_p, kb_p, pe_p, st_idx, fin = ([] for _ in range(10))
    for bi in range(nb):
        x = x_ref[bi, chunk_rows, :]
        prev = jnp.where(rows == 0, prev_ref[bi], pltpu.roll(x, shift=1, axis=0))
        prev_ref[bi] = x[c - 1:c, :]
        sh_ref[bi] = x[c - 1:c, :]
        r, lw, k, v, kk, a, g, bonus = _rwkv_prep(
            x, prev, mu_ref[...], w0_ref[...], w2e_ref[...], a0_ref[...], a2e_ref[...],
            g2_ref[...], kk_ref[...], ka_ref[...], rk_ref[...], ones)
        fin.append((bonus, g))
        l_hi = lw.astype(BF16)
        l_r1 = lw - l_hi.astype(F32)
        l_mid = l_r1.astype(BF16)
        l_lo = (l_r1 - l_mid.astype(F32)).astype(BF16)
        cum = _dot(tri, l_hi) + _dot(tri, l_mid) + _dot(tri, l_lo)
        cum_end = cum[c - 1:c, :]
        e_neg = jnp.exp(-cum)
        e_end = jnp.exp(cum_end - cum)
        kka = kk * a
        ah = -kk * jnp.exp(cum - lw)
        rh = r * jnp.exp(cum)
        bt = kka * e_neg
        kt = k * e_neg
        bb = kka * e_end
        kb = k * e_end
        p_end = jnp.exp(cum_end)
        for p, sl in enumerate(sls):
            for lst, z in ((ah_p, ah), (rh_p, rh), (v_p, v), (bt_p, bt), (kt_p, kt),
                           (bb_p, bb), (kb_p, kb), (pe_p, p_end)):
                lst.append(z[:, sl])
            st_idx.append((bi, p))

    t_idx = _iota((c, 2 * c), 0)
    s_idx = _iota((c, 2 * c), 1) & (c - 1)
    strict = t_idx > s_idx
    incl = t_idx >= s_idx
    eye_pack = (t_idx == s_idx).astype(F32)
    bd_mask = (_iota((LANES, LANES), 0) >> 6) == (_iota((LANES, LANES), 1) >> 6)
    eye_l = _iota((LANES, LANES), 0) == _iota((LANES, LANES), 1)
    zeros_c = jnp.zeros((c, LANES), F32)

    units = range(len(st_idx))
    v_blk = [_lane_halves(z, R_HEAD) for z in v_p]
    a_all = [_mm(jnp.concatenate([ah_p[u], rh_p[u]], axis=0),
                 jnp.concatenate([_lane_halves(bt_p[u], R_HEAD), _lane_halves(kt_p[u], R_HEAD)],
                                 axis=0), _dot_nt) for u in units]
    a_ab = [jnp.where(strict, z[0:c, 0:2 * c], 0.0) for z in a_all]
    a_ak = [jnp.where(strict, z[0:c, 2 * c:4 * c], 0.0) for z in a_all]
    a_rb = [jnp.where(incl, z[c:2 * c, 0:2 * c], 0.0) for z in a_all]
    a_rk = [jnp.where(incl, z[c:2 * c, 2 * c:4 * c], 0.0) for z in a_all]
    w1 = [_mm(a_ak[u], v_blk[u]) for u in units]
    quads = range(len(st_idx) // 2)
    a4 = [jnp.concatenate([a_ab[2 * q], a_ab[2 * q + 1]], axis=1) for q in quads]
    eye4 = jnp.concatenate([eye_pack, eye_pack], axis=1)
    inv4 = [eye4 + z for z in a4]
    bd = [_lane_blocks(z, c) for z in a4]
    pw = [_mm(a4[q], bd[q]) for q in quads]
    n_sq = int(math.log2(c)) - 1
    for step in range(n_sq):
        bd = [_lane_blocks(z, c) for z in pw]
        if step < n_sq - 1:
            both = [_mm(jnp.concatenate([pw[q], inv4[q]], axis=0), bd[q]) for q in quads]
            pw = [z[0:c] for z in both]
            inv4 = [inv4[q] + both[q][c:2 * c] for q in quads]
        else:
            inv4 = [inv4[q] + _mm(inv4[q], bd[q]) for q in quads]
    inv = [inv4[u // 2][:, (u % 2) * 2 * c:(u % 2 + 1) * 2 * c] for u in units]
    t4 = [_mm(inv[u], jnp.concatenate([_lane_halves(w1[u], R_HEAD),
                                       _lane_halves(ah_p[u], R_HEAD)], axis=1)) for u in units]
    wu = [z[:, 0:LANES] for z in t4]
    ah2 = [z[:, LANES:] for z in t4]
    zeros_blk = jnp.zeros((2 * c, LANES), F32)
    t5 = [_mm(jnp.concatenate([a_rb[u], a_rk[u]], axis=1),
              jnp.concatenate([
                  jnp.concatenate([_lane_halves(wu[u], R_HEAD), _lane_halves(ah2[u], R_HEAD)],
                                  axis=1),
                  jnp.concatenate([v_blk[u], zeros_blk], axis=1)], axis=0)) for u in units]
    t6 = [_mm(jnp.concatenate([bb_p[u], kb_p[u]], axis=0),
              jnp.concatenate([jnp.concatenate([wu[u], ah2[u]], axis=1),
                               jnp.concatenate([v_p[u], zeros_c], axis=1)], axis=0), _dot_tn)
          for u in units]
    st = [st_ref[bi, p] for bi, p in st_idx]
    ys = [_mm(rh_p[u] + t5[u][:, LANES:], st[u]) + t5[u][:, 0:LANES] for u in units]
    for u in units:
        h_new = jnp.where(bd_mask, t6[u][:, 0:LANES], 0.0)
        g_x = jnp.where(bd_mask, t6[u][:, LANES:], 0.0)
        p_col = jnp.sum(jnp.where(eye_l, pe_p[u], 0.0), axis=1, keepdims=True)
        st_ref[st_idx[u]] = p_col * st[u] + _mm(g_x, st[u]) + h_new

    for bi in range(nb):
        y = jnp.concatenate(ys[bi * R_PAIRS:(bi + 1) * R_PAIRS], axis=1)
        bonus, g = fin[bi]
        y_ref[bi, chunk_rows, :] = _rwkv_finish(y, bonus, g, gnw_ref[...], gnb_ref[...], ones)


def _rwkv_prompt(rc3, params, chunk, nb):
    b, t, _ = rc3.shape
    rows = _pick_tile(t, RWKV_CHUNKS_PER_STEP * chunk)
    const = lambda bi, j: (0, 0)
    specs = [pl.BlockSpec(p.shape, const) for p in params]
    return pl.pallas_call(
        functools.partial(_rwkv_chunk_kernel, chunk=chunk, nb=nb),
        grid=(b // nb, t // rows),
        in_specs=[pl.BlockSpec((nb, rows, R_SHIFT_W), lambda bi, j: (bi, j, 0))] + specs,
        out_specs=[
            pl.BlockSpec((nb, rows, R_WIDTH), lambda bi, j: (bi, j, 0)),
            pl.BlockSpec((nb, R_PAIRS, LANES, LANES), lambda bi, j: (bi, 0, 0, 0)),
            pl.BlockSpec((nb, 1, R_SHIFT_W), lambda bi, j: (bi, 0, 0)),
        ],
        out_shape=(
            jax.ShapeDtypeStruct((b, t, R_WIDTH), BF16),
            jax.ShapeDtypeStruct((b, R_PAIRS, LANES, LANES), F32),
            jax.ShapeDtypeStruct((b, 1, R_SHIFT_W), F32),
        ),
        scratch_shapes=[pltpu.VMEM((nb, 1, R_SHIFT_W), F32)],
        compiler_params=pltpu.CompilerParams(
            dimension_semantics=("arbitrary", "arbitrary"), vmem_limit_bytes=V7X_VMEM_LIMIT),
        name="rwkv_chunk",
    )(rc3, *params)


def _rwkv_step_kernel(x_ref, prev_ref, st_in_ref, mu_ref, w0_ref, w2e_ref, a0_ref, a2e_ref,
                      g2_ref, kk_ref, ka_ref, rk_ref, gnw_ref, gnb_ref, ones_ref,
                      y_ref, st_ref, yrow_ref, *, rows):
    ones = ones_ref[...]
    r, lw, k, v, kk, a, g, bonus = _rwkv_prep(
        x_ref[...], prev_ref[...], mu_ref[...], w0_ref[...], w2e_ref[...], a0_ref[...],
        a2e_ref[...], g2_ref[...], kk_ref[...], ka_ref[...], rk_ref[...], ones)
    vecs = (jnp.exp(lw), kk * a, k, v, kk, r)
    row8 = _iota((8, R_HEAD), 0)

    def rows8(*row_vecs):
        out = jnp.zeros((8, R_HEAD), F32)
        for n, vec in enumerate(row_vecs):
            out = jnp.where(row8 == n, vec, out)
        return out

    def hi_lo(x):
        hi = x.astype(BF16).astype(F32)
        return hi, x - hi

    def vec_mat(vec_row, s):
        return _mm(jnp.broadcast_to(vec_row, (8, R_HEAD)), s, _dot_nt)[0:1]

    group = math.gcd(rows, 4)
    for g0 in range(0, rows, group):
        units = [(bi, h) for bi in range(g0, g0 + group) for h in range(R_HEADS)]
        w_r, b_r, k_r, v_r, kk_r, r_r = (
            [z[bi:bi + 1, h * R_HEAD:(h + 1) * R_HEAD] for bi, h in units] for z in vecs)
        ids = range(len(units))
        s_old = [st_in_ref[bi, h] for bi, h in units]
        sa = [-vec_mat(kk_r[u], s_old[u]) for u in ids]
        upd = []
        for u in ids:
            (sa_h, sa_l), (b_h, b_l) = hi_lo(sa[u]), hi_lo(b_r[u])
            (v_h, v_l), (k_h, k_l) = hi_lo(v_r[u]), hi_lo(k_r[u])
            upd.append(_mm(rows8(sa_h, sa_h, sa_l, v_h, v_h, v_l),
                           rows8(b_h, b_l, b_h, k_h, k_l, k_h), _dot_tn))
        s_new = [s_old[u] * w_r[u] + upd[u] for u in ids]
        for u, (bi, h) in enumerate(units):
            st_ref[bi, h] = s_new[u]
        y_r = [vec_mat(r_r[u], s_new[u]) for u in ids]
        for n, bi in enumerate(range(g0, g0 + group)):
            yrow_ref[bi:bi + 1, :] = jnp.concatenate(y_r[n * R_HEADS:(n + 1) * R_HEADS], axis=1)
    y_ref[...] = _rwkv_finish(yrow_ref[...], bonus, g, gnw_ref[...], gnb_ref[...], ones)


def _rwkv_step(rc, shift_prev, state, params, rows):
    db = rc.shape[0]
    const = lambda i: (0, 0)
    specs = [pl.BlockSpec(p.shape, const) for p in params]
    state_spec = pl.BlockSpec((rows, R_HEADS, R_HEAD, R_HEAD), lambda i: (i, 0, 0, 0))
    return pl.pallas_call(
        functools.partial(_rwkv_step_kernel, rows=rows),
        grid=(db // rows,),
        in_specs=[
            pl.BlockSpec((rows, R_SHIFT_W), lambda i: (i, 0)),
            pl.BlockSpec((rows, R_SHIFT_W), lambda i: (i, 0)),
            state_spec,
        ] + specs,
        out_specs=[pl.BlockSpec((rows, R_WIDTH), lambda i: (i, 0)), state_spec],
        out_shape=(
            jax.ShapeDtypeStruct((db, R_WIDTH), BF16),
            jax.ShapeDtypeStruct((db, R_HEADS, R_HEAD, R_HEAD), F32),
        ),
        scratch_shapes=[pltpu.VMEM((rows, R_WIDTH), F32)],
        compiler_params=pltpu.CompilerParams(
            dimension_semantics=("arbitrary",), vmem_limit_bytes=V7X_VMEM_LIMIT),
        name="rwkv_step",
    )(rc, shift_prev, state, *params)


def _pairs_to_state(st):
    b = st.shape[0]
    blocks = st.reshape(b, R_PAIRS, 2, R_HEAD, 2, R_HEAD)
    diag = jnp.stack([blocks[:, :, 0, :, 0, :], blocks[:, :, 1, :, 1, :]], axis=2)
    return jnp.swapaxes(diag.reshape(b, R_HEADS, R_HEAD, R_HEAD), -1, -2)


def _merge_ffn_kernel(x_ref, o_ref, rw_ref, g_ref, wpa_ref, wpb_ref, wout_ref, nf_ref,
                      wg_ref, wu_ref, wd_ref, y_ref, *, d_ff):
    ya = _dot(o_ref[...].astype(BF16), wpa_ref[...])
    yb = _dot(rw_ref[...], wpb_ref[...])
    merged = (g_ref[:, 0:D_MODEL] * ya + g_ref[:, D_MODEL:] * yb).astype(BF16)
    x1 = x_ref[...] + _dot(merged, wout_ref[...])
    hf = _rms_rows(x1, nf_ref[...]).astype(BF16)
    acc = x1
    step = 512
    for c in range(0, d_ff, step):
        n = min(step, d_ff - c)
        gate = _dot(hf, wg_ref[:, c:c + n])
        up = _dot(hf, wu_ref[:, c:c + n])
        act = (gate * _sigmoid(gate) * up).astype(BF16)
        acc = acc + _dot(act, wd_ref[c:c + n, :])
    y_ref[...] = acc


def _merge_ffn(x2, o, rw, gates, wpa, wpb, wout, nf, wg, wu, wd, tm):
    m = x2.shape[0]
    d_ff = wg.shape[1]
    row = lambda i: (i, 0)
    const = lambda i: (0, 0)
    return pl.pallas_call(
        functools.partial(_merge_ffn_kernel, d_ff=d_ff),
        grid=(m // tm,),
        in_specs=[
            pl.BlockSpec((tm, D_MODEL), row),
            pl.BlockSpec((tm, A_WIDTH), row),
            pl.BlockSpec((tm, R_WIDTH), row),
            pl.BlockSpec((tm, 2 * D_MODEL), row),
            pl.BlockSpec(wpa.shape, const),
            pl.BlockSpec(wpb.shape, const),
            pl.BlockSpec(wout.shape, const),
            pl.BlockSpec((1, D_MODEL), const),
            pl.BlockSpec(wg.shape, const),
            pl.BlockSpec(wu.shape, const),
            pl.BlockSpec(wd.shape, const),
        ],
        out_specs=pl.BlockSpec((tm, D_MODEL), row),
        out_shape=jax.ShapeDtypeStruct((m, D_MODEL), F32),
        compiler_params=pltpu.CompilerParams(
            dimension_semantics=("arbitrary",), vmem_limit_bytes=V7X_VMEM_LIMIT),
        name="merge_ffn",
    )(x2, o, rw, gates, wpa, wpb, wout, nf, wg, wu, wd)


def _row(v):
    return v.reshape(1, -1).astype(F32)


def _pick_tile(m, target):
    t = min(m, target)
    while m % t:
        t //= 2
    return t


def kernel(x_prompt, x_sample, cache_k, cache_v, page_table, state_wkv, state_shift, norm_mix, w_in, q_gain, k_gain, lambda_q1, lambda_k1, lambda_q2, lambda_k2, attn_out_gain, w_pa, shift_mu, w0, w2, a0, a2, g2, k_k, k_a, r_k, gn_w, gn_b, w_pb, w_out, norm_ffn, w_gate, w_up, w_down):
    b, t, _ = x_prompt.shape
    db, ds, _ = x_sample.shape
    assert ds == 1, "sample group carries one new token per sequence"
    depth = w_in.shape[0]
    n_pages = page_table.shape[1]
    xp = x_prompt.reshape(b * t, D_MODEL)
    xs = x_sample.reshape(db, D_MODEL)
    head_ones = (jnp.arange(A_WIDTH)[:, None] // A_HEAD_DIM
                 == jnp.arange(A_WIDTH)[None, :] // A_HEAD_DIM).astype(BF16)
    outs = [[] for _ in range(8)]
    tm_p = _pick_tile(b * t, 512)
    tm_s = _pick_tile(db, 512)
    tq = _pick_tile(t, 512)
    chunk = _pick_tile(t, RWKV_CHUNK)
    chunk_nb = _pick_tile(b, 4)
    pps = _pick_tile(n_pages, 32)
    step_rows = _pick_tile(db, 16)
    for l in range(depth):
        lambda_init = 0.8 - 0.6 * math.exp(-0.3 * l)
        w_in_bf = w_in[l].astype(BF16)
        qg = _row(jnp.tile(q_gain[l], A_WIDTH // A_HEAD_DIM))
        kg = _row(jnp.tile(k_gain[l], A_WIDTH // A_HEAD_DIM))
        nm = _row(norm_mix[l])
        lam_vecs = (_row(lambda_q1[l]), _row(lambda_k1[l]), _row(lambda_q2[l]), _row(lambda_k2[l]))
        og = _row(attn_out_gain[l])
        zeros_lora = jnp.zeros((R_LORA_W, R_WIDTH), F32)
        rwkv_params = (
            _row(shift_mu[l]), _row(w0[l]),
            jnp.concatenate([w2[l], zeros_lora], axis=0).astype(BF16),
            _row(a0[l]),
            jnp.concatenate([zeros_lora, a2[l]], axis=0).astype(BF16),
            g2[l].astype(BF16), _row(k_k[l]), _row(k_a[l]), _row(r_k[l]),
            _row(gn_w[l]), _row(gn_b[l]), head_ones)
        ffn_w = (w_pa[l].astype(BF16), w_pb[l].astype(BF16), w_out[l].astype(BF16),
                 _row(norm_ffn[l]), w_gate[l].astype(BF16), w_up[l].astype(BF16),
                 w_down[l].astype(BF16))

        qb, k32, kb, v32, vb, rc, gates = _proj(xp, nm, w_in_bf, qg, kg, head_ones, tm_p)
        o = _prompt_attn(qb, kb, vb, *lam_vecs, og, b, t, tq, lambda_init)
        rw, st_p, sh_p = _rwkv_prompt(rc.reshape(b, t, R_SHIFT_W), rwkv_params, chunk, chunk_nb)
        xp = _merge_ffn(xp, o, rw.reshape(b * t, R_WIDTH), gates, *ffn_w, tm_p)
        outs[0].append(k32.reshape(b, t, A_HEADS, A_QK_DIM))
        outs[1].append(v32.reshape(b, t, A_HEADS, A_V_DIM))
        outs[2].append(_pairs_to_state(st_p))
        outs[3].append(sh_p.reshape(b, R_SHIFT_W))

        qb, k32, kb, v32, vb, rc, gates = _proj(xs, nm, w_in_bf, qg, kg, head_ones, tm_s)
        o = _decode_attn(page_table, qb, kb, vb, cache_k, cache_v, l, *lam_vecs, og,
                         pps, lambda_init)
        rw, st_s = _rwkv_step(rc, state_shift[l], state_wkv[l], rwkv_params,
                              step_rows)
        xs = _merge_ffn(xs, o, rw, gates, *ffn_w, tm_s)
        outs[4].append(k32.reshape(db, 1, A_HEADS, A_QK_DIM))
        outs[5].append(v32.reshape(db, 1, A_HEADS, A_V_DIM))
        outs[6].append(st_s)
        outs[7].append(rc)
    return (xp.reshape(b, t, D_MODEL), xs.reshape(db, 1, D_MODEL),
            *(jnp.stack(o) for o in outs))
```

```python
import functools
import math

import jax
import jax.numpy as jnp
from jax import lax
from jax.experimental import pallas as pl
from jax.experimental.pallas import tpu as pltpu

F32 = jnp.float32
BF16 = jnp.bfloat16

D_MODEL = 1024
PAGE_SIZE = 128
A_HEADS = 4
A_HEAD_DIM = 64
A_QK_DIM = 2 * A_HEAD_DIM
A_V_DIM = 2 * A_HEAD_DIM
A_WIDTH = A_HEADS * A_V_DIM
R_HEAD = 64
R_HEADS = 8
R_WIDTH = R_HEADS * R_HEAD
R_LORA_W = 64
R_LORA_A = 64
R_LORA_G = 128
R_SHIFT_W = 3 * R_WIDTH + R_LORA_W + R_LORA_A + R_LORA_G
OFF_Q = 0
OFF_K = OFF_Q + A_HEADS * A_QK_DIM
OFF_V = OFF_K + A_HEADS * A_QK_DIM
OFF_R = OFF_V + A_WIDTH
OFF_G = OFF_R + R_SHIFT_W
IN_WIDTH = OFF_G + 2 * D_MODEL
NORM_EPS = 1e-6
GN_EPS = 64e-5
NEG_BIG = -1e30

V7X_VMEM_LIMIT = 56 * 1024 * 1024
LANES = 128
SUBLANES = 8
R_PAIRS = R_WIDTH // LANES
RWKV_CHUNK = 64
RWKV_CHUNKS_PER_STEP = 4


def _dot(a, b):
    return jnp.dot(a, b, preferred_element_type=F32)


def _dot_nt(a, b):
    return lax.dot_general(a, b, (((1,), (1,)), ((), ())), preferred_element_type=F32)


def _dot_tn(a, b):
    return lax.dot_general(a, b, (((0,), (0,)), ((), ())), preferred_element_type=F32)


def _mm(a, b, dot=_dot):
    return dot(a.astype(BF16), b.astype(BF16))


def _sigmoid(x):
    return 1.0 / (1.0 + jnp.exp(-x))


def _iota(shape, dim):
    return lax.broadcasted_iota(jnp.int32, shape, dim)


def _group_sum(z, ones_bf):
    return _dot(z.astype(BF16), ones_bf)


def _rms_rows(x, gain_row):
    ms = jnp.mean(x * x, axis=-1, keepdims=True)
    return x * lax.rsqrt(ms + NORM_EPS) * gain_row


def _proj_kernel(x_ref, nm_ref, w_ref, qg_ref, kg_ref, ones_ref,
                 q_ref, k_ref, kb_ref, v_ref, vb_ref, r_ref, g_ref):
    h = _rms_rows(x_ref[...], nm_ref[...]).astype(BF16)
    ones = ones_ref[...]

    def head_norm(z, gain_row):
        ss = _group_sum(z * z, ones)
        return z * lax.rsqrt(ss * (1.0 / A_HEAD_DIM) + NORM_EPS) * gain_row

    def w(lo, hi):
        return w_ref[:, lo:hi].astype(BF16)

    q = head_norm(_dot(h, w(OFF_Q, OFF_K)), qg_ref[...])
    q_ref[...] = (q * (A_HEAD_DIM ** -0.5)).astype(BF16)
    tm = x_ref.shape[0]
    k = head_norm(_dot(h, w(OFF_K, OFF_V)), kg_ref[...])
    kb_ref[...] = k.astype(BF16)
    v = _dot(h, w(OFF_V, OFF_R))
    vb_ref[...] = v.astype(BF16)
    for hd in range(A_HEADS):
        sl = slice(hd * A_QK_DIM, (hd + 1) * A_QK_DIM)
        k_ref[pl.ds(hd, tm, stride=A_HEADS), :] = k[:, sl]
        v_ref[pl.ds(hd, tm, stride=A_HEADS), :] = v[:, sl]
    step = 512
    for c in range(0, R_SHIFT_W, step):
        n = min(step, R_SHIFT_W - c)
        r_ref[:, c:c + n] = _dot(h, w(OFF_R + c, OFF_R + c + n))
    for c in range(0, 2 * D_MODEL, step):
        g_ref[:, c:c + step] = _sigmoid(_dot(h, w(OFF_G + c, OFF_G + c + step)))


def _proj(x2, nm, w_in_l, qg, kg, ones_bf, tm):
    m = x2.shape[0]
    row = lambda i: (i, 0)
    const = lambda i: (0, 0)
    out_shape = (
        jax.ShapeDtypeStruct((m, A_WIDTH), BF16),
        jax.ShapeDtypeStruct((m * A_HEADS, A_QK_DIM), F32),
        jax.ShapeDtypeStruct((m, A_WIDTH), BF16),
        jax.ShapeDtypeStruct((m * A_HEADS, A_V_DIM), F32),
        jax.ShapeDtypeStruct((m, A_WIDTH), BF16),
        jax.ShapeDtypeStruct((m, R_SHIFT_W), F32),
        jax.ShapeDtypeStruct((m, 2 * D_MODEL), F32),
    )
    return pl.pallas_call(
        _proj_kernel,
        grid=(m // tm,),
        in_specs=[
            pl.BlockSpec((tm, D_MODEL), row),
            pl.BlockSpec((1, D_MODEL), const),
            pl.BlockSpec((D_MODEL, IN_WIDTH), const),
            pl.BlockSpec((1, A_WIDTH), const),
            pl.BlockSpec((1, A_WIDTH), const),
            pl.BlockSpec((A_WIDTH, A_WIDTH), const),
        ],
        out_specs=[
            pl.BlockSpec((tm, A_WIDTH), row),
            pl.BlockSpec((tm * A_HEADS, A_QK_DIM), row),
            pl.BlockSpec((tm, A_WIDTH), row),
            pl.BlockSpec((tm * A_HEADS, A_V_DIM), row),
            pl.BlockSpec((tm, A_WIDTH), row),
            pl.BlockSpec((tm, R_SHIFT_W), row),
            pl.BlockSpec((tm, 2 * D_MODEL), row),
        ],
        out_shape=out_shape,
        compiler_params=pltpu.CompilerParams(
            dimension_semantics=("arbitrary",), vmem_limit_bytes=V7X_VMEM_LIMIT),
        name="proj",
    )(x2, nm, w_in_l, qg, kg, ones_bf)


def _lambda_full(lq1, lk1, lq2, lk2, lambda_init):
    s1 = jnp.sum(lq1 * lk1, axis=-1, keepdims=True)
    s2 = jnp.sum(lq2 * lk2, axis=-1, keepdims=True)
    return jnp.exp(s1) - jnp.exp(s2) + lambda_init


ATTN_ACC_ROWS = A_V_DIM + 16
ATTN_HEADS_PER_STEP = 1


def _alibi_lanes(pos, slope, keys):
    l4 = _iota(pos.shape, 1) & (A_HEAD_DIM - 1)
    lo = slope * (pos & 255).astype(F32)
    hi = slope * ((pos >> 8) << 8).astype(F32)
    if keys:
        val = jnp.where(l4 < 2, 1.0, jnp.where(l4 == 2, lo, jnp.where(l4 == 3, hi, 0.0)))
    else:
        val = jnp.where(l4 == 0, -lo, jnp.where(l4 == 1, -hi, jnp.where(l4 < 4, 1.0, 0.0)))
    return val.astype(BF16)


def _prompt_attn_kernel(q_ref, k_ref, v_ref, lq1_ref, lk1_ref, lq2_ref, lk2_ref, gain_ref,
                        o_ref, kc_ref, vt_ref, acc_ref, *, tq, lambda_init):
    hg = pl.program_id(1)
    t = k_ref.shape[0]
    hp = ATTN_HEADS_PER_STEP
    n_st = 2 * hp
    slopes = [jnp.exp2(jnp.full((1, 1), -8.0 / A_HEADS, F32) * (hg * hp + hh + 1).astype(F32))
              for hh in range(hp)]
    head_lanes = [slice(hh * A_QK_DIM, (hh + 1) * A_QK_DIM) for hh in range(hp)]
    pos = _iota((tq, A_QK_DIM), 0)
    low = _iota((tq, A_QK_DIM), 1) < A_HEAD_DIM

    ones_rows = (_iota((ATTN_ACC_ROWS - A_V_DIM, tq), 0) == 0).astype(BF16)
    for hh in range(hp):
        ek = _alibi_lanes(pos, slopes[hh], keys=True)
        for jc in range(t // tq):
            rows = slice(jc * tq, (jc + 1) * tq)
            kt = k_ref[rows, head_lanes[hh]]
            kc_ref[2 * hh, rows, :] = jnp.where(low, kt, ek)
            kc_ref[2 * hh + 1, rows, :] = jnp.where(low, ek, kt)
            vt_ref[hh, jc, 0:A_V_DIM, :] = v_ref[rows, head_lanes[hh]].astype(F32).T.astype(BF16)
            vt_ref[hh, jc, A_V_DIM:, :] = ones_rows

    eqs = [_alibi_lanes(pos, slopes[hh], keys=False) for hh in range(hp)]
    lam = _lambda_full(lq1_ref[...], lk1_ref[...], lq2_ref[...], lk2_ref[...], lambda_init)
    kv_pos = _iota((tq, tq), 0)
    q_pos = _iota((tq, tq), 1)

    def q_tile(i, carry):
        i = jnp.asarray(i, jnp.int32)
        q_rows = pl.ds(pl.multiple_of(i * tq, tq), tq)
        qc = []
        for hh in range(hp):
            q = q_ref[q_rows, head_lanes[hh]]
            qc += [jnp.where(low, q, eqs[hh]), jnp.where(low, eqs[hh], q)]

        def tiles(group, ms):
            s = []
            for j, masked in group:
                rows = pl.ds(pl.multiple_of(jnp.asarray(j * tq, jnp.int32), tq), tq)
                sj = [_dot_nt(kc_ref[st, rows, :], qc[st]) for st in range(n_st)]
                if masked:
                    sj = [jnp.where(kv_pos <= q_pos, sc, NEG_BIG) for sc in sj]
                s.append(sj)
            ms = list(ms)
            acc = [acc_ref[st] for st in range(n_st)]
            for (j, _), sj in zip(group, s):
                for st in range(n_st):
                    off = slopes[st // 2] * jnp.asarray((j - i) * tq, F32)
                    m_rel = ms[st] - off
                    m_new = jnp.maximum(m_rel, jnp.max(sj[st], axis=0, keepdims=True))
                    p = jnp.exp(sj[st] - m_new).astype(BF16)
                    acc[st] = jnp.exp(m_rel - m_new) * acc[st] + _dot(vt_ref[st // 2, j], p)
                    ms[st] = m_new + off
            for st in range(n_st):
                acc_ref[st] = acc[st]
            return tuple(ms)

        acc_ref[...] = jnp.zeros_like(acc_ref)
        ms = lax.fori_loop(0, i // 2,
                           lambda pr, ms: tiles([(2 * pr, False), (2 * pr + 1, False)], ms),
                           tuple(jnp.full((1, tq), NEG_BIG, F32) for _ in range(n_st)))
        lax.cond((i & 1) == 1,
                 lambda ms: tiles([(i - 1, False), (i, True)], ms),
                 lambda ms: tiles([(i, True)], ms), ms)
        for hh in range(hp):
            a0, a1 = acc_ref[2 * hh], acc_ref[2 * hh + 1]
            o_t = (a0[0:A_V_DIM] / a0[A_V_DIM:A_V_DIM + 1]
                   - lam * (a1[0:A_V_DIM] / a1[A_V_DIM:A_V_DIM + 1]))
            ms_o = jnp.mean(o_t * o_t, axis=0, keepdims=True)
            o_t = o_t * lax.rsqrt(ms_o + NORM_EPS) * gain_ref[...] * (1.0 - lambda_init)
            o_ref[q_rows, head_lanes[hh]] = o_t.T.astype(BF16)
        return carry

    lax.fori_loop(0, t // tq, q_tile, 0)


def _prompt_attn(qb, kb, vb, lq1, lk1, lq2, lk2, gain, b, t, tq, lambda_init):
    nq = t // tq
    hp = ATTN_HEADS_PER_STEP
    vec = lambda bi, h: (0, 0)
    seq_head = lambda bi, h: (bi, h)
    return pl.pallas_call(
        functools.partial(_prompt_attn_kernel, tq=tq, lambda_init=lambda_init),
        grid=(b, A_HEADS // hp),
        in_specs=[
            pl.BlockSpec((t, hp * A_QK_DIM), seq_head),
            pl.BlockSpec((t, hp * A_QK_DIM), seq_head),
            pl.BlockSpec((t, hp * A_V_DIM), seq_head),
            pl.BlockSpec((1, A_HEAD_DIM), vec),
            pl.BlockSpec((1, A_HEAD_DIM), vec),
            pl.BlockSpec((1, A_HEAD_DIM), vec),
            pl.BlockSpec((1, A_HEAD_DIM), vec),
            pl.BlockSpec((A_V_DIM, 1), vec),
        ],
        out_specs=pl.BlockSpec((t, hp * A_V_DIM), seq_head),
        out_shape=jax.ShapeDtypeStruct((b * t, A_WIDTH), BF16),
        scratch_shapes=[pltpu.VMEM((2 * hp, t, A_QK_DIM), BF16),
                        pltpu.VMEM((hp, nq, ATTN_ACC_ROWS, tq), BF16),
                        pltpu.VMEM((2 * hp, ATTN_ACC_ROWS, tq), F32)],
        compiler_params=pltpu.CompilerParams(
            dimension_semantics=("arbitrary", "arbitrary"),
            vmem_limit_bytes=V7X_VMEM_LIMIT),
        name="prompt_attn",
    )(qb, kb, vb, lq1, lk1, lq2, lk2, gain.reshape(A_V_DIM, 1))


def _decode_attn_kernel(pt_ref, q_ref, kn_ref, vn_ref, lq1_ref, lk1_ref, lq2_ref, lk2_ref,
                        gain_ref, *refs, pps, n_pages, lambda_init):
    k_refs = refs[:pps]
    v_refs = refs[pps:2 * pps]
    o_ref = refs[2 * pps]
    m_ref, l_ref, acc_ref = refs[2 * pps + 1:]
    j = pl.program_id(1)
    rows = 2 * A_HEADS
    page_rows = PAGE_SIZE * A_HEADS
    row_q = _iota((rows, A_QK_DIM), 0)
    lane_q = _iota((rows, A_QK_DIM), 1)
    qm = jnp.where((lane_q >> 6) == (row_q & 1), q_ref[0], 0.0).astype(BF16)
    slope = jnp.exp2((-8.0 / A_HEADS) * ((_iota((rows, 1), 0) >> 1) + 1).astype(F32))
    col = _iota((rows, pps * page_rows), 1)
    own_head = (col & (A_HEADS - 1)) == (_iota((rows, pps * page_rows), 0) >> 1)
    dist = (n_pages - j * pps) * PAGE_SIZE - (col >> 2)

    @pl.when(j == 0)
    def _():
        m_ref[...] = jnp.full((rows, 1), NEG_BIG, F32)
        l_ref[...] = jnp.zeros((rows, 1), F32)
        acc_ref[...] = jnp.zeros((rows, A_V_DIM), F32)

    s = jnp.concatenate([_dot_nt(qm, k_refs[u][...].astype(BF16)) for u in range(pps)], axis=1)
    s = jnp.where(own_head, s - slope * dist.astype(F32), NEG_BIG)
    m = m_ref[...]
    m_new = jnp.maximum(m, jnp.max(s, axis=-1, keepdims=True))
    alpha = jnp.exp(m - m_new)
    p = jnp.exp(s - m_new)
    pb = p.astype(BF16)
    pv = _dot(pb[:, 0:page_rows], v_refs[0][...].astype(BF16))
    for u in range(1, pps):
        pv = pv + _dot(pb[:, u * page_rows:(u + 1) * page_rows], v_refs[u][...].astype(BF16))
    m = m_new
    l = alpha * l_ref[...] + jnp.sum(p, axis=-1, keepdims=True)
    acc = alpha * acc_ref[...] + pv
    m_ref[...], l_ref[...], acc_ref[...] = m, l, acc

    @pl.when(j == pl.num_programs(1) - 1)
    def _():
        s_new = jnp.sum(qm.astype(F32) * kn_ref[0], axis=-1, keepdims=True)
        m_new = jnp.maximum(m, s_new)
        alpha = jnp.exp(m - m_new)
        p_new = jnp.exp(s_new - m_new)
        l_fin = alpha * l + p_new
        acc_fin = alpha * acc + p_new.astype(BF16).astype(F32) * vn_ref[0]
        lam = _lambda_full(lq1_ref[...], lk1_ref[...], lq2_ref[...], lk2_ref[...], lambda_init)
        comp = _iota((rows, 1), 0) & 1
        x = acc_fin * (jnp.where(comp == 0, 1.0, -lam) / l_fin)
        o = x + pltpu.roll(x, shift=rows - 1, axis=0)
        o_ref[0] = _rms_rows(o, gain_ref[...]) * (1.0 - lambda_init)


def _decode_attn(page_table, qb, kb, vb, cache_k, cache_v, layer, lq1, lk1, lq2, lk2, gain,
                 pps, lambda_init):
    db, n_pages = page_table.shape
    n_pool = cache_k.shape[1]
    page_rows = PAGE_SIZE * A_HEADS
    ck = cache_k.reshape(-1, A_QK_DIM)
    cv = cache_v.reshape(-1, A_V_DIM)
    pt = page_table.reshape(db * n_pages)
    rows = 2 * A_HEADS
    tok3 = lambda b, j, pt_ref: (b, 0, 0)
    vec = lambda b, j, pt_ref: (0, 0)

    def page_spec(u):
        return pl.BlockSpec(
            (page_rows, A_QK_DIM),
            lambda b, j, pt_ref: (layer * n_pool + pt_ref[b * n_pages + j * pps + u], 0))

    def per_row(z):
        return jnp.repeat(z.astype(F32).reshape(db, A_HEADS, A_QK_DIM), 2, axis=1)

    grid_spec = pltpu.PrefetchScalarGridSpec(
        num_scalar_prefetch=1,
        grid=(db, n_pages // pps),
        in_specs=[
            pl.BlockSpec((1, rows, A_QK_DIM), tok3),
            pl.BlockSpec((1, rows, A_QK_DIM), tok3),
            pl.BlockSpec((1, rows, A_V_DIM), tok3),
            pl.BlockSpec((1, A_HEAD_DIM), vec),
            pl.BlockSpec((1, A_HEAD_DIM), vec),
            pl.BlockSpec((1, A_HEAD_DIM), vec),
            pl.BlockSpec((1, A_HEAD_DIM), vec),
            pl.BlockSpec((1, A_V_DIM), vec),
        ] + [page_spec(u) for u in range(pps)] + [page_spec(u) for u in range(pps)],
        out_specs=pl.BlockSpec((1, rows, A_V_DIM), tok3),
        scratch_shapes=[pltpu.VMEM((rows, 1), F32), pltpu.VMEM((rows, 1), F32),
                        pltpu.VMEM((rows, A_V_DIM), F32)],
    )
    out = pl.pallas_call(
        functools.partial(_decode_attn_kernel, pps=pps, n_pages=n_pages, lambda_init=lambda_init),
        grid_spec=grid_spec,
        out_shape=jax.ShapeDtypeStruct((db, rows, A_V_DIM), F32),
        compiler_params=pltpu.CompilerParams(
            dimension_semantics=("arbitrary", "arbitrary"), vmem_limit_bytes=V7X_VMEM_LIMIT),
        name="decode_attn",
    )(pt, per_row(qb), per_row(kb), per_row(vb),
      lq1, lk1, lq2, lk2, gain, *([ck] * pps), *([cv] * pps))
    return out[:, 0::2, :].reshape(db, A_WIDTH)


def _rwkv_prep(x, prev, mu, w0, w2e, a0, a2e, g2, k_k, k_a, r_k, ones):
    m = x + (prev - x) * mu
    r = m[:, 0:R_WIDTH]
    k = m[:, R_WIDTH:2 * R_WIDTH]
    v = m[:, 2 * R_WIDTH:3 * R_WIDTH]
    xwa = m[:, 3 * R_WIDTH:3 * R_WIDTH + R_LORA_W + R_LORA_A]
    xg = m[:, 3 * R_WIDTH + R_LORA_W + R_LORA_A:]
    z = w0 + _dot(jnp.tanh(xwa).astype(BF16), w2e)
    lw = (-math.exp(-0.5)) * _sigmoid(z)
    a = _sigmoid(a0 + _dot(xwa.astype(BF16), a2e))
    g = _dot(_sigmoid(xg).astype(BF16), g2)
    kk = k * k_k
    kk = kk * lax.rsqrt(jnp.maximum(_group_sum(kk * kk, ones), 1e-24))
    k = k * (1.0 + (a - 1.0) * k_a)
    bonus = _group_sum(r * k * r_k, ones) * v
    return r, lw, k, v, kk, a, g, bonus


def _rwkv_finish(y, bonus, g, gn_w, gn_b, ones):
    mu = _group_sum(y, ones) * (1.0 / R_HEAD)
    d = y - mu
    var = _group_sum(d * d, ones) * (1.0 / R_HEAD)
    yn = d * lax.rsqrt(var + GN_EPS) * gn_w + gn_b
    return ((yn + bonus) * g).astype(BF16)


def _lane_halves(x, width):
    lane = _iota(x.shape, 1)
    zero = jnp.zeros_like(x)
    return jnp.concatenate([jnp.where(lane < width, x, zero), jnp.where(lane >= width, x, zero)],
                           axis=0)


def _lane_blocks(x, width):
    block = _iota(x.shape, 1) >> int(math.log2(width))
    zero = jnp.zeros_like(x)
    return jnp.concatenate([jnp.where(block == b, x, zero) for b in range(x.shape[1] // width)],
                           axis=0)


def _rwkv_chunk_kernel(x_ref, *refs, chunk, nb):
    st_ref, prev_ref = refs[-3], refs[-1]

    @pl.when(pl.program_id(1) == 0)
    def _():
        prev_ref[...] = jnp.zeros_like(prev_ref)
        st_ref[...] = jnp.zeros_like(st_ref)

    def body(k, carry):
        _rwkv_one_chunk(jnp.asarray(k, jnp.int32), x_ref, *refs, chunk=chunk, nb=nb)
        return carry

    lax.fori_loop(0, x_ref.shape[1] // chunk, body, 0)


def _rwkv_one_chunk(k, x_ref, mu_ref, w0_ref, w2e_ref, a0_ref, a2e_ref, g2_ref, kk_ref, ka_ref,
                    rk_ref, gnw_ref, gnb_ref, ones_ref,
                    y_ref, st_ref, sh_ref, prev_ref, *, chunk, nb):
    c = chunk
    chunk_rows = pl.ds(pl.multiple_of(k * c, c), c)
    ones = ones_ref[...]
    rows = _iota((c, R_SHIFT_W), 0)
    tri = (_iota((c, c), 0) >= _iota((c, c), 1)).astype(BF16)
    sls = [slice(p * LANES, (p + 1) * LANES) for p in range(R_PAIRS)]
    ah_p, rh_p, v_p, bt_p, kt_p, bb_p, kb_p, pe_p, st_idx, fin = ([] for _ in range(10))
    for bi in range(nb):
        x = x_ref[bi, chunk_rows, :]
        prev = jnp.where(rows == 0, prev_ref[bi], pltpu.roll(x, shift=1, axis=0))
        prev_ref[bi] = x[c - 1:c, :]
        sh_ref[bi] = x[c - 1:c, :]
        r, lw, k, v, kk, a, g, bonus = _rwkv_prep(
            x, prev, mu_ref[...], w0_ref[...], w2e_ref[...], a0_ref[...], a2e_ref[...],
            g2_ref[...], kk_ref[...], ka_ref[...], rk_ref[...], ones)
        fin.append((bonus, g))
        l_hi = lw.astype(BF16)
        l_r1 = lw - l_hi.astype(F32)
        l_mid = l_r1.astype(BF16)
        l_lo = (l_r1 - l_mid.astype(F32)).astype(BF16)
        cum = _dot(tri, l_hi) + _dot(tri, l_mid) + _dot(tri, l_lo)
        cum_end = cum[c - 1:c, :]
        e_neg = jnp.exp(-cum)
        e_end = jnp.exp(cum_end - cum)
        kka = kk * a
        ah = -kk * jnp.exp(cum - lw)
        rh = r * jnp.exp(cum)
        bt = kka * e_neg
        kt = k * e_neg
        bb = kka * e_end
        kb = k * e_end
        p_end = jnp.exp(cum_end)
        for p, sl in enumerate(sls):
            for lst, z in ((ah_p, ah), (rh_p, rh), (v_p, v), (bt_p, bt), (kt_p, kt),
                           (bb_p, bb), (kb_p, kb), (pe_p, p_end)):
                lst.append(z[:, sl])
            st_idx.append((bi, p))

    t_idx = _iota((c, 2 * c), 0)
    s_idx = _iota((c, 2 * c), 1) & (c - 1)
    strict = t_idx > s_idx
    incl = t_idx >= s_idx
    eye_pack = (t_idx == s_idx).astype(F32)
    bd_mask = (_iota((LANES, LANES), 0) >> 6) == (_iota((LANES, LANES), 1) >> 6)
    eye_l = _iota((LANES, LANES), 0) == _iota((LANES, LANES), 1)
    zeros_c = jnp.zeros((c, LANES), F32)

    units = range(len(st_idx))
    v_blk = [_lane_halves(z, R_HEAD) for z in v_p]
    a_all = [_mm(jnp.concatenate([ah_p[u], rh_p[u]], axis=0),
                 jnp.concatenate([_lane_halves(bt_p[u], R_HEAD), _lane_halves(kt_p[u], R_HEAD)],
                                 axis=0), _dot_nt) for u in units]
    a_ab = [jnp.where(strict, z[0:c, 0:2 * c], 0.0) for z in a_all]
    a_ak = [jnp.where(strict, z[0:c, 2 * c:4 * c], 0.0) for z in a_all]
    a_rb = [jnp.where(incl, z[c:2 * c, 0:2 * c], 0.0) for z in a_all]
    a_rk = [jnp.where(incl, z[c:2 * c, 2 * c:4 * c], 0.0) for z in a_all]
    w1 = [_mm(a_ak[u], v_blk[u]) for u in units]
    quads = range(len(st_idx) // 2)
    a4 = [jnp.concatenate([a_ab[2 * q], a_ab[2 * q + 1]], axis=1) for q in quads]
    eye4 = jnp.concatenate([eye_pack, eye_pack], axis=1)
    inv4 = [eye4 + z for z in a4]
    bd = [_lane_blocks(z, c) for z in a4]
    pw = [_mm(a4[q], bd[q]) for q in quads]
    n_sq = int(math.log2(c)) - 1
    for step in range(n_sq):
        bd = [_lane_blocks(z, c) for z in pw]
        if step < n_sq - 1:
            both = [_mm(jnp.concatenate([pw[q], inv4[q]], axis=0), bd[q]) for q in quads]
            pw = [z[0:c] for z in both]
            inv4 = [inv4[q] + both[q][c:2 * c] for q in quads]
        else:
            inv4 = [inv4[q] + _mm(inv4[q], bd[q]) for q in quads]
    inv = [inv4[u // 2][:, (u % 2) * 2 * c:(u % 2 + 1) * 2 * c] for u in units]
    t4 = [_mm(inv[u], jnp.concatenate([_lane_halves(w1[u], R_HEAD),
                                       _lane_halves(ah_p[u], R_HEAD)], axis=1)) for u in units]
    wu = [z[:, 0:LANES] for z in t4]
    ah2 = [z[:, LANES:] for z in t4]
    zeros_blk = jnp.zeros((2 * c, LANES), F32)
    t5 = [_mm(jnp.concatenate([a_rb[u], a_rk[u]], axis=1),
              jnp.concatenate([
                  jnp.concatenate([_lane_halves(wu[u], R_HEAD), _lane_halves(ah2[u], R_HEAD)],
                                  axis=1),
                  jnp.concatenate([v_blk[u], zeros_blk], axis=1)], axis=0)) for u in units]
    t6 = [_mm(jnp.concatenate([bb_p[u], kb_p[u]], axis=0),
              jnp.concatenate([jnp.concatenate([wu[u], ah2[u]], axis=1),
                               jnp.concatenate([v_p[u], zeros_c], axis=1)], axis=0), _dot_tn)
          for u in units]
    st = [st_ref[bi, p] for bi, p in st_idx]
    ys = [_mm(rh_p[u] + t5[u][:, LANES:], st[u]) + t5[u][:, 0:LANES] for u in units]
    for u in units:
        h_new = jnp.where(bd_mask, t6[u][:, 0:LANES], 0.0)
        g_x = jnp.where(bd_mask, t6[u][:, LANES:], 0.0)
        p_col = jnp.sum(jnp.where(eye_l, pe_p[u], 0.0), axis=1, keepdims=True)
        st_ref[st_idx[u]] = p_col * st[u] + _mm(g_x, st[u]) + h_new

    for bi in range(nb):
        y = jnp.concatenate(ys[bi * R_PAIRS:(bi + 1) * R_PAIRS], axis=1)
        bonus, g = fin[bi]
        y_ref[bi, chunk_rows, :] = _rwkv_finish(y, bonus, g, gnw_ref[...], gnb_ref[...], ones)


def _rwkv_prompt(rc3, params, chunk, nb):
    b, t, _ = rc3.shape
    rows = _pick_tile(t, RWKV_CHUNKS_PER_STEP * chunk)
    const = lambda bi, j: (0, 0)
    specs = [pl.BlockSpec(p.shape, const) for p in params]
    return pl.pallas_call(
        functools.partial(_rwkv_chunk_kernel, chunk=chunk, nb=nb),
        grid=(b // nb, t // rows),
        in_specs=[pl.BlockSpec((nb, rows, R_SHIFT_W), lambda bi, j: (bi, j, 0))] + specs,
        out_specs=[
            pl.BlockSpec((nb, rows, R_WIDTH), lambda bi, j: (bi, j, 0)),
            pl.BlockSpec((nb, R_PAIRS, LANES, LANES), lambda bi, j: (bi, 0, 0, 0)),
            pl.BlockSpec((nb, 1, R_SHIFT_W), lambda bi, j: (bi, 0, 0)),
        ],
        out_shape=(
            jax.ShapeDtypeStruct((b, t, R_WIDTH), BF16),
            jax.ShapeDtypeStruct((b, R_PAIRS, LANES, LANES), F32),
            jax.ShapeDtypeStruct((b, 1, R_SHIFT_W), F32),
        ),
        scratch_shapes=[pltpu.VMEM((nb, 1, R_SHIFT_W), F32)],
        compiler_params=pltpu.CompilerParams(
            dimension_semantics=("arbitrary", "arbitrary"), vmem_limit_bytes=V7X_VMEM_LIMIT),
        name="rwkv_chunk",
    )(rc3, *params)


def _rwkv_step_kernel(x_ref, prev_ref, st_in_ref, mu_ref, w0_ref, w2e_ref, a0_ref, a2e_ref,
                      g2_ref, kk_ref, ka_ref, rk_ref, gnw_ref, gnb_ref, ones_ref,
                      y_ref, st_ref, yrow_ref, *, rows):
    ones = ones_ref[...]
    r, lw, k, v, kk, a, g, bonus = _rwkv_prep(
        x_ref[...], prev_ref[...], mu_ref[...], w0_ref[...], w2e_ref[...], a0_ref[...],
        a2e_ref[...], g2_ref[...], kk_ref[...], ka_ref[...], rk_ref[...], ones)
    vecs = (jnp.exp(lw), kk * a, k, v, kk, r)
    row8 = _iota((SUBLANES, R_HEAD), 0)

    def rows8(*row_vecs):
        out = jnp.zeros((SUBLANES, R_HEAD), F32)
        for n, vec in enumerate(row_vecs):
            out = jnp.where(row8 == n, vec, out)
        return out

    def hi_lo(x):
        hi = x.astype(BF16).astype(F32)
        return hi, x - hi

    def vec_mat(vec_row, s):
        return _mm(jnp.broadcast_to(vec_row, (SUBLANES, R_HEAD)), s, _dot_nt)[0:1]

    group = math.gcd(rows, 4)
    for g0 in range(0, rows, group):
        units = [(bi, h) for bi in range(g0, g0 + group) for h in range(R_HEADS)]
        w_r, b_r, k_r, v_r, kk_r, r_r = (
            [z[bi:bi + 1, h * R_HEAD:(h + 1) * R_HEAD] for bi, h in units] for z in vecs)
        ids = range(len(units))
        s_old = [st_in_ref[bi, h] for bi, h in units]
        sa = [-vec_mat(kk_r[u], s_old[u]) for u in ids]
        upd = []
        for u in ids:
            (sa_h, sa_l), (b_h, b_l) = hi_lo(sa[u]), hi_lo(b_r[u])
            (v_h, v_l), (k_h, k_l) = hi_lo(v_r[u]), hi_lo(k_r[u])
            upd.append(_mm(rows8(sa_h, sa_h, sa_l, v_h, v_h, v_l),
                           rows8(b_h, b_l, b_h, k_h, k_l, k_h), _dot_tn))
        s_new = [s_old[u] * w_r[u] + upd[u] for u in ids]
        for u, (bi, h) in enumerate(units):
            st_ref[bi, h] = s_new[u]
        y_r = [vec_mat(r_r[u], s_new[u]) for u in ids]
        for n, bi in enumerate(range(g0, g0 + group)):
            yrow_ref[bi:bi + 1, :] = jnp.concatenate(y_r[n * R_HEADS:(n + 1) * R_HEADS], axis=1)
    y_ref[...] = _rwkv_finish(yrow_ref[...], bonus, g, gnw_ref[...], gnb_ref[...], ones)


def _rwkv_step(rc, shift_prev, state, params, rows):
    db = rc.shape[0]
    const = lambda i: (0, 0)
    specs = [pl.BlockSpec(p.shape, const) for p in params]
    state_spec = pl.BlockSpec((rows, R_HEADS, R_HEAD, R_HEAD), lambda i: (i, 0, 0, 0))
    return pl.pallas_call(
        functools.partial(_rwkv_step_kernel, rows=rows),
        grid=(db // rows,),
        in_specs=[
            pl.BlockSpec((rows, R_SHIFT_W), lambda i: (i, 0)),
            pl.BlockSpec((rows, R_SHIFT_W), lambda i: (i, 0)),
            state_spec,
        ] + specs,
        out_specs=[pl.BlockSpec((rows, R_WIDTH), lambda i: (i, 0)), state_spec],
        out_shape=(
            jax.ShapeDtypeStruct((db, R_WIDTH), BF16),
            jax.ShapeDtypeStruct((db, R_HEADS, R_HEAD, R_HEAD), F32),
        ),
        scratch_shapes=[pltpu.VMEM((rows, R_WIDTH), F32)],
        compiler_params=pltpu.CompilerParams(
            dimension_semantics=("arbitrary",), vmem_limit_bytes=V7X_VMEM_LIMIT),
        name="rwkv_step",
    )(rc, shift_prev, state, *params)


def _pairs_to_state(st):
    b = st.shape[0]
    blocks = st.reshape(b, R_PAIRS, 2, R_HEAD, 2, R_HEAD)
    diag = jnp.stack([blocks[:, :, 0, :, 0, :], blocks[:, :, 1, :, 1, :]], axis=2)
    return jnp.swapaxes(diag.reshape(b, R_HEADS, R_HEAD, R_HEAD), -1, -2)


def _merge_ffn_kernel(x_ref, o_ref, rw_ref, g_ref, wpa_ref, wpb_ref, wout_ref, nf_ref,
                      wg_ref, wu_ref, wd_ref, y_ref, *, d_ff):
    ya = _dot(o_ref[...].astype(BF16), wpa_ref[...])
    yb = _dot(rw_ref[...], wpb_ref[...])
    merged = (g_ref[:, 0:D_MODEL] * ya + g_ref[:, D_MODEL:] * yb).astype(BF16)
    x1 = x_ref[...] + _dot(merged, wout_ref[...])
    hf = _rms_rows(x1, nf_ref[...]).astype(BF16)
    acc = x1
    step = 512
    for c in range(0, d_ff, step):
        n = min(step, d_ff - c)
        gate = _dot(hf, wg_ref[:, c:c + n])
        up = _dot(hf, wu_ref[:, c:c + n])
        act = (gate * _sigmoid(gate) * up).astype(BF16)
        acc = acc + _dot(act, wd_ref[c:c + n, :])
    y_ref[...] = acc


def _merge_ffn(x2, o, rw, gates, wpa, wpb, wout, nf, wg, wu, wd, tm):
    m = x2.shape[0]
    d_ff = wg.shape[1]
    row = lambda i: (i, 0)
    const = lambda i: (0, 0)
    return pl.pallas_call(
        functools.partial(_merge_ffn_kernel, d_ff=d_ff),
        grid=(m // tm,),
        in_specs=[
            pl.BlockSpec((tm, D_MODEL), row),
            pl.BlockSpec((tm, A_WIDTH), row),
            pl.BlockSpec((tm, R_WIDTH), row),
            pl.BlockSpec((tm, 2 * D_MODEL), row),
            pl.BlockSpec(wpa.shape, const),
            pl.BlockSpec(wpb.shape, const),
            pl.BlockSpec(wout.shape, const),
            pl.BlockSpec((1, D_MODEL), const),
            pl.BlockSpec(wg.shape, const),
            pl.BlockSpec(wu.shape, const),
            pl.BlockSpec(wd.shape, const),
        ],
        out_specs=pl.BlockSpec((tm, D_MODEL), row),
        out_shape=jax.ShapeDtypeStruct((m, D_MODEL), F32),
        compiler_params=pltpu.CompilerParams(
            dimension_semantics=("arbitrary",), vmem_limit_bytes=V7X_VMEM_LIMIT),
        name="merge_ffn",
    )(x2, o, rw, gates, wpa, wpb, wout, nf, wg, wu, wd)


def _row(v):
    return v.reshape(1, -1).astype(F32)


def _pick_tile(m, target):
    t = min(m, target)
    while m % t:
        t //= 2
    return t


def kernel(x_prompt, x_sample, cache_k, cache_v, page_table, state_wkv, state_shift, norm_mix, w_in, q_gain, k_gain, lambda_q1, lambda_k1, lambda_q2, lambda_k2, attn_out_gain, w_pa, shift_mu, w0, w2, a0, a2, g2, k_k, k_a, r_k, gn_w, gn_b, w_pb, w_out, norm_ffn, w_gate, w_up, w_down):
    b, t, _ = x_prompt.shape
    db, ds, _ = x_sample.shape
    assert ds == 1, "sample group carries one new token per sequence"
    depth = w_in.shape[0]
    n_pages = page_table.shape[1]
    xp = x_prompt.reshape(b * t, D_MODEL)
    xs = x_sample.reshape(db, D_MODEL)
    head_ones = (jnp.arange(A_WIDTH)[:, None] // A_HEAD_DIM
                 == jnp.arange(A_WIDTH)[None, :] // A_HEAD_DIM).astype(BF16)
    outs = [[] for _ in range(8)]
    tm_p = _pick_tile(b * t, 512)
    tm_s = _pick_tile(db, 512)
    tq = _pick_tile(t, 512)
    chunk = _pick_tile(t, RWKV_CHUNK)
    chunk_nb = _pick_tile(b, 4)
    pps = _pick_tile(n_pages, 32)
    step_rows = _pick_tile(db, 16)
    for l in range(depth):
        lambda_init = 0.8 - 0.6 * math.exp(-0.3 * l)
        w_in_l = w_in[l]
        qg = _row(jnp.tile(q_gain[l], A_WIDTH // A_HEAD_DIM))
        kg = _row(jnp.tile(k_gain[l], A_WIDTH // A_HEAD_DIM))
        nm = _row(norm_mix[l])
        lam_vecs = (_row(lambda_q1[l]), _row(lambda_k1[l]), _row(lambda_q2[l]), _row(lambda_k2[l]))
        og = _row(attn_out_gain[l])
        zeros_lora = jnp.zeros((R_LORA_W, R_WIDTH), F32)
        rwkv_params = (
            _row(shift_mu[l]), _row(w0[l]),
            jnp.concatenate([w2[l], zeros_lora], axis=0).astype(BF16),
            _row(a0[l]),
            jnp.concatenate([zeros_lora, a2[l]], axis=0).astype(BF16),
            g2[l].astype(BF16), _row(k_k[l]), _row(k_a[l]), _row(r_k[l]),
            _row(gn_w[l]), _row(gn_b[l]), head_ones)
        ffn_w = (w_pa[l].astype(BF16), w_pb[l].astype(BF16), w_out[l].astype(BF16),
                 _row(norm_ffn[l]), w_gate[l].astype(BF16), w_up[l].astype(BF16),
                 w_down[l].astype(BF16))

        qb, k32, kb, v32, vb, rc, gates = _proj(xp, nm, w_in_l, qg, kg, head_ones, tm_p)
        o = _prompt_attn(qb, kb, vb, *lam_vecs, og, b, t, tq, lambda_init)
        rw, st_p, sh_p = _rwkv_prompt(rc.reshape(b, t, R_SHIFT_W), rwkv_params, chunk, chunk_nb)
        xp = _merge_ffn(xp, o, rw.reshape(b * t, R_WIDTH), gates, *ffn_w, tm_p)
        outs[0].append(k32.reshape(b, t, A_HEADS, A_QK_DIM))
        outs[1].append(v32.reshape(b, t, A_HEADS, A_V_DIM))
        outs[2].append(_pairs_to_state(st_p))
        outs[3].append(sh_p.reshape(b, R_SHIFT_W))

        qb, k32, kb, v32, vb, rc, gates = _proj(xs, nm, w_in_l, qg, kg, head_ones, tm_s)
        o = _decode_attn(page_table, qb, kb, vb, cache_k, cache_v, l, *lam_vecs, og,
                         pps, lambda_init)
        rw, st_s = _rwkv_step(rc, state_shift[l], state_wkv[l], rwkv_params,
                              step_rows)
        xs = _merge_ffn(xs, o, rw, gates, *ffn_w, tm_s)
        outs[4].append(k32.reshape(db, 1, A_HEADS, A_QK_DIM))
        outs[5].append(v32.reshape(db, 1, A_HEADS, A_V_DIM))
        outs[6].append(st_s)
        outs[7].append(rc)
    return (xp.reshape(b, t, D_MODEL), xs.reshape(db, 1, D_MODEL),
            *(jnp.stack(o) for o in outs))
```

```python
import functools
import math

import jax
import jax.numpy as jnp
from jax import lax
from jax.experimental import pallas as pl
from jax.experimental.pallas import tpu as pltpu

F32 = jnp.float32
BF16 = jnp.bfloat16

D_MODEL = 1024
PAGE_SIZE = 128
A_HEADS = 4
A_HEAD_DIM = 64
A_QK_DIM = 2 * A_HEAD_DIM
A_V_DIM = 2 * A_HEAD_DIM
A_WIDTH = A_HEADS * A_V_DIM
R_HEAD = 64
R_HEADS = 8
R_WIDTH = R_HEADS * R_HEAD
R_LORA_W = 64
R_LORA_A = 64
R_LORA_G = 128
R_SHIFT_W = 3 * R_WIDTH + R_LORA_W + R_LORA_A + R_LORA_G
OFF_Q = 0
OFF_K = OFF_Q + A_HEADS * A_QK_DIM
OFF_V = OFF_K + A_HEADS * A_QK_DIM
OFF_R = OFF_V + A_WIDTH
OFF_G = OFF_R + R_SHIFT_W
IN_WIDTH = OFF_G + 2 * D_MODEL
NORM_EPS = 1e-6
GN_EPS = 64e-5
NEG_BIG = -1e30

V7X_VMEM_LIMIT = 56 * 1024 * 1024
LANES = 128
SUBLANES = 8
R_PAIRS = R_WIDTH // LANES
RWKV_CHUNK = 64
RWKV_CHUNKS_PER_STEP = 4


def _dot(a, b):
    return jnp.dot(a, b, preferred_element_type=F32)


def _dot_nt(a, b):
    return lax.dot_general(a, b, (((1,), (1,)), ((), ())), preferred_element_type=F32)


def _dot_tn(a, b):
    return lax.dot_general(a, b, (((0,), (0,)), ((), ())), preferred_element_type=F32)


def _mm(a, b, dot=_dot):
    return dot(a.astype(BF16), b.astype(BF16))


def _sigmoid(x):
    return 1.0 / (1.0 + jnp.exp(-x))


def _iota(shape, dim):
    return lax.broadcasted_iota(jnp.int32, shape, dim)


def _group_sum(z, ones_bf):
    return _dot(z.astype(BF16), ones_bf)


def _rms_rows(x, gain_row):
    ms = jnp.mean(x * x, axis=-1, keepdims=True)
    return x * lax.rsqrt(ms + NORM_EPS) * gain_row


def _proj_kernel(x_ref, nm_ref, w_ref, qg_ref, kg_ref, ones_ref,
                 q_ref, k_ref, kb_ref, v_ref, vb_ref, r_ref, g_ref):
    h = _rms_rows(x_ref[...], nm_ref[...]).astype(BF16)
    ones = ones_ref[...]

    def head_norm(z, gain_row):
        ss = _group_sum(z * z, ones)
        return z * lax.rsqrt(ss * (1.0 / A_HEAD_DIM) + NORM_EPS) * gain_row

    def w(lo, hi):
        return w_ref[:, lo:hi].astype(BF16)

    q = head_norm(_dot(h, w(OFF_Q, OFF_K)), qg_ref[...])
    q_ref[...] = (q * (A_HEAD_DIM ** -0.5)).astype(BF16)
    tm = x_ref.shape[0]
    k = head_norm(_dot(h, w(OFF_K, OFF_V)), kg_ref[...])
    kb_ref[...] = k.astype(BF16)
    v = _dot(h, w(OFF_V, OFF_R))
    vb_ref[...] = v.astype(BF16)
    for hd in range(A_HEADS):
        sl = slice(hd * A_QK_DIM, (hd + 1) * A_QK_DIM)
        k_ref[pl.ds(hd, tm, stride=A_HEADS), :] = k[:, sl]
        v_ref[pl.ds(hd, tm, stride=A_HEADS), :] = v[:, sl]
    step = 512
    for c in range(0, R_SHIFT_W, step):
        n = min(step, R_SHIFT_W - c)
        r_ref[:, c:c + n] = _dot(h, w(OFF_R + c, OFF_R + c + n))
    for c in range(0, 2 * D_MODEL, step):
        g_ref[:, c:c + step] = _sigmoid(_dot(h, w(OFF_G + c, OFF_G + c + step)))


def _proj(x2, nm, w_in_l, qg, kg, ones_bf, tm):
    m = x2.shape[0]
    row = lambda i: (i, 0)
    const = lambda i: (0, 0)
    out_shape = (
        jax.ShapeDtypeStruct((m, A_WIDTH), BF16),
        jax.ShapeDtypeStruct((m * A_HEADS, A_QK_DIM), F32),
        jax.ShapeDtypeStruct((m, A_WIDTH), BF16),
        jax.ShapeDtypeStruct((m * A_HEADS, A_V_DIM), F32),
        jax.ShapeDtypeStruct((m, A_WIDTH), BF16),
        jax.ShapeDtypeStruct((m, R_SHIFT_W), F32),
        jax.ShapeDtypeStruct((m, 2 * D_MODEL), F32),
    )
    return pl.pallas_call(
        _proj_kernel,
        grid=(m // tm,),
        in_specs=[
            pl.BlockSpec((tm, D_MODEL), row),
            pl.BlockSpec((1, D_MODEL), const),
            pl.BlockSpec((D_MODEL, IN_WIDTH), const),
            pl.BlockSpec((1, A_WIDTH), const),
            pl.BlockSpec((1, A_WIDTH), const),
            pl.BlockSpec((A_WIDTH, A_WIDTH), const),
        ],
        out_specs=[
            pl.BlockSpec((tm, A_WIDTH), row),
            pl.BlockSpec((tm * A_HEADS, A_QK_DIM), row),
            pl.BlockSpec((tm, A_WIDTH), row),
            pl.BlockSpec((tm * A_HEADS, A_V_DIM), row),
            pl.BlockSpec((tm, A_WIDTH), row),
            pl.BlockSpec((tm, R_SHIFT_W), row),
            pl.BlockSpec((tm, 2 * D_MODEL), row),
        ],
        out_shape=out_shape,
        compiler_params=pltpu.CompilerParams(
            dimension_semantics=("arbitrary",), vmem_limit_bytes=V7X_VMEM_LIMIT),
        name="proj",
    )(x2, nm, w_in_l, qg, kg, ones_bf)


def _lambda_full(lq1, lk1, lq2, lk2, lambda_init):
    s1 = jnp.sum(lq1 * lk1, axis=-1, keepdims=True)
    s2 = jnp.sum(lq2 * lk2, axis=-1, keepdims=True)
    return jnp.exp(s1) - jnp.exp(s2) + lambda_init


ATTN_ACC_ROWS = A_V_DIM + 16
ATTN_HEADS_PER_STEP = 1


def _alibi_lanes(pos, slope, keys):
    l4 = _iota(pos.shape, 1) & (A_HEAD_DIM - 1)
    lo = slope * (pos & 255).astype(F32)
    hi = slope * ((pos >> 8) << 8).astype(F32)
    if keys:
        val = jnp.where(l4 < 2, 1.0, jnp.where(l4 == 2, lo, jnp.where(l4 == 3, hi, 0.0)))
    else:
        val = jnp.where(l4 == 0, -lo, jnp.where(l4 == 1, -hi, jnp.where(l4 < 4, 1.0, 0.0)))
    return val.astype(BF16)


def _prompt_attn_kernel(q_ref, k_ref, v_ref, lq1_ref, lk1_ref, lq2_ref, lk2_ref, gain_ref,
                        o_ref, kc_ref, vt_ref, acc_ref, *, tq, lambda_init):
    hg = pl.program_id(1)
    t = k_ref.shape[0]
    hp = ATTN_HEADS_PER_STEP
    n_st = 2 * hp
    slopes = [jnp.exp2(jnp.full((1, 1), -8.0 / A_HEADS, F32) * (hg * hp + hh + 1).astype(F32))
              for hh in range(hp)]
    head_lanes = [slice(hh * A_QK_DIM, (hh + 1) * A_QK_DIM) for hh in range(hp)]
    pos = _iota((tq, A_QK_DIM), 0)
    low = _iota((tq, A_QK_DIM), 1) < A_HEAD_DIM

    ones_rows = (_iota((ATTN_ACC_ROWS - A_V_DIM, tq), 0) == 0).astype(BF16)
    for hh in range(hp):
        ek = _alibi_lanes(pos, slopes[hh], keys=True)
        for jc in range(t // tq):
            rows = slice(jc * tq, (jc + 1) * tq)
            kt = k_ref[rows, head_lanes[hh]]
            kc_ref[2 * hh, rows, :] = jnp.where(low, kt, ek)
            kc_ref[2 * hh + 1, rows, :] = jnp.where(low, ek, kt)
            vt_ref[hh, jc, 0:A_V_DIM, :] = v_ref[rows, head_lanes[hh]].astype(F32).T.astype(BF16)
            vt_ref[hh, jc, A_V_DIM:, :] = ones_rows

    eqs = [_alibi_lanes(pos, slopes[hh], keys=False) for hh in range(hp)]
    lam = _lambda_full(lq1_ref[...], lk1_ref[...], lq2_ref[...], lk2_ref[...], lambda_init)
    kv_pos = _iota((tq, tq), 0)
    q_pos = _iota((tq, tq), 1)

    def q_tile(i, carry):
        i = jnp.asarray(i, jnp.int32)
        q_rows = pl.ds(pl.multiple_of(i * tq, tq), tq)
        qc = []
        for hh in range(hp):
            q = q_ref[q_rows, head_lanes[hh]]
            qc += [jnp.where(low, q, eqs[hh]), jnp.where(low, eqs[hh], q)]

        def tiles(group, ms):
            s = []
            for j, masked in group:
                rows = pl.ds(pl.multiple_of(jnp.asarray(j * tq, jnp.int32), tq), tq)
                sj = [_dot_nt(kc_ref[st, rows, :], qc[st]) for st in range(n_st)]
                if masked:
                    sj = [jnp.where(kv_pos <= q_pos, sc, NEG_BIG) for sc in sj]
                s.append(sj)
            ms = list(ms)
            acc = [acc_ref[st] for st in range(n_st)]
            for (j, _), sj in zip(group, s):
                for st in range(n_st):
                    off = slopes[st // 2] * jnp.asarray((j - i) * tq, F32)
                    m_rel = ms[st] - off
                    m_new = jnp.maximum(m_rel, jnp.max(sj[st], axis=0, keepdims=True))
                    p = jnp.exp(sj[st] - m_new).astype(BF16)
                    acc[st] = jnp.exp(m_rel - m_new) * acc[st] + _dot(vt_ref[st // 2, j], p)
                    ms[st] = m_new + off
            for st in range(n_st):
                acc_ref[st] = acc[st]
            return tuple(ms)

        acc_ref[...] = jnp.zeros_like(acc_ref)
        ms = lax.fori_loop(0, i // 2,
                           lambda pr, ms: tiles([(2 * pr, False), (2 * pr + 1, False)], ms),
                           tuple(jnp.full((1, tq), NEG_BIG, F32) for _ in range(n_st)))
        lax.cond((i & 1) == 1,
                 lambda ms: tiles([(i - 1, False), (i, True)], ms),
                 lambda ms: tiles([(i, True)], ms), ms)
        for hh in range(hp):
            a0, a1 = acc_ref[2 * hh], acc_ref[2 * hh + 1]
            o_t = (a0[0:A_V_DIM] / a0[A_V_DIM:A_V_DIM + 1]
                   - lam * (a1[0:A_V_DIM] / a1[A_V_DIM:A_V_DIM + 1]))
            ms_o = jnp.mean(o_t * o_t, axis=0, keepdims=True)
            o_t = o_t * lax.rsqrt(ms_o + NORM_EPS) * gain_ref[...] * (1.0 - lambda_init)
            o_ref[q_rows, head_lanes[hh]] = o_t.T.astype(BF16)
        return carry

    lax.fori_loop(0, t // tq, q_tile, 0)


def _prompt_attn(qb, kb, vb, lq1, lk1, lq2, lk2, gain, b, t, tq, lambda_init):
    nq = t // tq
    hp = ATTN_HEADS_PER_STEP
    vec = lambda bi, h: (0, 0)
    seq_head = lambda bi, h: (bi, h)
    return pl.pallas_call(
        functools.partial(_prompt_attn_kernel, tq=tq, lambda_init=lambda_init),
        grid=(b, A_HEADS // hp),
        in_specs=[
            pl.BlockSpec((t, hp * A_QK_DIM), seq_head),
            pl.BlockSpec((t, hp * A_QK_DIM), seq_head),
            pl.BlockSpec((t, hp * A_V_DIM), seq_head),
            pl.BlockSpec((1, A_HEAD_DIM), vec),
            pl.BlockSpec((1, A_HEAD_DIM), vec),
            pl.BlockSpec((1, A_HEAD_DIM), vec),
            pl.BlockSpec((1, A_HEAD_DIM), vec),
            pl.BlockSpec((A_V_DIM, 1), vec),
        ],
        out_specs=pl.BlockSpec((t, hp * A_V_DIM), seq_head),
        out_shape=jax.ShapeDtypeStruct((b * t, A_WIDTH), BF16),
        scratch_shapes=[pltpu.VMEM((2 * hp, t, A_QK_DIM), BF16),
                        pltpu.VMEM((hp, nq, ATTN_ACC_ROWS, tq), BF16),
                        pltpu.VMEM((2 * hp, ATTN_ACC_ROWS, tq), F32)],
        compiler_params=pltpu.CompilerParams(
            dimension_semantics=("arbitrary", "arbitrary"),
            vmem_limit_bytes=V7X_VMEM_LIMIT),
        name="prompt_attn",
    )(qb, kb, vb, lq1, lk1, lq2, lk2, gain.reshape(A_V_DIM, 1))


def _decode_attn_kernel(pt_ref, q_ref, kn_ref, vn_ref, lq1_ref, lk1_ref, lq2_ref, lk2_ref,
                        gain_ref, *refs, pps, n_pages, lambda_init):
    k_refs = refs[:pps]
    v_refs = refs[pps:2 * pps]
    o_ref = refs[2 * pps]
    m_ref, l_ref, acc_ref = refs[2 * pps + 1:]
    j = pl.program_id(1)
    rows = 2 * A_HEADS
    page_rows = PAGE_SIZE * A_HEADS
    row_q = _iota((rows, A_QK_DIM), 0)
    lane_q = _iota((rows, A_QK_DIM), 1)
    qm = jnp.where((lane_q >> 6) == (row_q & 1), q_ref[0], 0.0).astype(BF16)
    slope = jnp.exp2((-8.0 / A_HEADS) * ((_iota((rows, 1), 0) >> 1) + 1).astype(F32))
    col = _iota((rows, pps * page_rows), 1)
    own_head = (col & (A_HEADS - 1)) == (_iota((rows, pps * page_rows), 0) >> 1)
    dist = (n_pages - j * pps) * PAGE_SIZE - (col >> 2)

    @pl.when(j == 0)
    def _():
        m_ref[...] = jnp.full((rows, 1), NEG_BIG, F32)
        l_ref[...] = jnp.zeros((rows, 1), F32)
        acc_ref[...] = jnp.zeros((rows, A_V_DIM), F32)

    s = jnp.concatenate([_dot_nt(qm, k_refs[u][...].astype(BF16)) for u in range(pps)], axis=1)
    s = jnp.where(own_head, s - slope * dist.astype(F32), NEG_BIG)
    m = m_ref[...]
    m_new = jnp.maximum(m, jnp.max(s, axis=-1, keepdims=True))
    alpha = jnp.exp(m - m_new)
    p = jnp.exp(s - m_new)
    pb = p.astype(BF16)
    pv = _dot(pb[:, 0:page_rows], v_refs[0][...].astype(BF16))
    for u in range(1, pps):
        pv = pv + _dot(pb[:, u * page_rows:(u + 1) * page_rows], v_refs[u][...].astype(BF16))
    m = m_new
    l = alpha * l_ref[...] + jnp.sum(p, axis=-1, keepdims=True)
    acc = alpha * acc_ref[...] + pv
    m_ref[...], l_ref[...], acc_ref[...] = m, l, acc

    @pl.when(j == pl.num_programs(1) - 1)
    def _():
        s_new = jnp.sum(qm.astype(F32) * kn_ref[0], axis=-1, keepdims=True)
        m_new = jnp.maximum(m, s_new)
        alpha = jnp.exp(m - m_new)
        p_new = jnp.exp(s_new - m_new)
        l_fin = alpha * l + p_new
        acc_fin = alpha * acc + p_new.astype(BF16).astype(F32) * vn_ref[0]
        lam = _lambda_full(lq1_ref[...], lk1_ref[...], lq2_ref[...], lk2_ref[...], lambda_init)
        comp = _iota((rows, 1), 0) & 1
        x = acc_fin * (jnp.where(comp == 0, 1.0, -lam) / l_fin)
        o = x + pltpu.roll(x, shift=rows - 1, axis=0)
        o_ref[0] = _rms_rows(o, gain_ref[...]) * (1.0 - lambda_init)


def _decode_attn(page_table, qb, kb, vb, cache_k, cache_v, layer, lq1, lk1, lq2, lk2, gain,
                 pps, lambda_init):
    db, n_pages = page_table.shape
    n_pool = cache_k.shape[1]
    page_rows = PAGE_SIZE * A_HEADS
    ck = cache_k.reshape(-1, A_QK_DIM)
    cv = cache_v.reshape(-1, A_V_DIM)
    pt = page_table.reshape(db * n_pages)
    rows = 2 * A_HEADS
    tok3 = lambda b, j, pt_ref: (b, 0, 0)
    vec = lambda b, j, pt_ref: (0, 0)

    def page_spec(u):
        return pl.BlockSpec(
            (page_rows, A_QK_DIM),
            lambda b, j, pt_ref: (layer * n_pool + pt_ref[b * n_pages + j * pps + u], 0))

    def per_row(z):
        return jnp.repeat(z.astype(F32).reshape(db, A_HEADS, A_QK_DIM), 2, axis=1)

    grid_spec = pltpu.PrefetchScalarGridSpec(
        num_scalar_prefetch=1,
        grid=(db, n_pages // pps),
        in_specs=[
            pl.BlockSpec((1, rows, A_QK_DIM), tok3),
            pl.BlockSpec((1, rows, A_QK_DIM), tok3),
            pl.BlockSpec((1, rows, A_V_DIM), tok3),
            pl.BlockSpec((1, A_HEAD_DIM), vec),
            pl.BlockSpec((1, A_HEAD_DIM), vec),
            pl.BlockSpec((1, A_HEAD_DIM), vec),
            pl.BlockSpec((1, A_HEAD_DIM), vec),
            pl.BlockSpec((1, A_V_DIM), vec),
        ] + [page_spec(u) for u in range(pps)] + [page_spec(u) for u in range(pps)],
        out_specs=pl.BlockSpec((1, rows, A_V_DIM), tok3),
        scratch_shapes=[pltpu.VMEM((rows, 1), F32), pltpu.VMEM((rows, 1), F32),
                        pltpu.VMEM((rows, A_V_DIM), F32)],
    )
    out = pl.pallas_call(
        functools.partial(_decode_attn_kernel, pps=pps, n_pages=n_pages, lambda_init=lambda_init),
        grid_spec=grid_spec,
        out_shape=jax.ShapeDtypeStruct((db, rows, A_V_DIM), F32),
        compiler_params=pltpu.CompilerParams(
            dimension_semantics=("arbitrary", "arbitrary"), vmem_limit_bytes=V7X_VMEM_LIMIT),
        name="decode_attn",
    )(pt, per_row(qb), per_row(kb), per_row(vb),
      lq1, lk1, lq2, lk2, gain, *([ck] * pps), *([cv] * pps))
    return out[:, 0::2, :].reshape(db, A_WIDTH)


def _rwkv_prep(x, prev, mu, w0, w2e, a0, a2e, g2, k_k, k_a, r_k, ones):
    m = x + (prev - x) * mu
    r = m[:, 0:R_WIDTH]
    k = m[:, R_WIDTH:2 * R_WIDTH]
    v = m[:, 2 * R_WIDTH:3 * R_WIDTH]
    xwa = m[:, 3 * R_WIDTH:3 * R_WIDTH + R_LORA_W + R_LORA_A]
    xg = m[:, 3 * R_WIDTH + R_LORA_W + R_LORA_A:]
    z = w0 + _dot(jnp.tanh(xwa).astype(BF16), w2e)
    lw = (-math.exp(-0.5)) * _sigmoid(z)
    a = _sigmoid(a0 + _dot(xwa.astype(BF16), a2e))
    g = _dot(_sigmoid(xg).astype(BF16), g2)
    kk = k * k_k
    kk = kk * lax.rsqrt(jnp.maximum(_group_sum(kk * kk, ones), 1e-24))
    k = k * (1.0 + (a - 1.0) * k_a)
    bonus = _group_sum(r * k * r_k, ones) * v
    return r, lw, k, v, kk, a, g, bonus


def _rwkv_finish(y, bonus, g, gn_w, gn_b, ones):
    mu = _group_sum(y, ones) * (1.0 / R_HEAD)
    d = y - mu
    var = _group_sum(d * d, ones) * (1.0 / R_HEAD)
    yn = d * lax.rsqrt(var + GN_EPS) * gn_w + gn_b
    return ((yn + bonus) * g).astype(BF16)


def _lane_halves(x, width):
    lane = _iota(x.shape, 1)
    zero = jnp.zeros_like(x)
    return jnp.concatenate([jnp.where(lane < width, x, zero), jnp.where(lane >= width, x, zero)],
                           axis=0)


def _lane_blocks(x, width):
    block = _iota(x.shape, 1) >> int(math.log2(width))
    zero = jnp.zeros_like(x)
    return jnp.concatenate([jnp.where(block == b, x, zero) for b in range(x.shape[1] // width)],
                           axis=0)


def _rwkv_chunk_kernel(x_ref, *refs, chunk, nb):
    st_ref, prev_ref = refs[-3], refs[-1]

    @pl.when(pl.program_id(1) == 0)
    def _():
        prev_ref[...] = jnp.zeros_like(prev_ref)
        st_ref[...] = jnp.zeros_like(st_ref)

    def body(k, carry):
        _rwkv_one_chunk(jnp.asarray(k, jnp.int32), x_ref, *refs, chunk=chunk, nb=nb)
        return carry

    lax.fori_loop(0, x_ref.shape[1] // chunk, body, 0)


def _rwkv_one_chunk(k, x_ref, mu_ref, w0_ref, w2e_ref, a0_ref, a2e_ref, g2_ref, kk_ref, ka_ref,
                    rk_ref, gnw_ref, gnb_ref, ones_ref,
                    y_ref, st_ref, sh_ref, prev_ref, *, chunk, nb):
    c = chunk
    chunk_rows = pl.ds(pl.multiple_of(k * c, c), c)
    ones = ones_ref[...]
    rows = _iota((c, R_SHIFT_W), 0)
    tri = (_iota((c, c), 0) >= _iota((c, c), 1)).astype(BF16)
    sls = [slice(p * LANES, (p + 1) * LANES) for p in range(R_PAIRS)]
    ah_p, rh_p, v_p, bt_p, kt_p, bb_p, kb_p, pe_p, st_idx, fin = ([] for _ in range(10))
    for bi in range(nb):
        x = x_ref[bi, chunk_rows, :]
        prev = jnp.where(rows == 0, prev_ref[bi], pltpu.roll(x, shift=1, axis=0))
        prev_ref[bi] = x[c - 1:c, :]
        sh_ref[bi] = x[c - 1:c, :]
        r, lw, k, v, kk, a, g, bonus = _rwkv_prep(
            x, prev, mu_ref[...], w0_ref[...], w2e_ref[...], a0_ref[...], a2e_ref[...],
            g2_ref[...], kk_ref[...], ka_ref[...], rk_ref[...], ones)
        fin.append((bonus, g))
        l_hi = lw.astype(BF16)
        l_r1 = lw - l_hi.astype(F32)
        l_mid = l_r1.astype(BF16)
        l_lo = (l_r1 - l_mid.astype(F32)).astype(BF16)
        cum = _dot(tri, l_hi) + _dot(tri, l_mid) + _dot(tri, l_lo)
        cum_end = cum[c - 1:c, :]
        e_neg = jnp.exp(-cum)
        e_end = jnp.exp(cum_end - cum)
        kka = kk * a
        ah = -kk * jnp.exp(cum - lw)
        rh = r * jnp.exp(cum)
        bt = kka * e_neg
        kt = k * e_neg
        bb = kka * e_end
        kb = k * e_end
        p_end = jnp.exp(cum_end)
        for p, sl in enumerate(sls):
            for lst, z in ((ah_p, ah), (rh_p, rh), (v_p, v), (bt_p, bt), (kt_p, kt),
                           (bb_p, bb), (kb_p, kb), (pe_p, p_end)):
                lst.append(z[:, sl])
            st_idx.append((bi, p))

    t_idx = _iota((c, 2 * c), 0)
    s_idx = _iota((c, 2 * c), 1) & (c - 1)
    strict = t_idx > s_idx
    incl = t_idx >= s_idx
    eye_pack = (t_idx == s_idx).astype(F32)
    bd_mask = (_iota((LANES, LANES), 0) >> 6) == (_iota((LANES, LANES), 1) >> 6)
    eye_l = _iota((LANES, LANES), 0) == _iota((LANES, LANES), 1)
    zeros_c = jnp.zeros((c, LANES), F32)

    units = range(len(st_idx))
    v_blk = [_lane_halves(z, R_HEAD) for z in v_p]
    a_all = [_mm(jnp.concatenate([ah_p[u], rh_p[u]], axis=0),
                 jnp.concatenate([_lane_halves(bt_p[u], R_HEAD), _lane_halves(kt_p[u], R_HEAD)],
                                 axis=0), _dot_nt) for u in units]
    a_ab = [jnp.where(strict, z[0:c, 0:2 * c], 0.0) for z in a_all]
    a_ak = [jnp.where(strict, z[0:c, 2 * c:4 * c], 0.0) for z in a_all]
    a_rb = [jnp.where(incl, z[c:2 * c, 0:2 * c], 0.0) for z in a_all]
    a_rk = [jnp.where(incl, z[c:2 * c, 2 * c:4 * c], 0.0) for z in a_all]
    w1 = [_mm(a_ak[u], v_blk[u]) for u in units]
    quads = range(len(st_idx) // 2)
    a4 = [jnp.concatenate([a_ab[2 * q], a_ab[2 * q + 1]], axis=1) for q in quads]
    eye4 = jnp.concatenate([eye_pack, eye_pack], axis=1)
    inv4 = [eye4 + z for z in a4]
    bd = [_lane_blocks(z, c) for z in a4]
    pw = [_mm(a4[q], bd[q]) for q in quads]
    n_sq = int(math.log2(c)) - 1
    for step in range(n_sq):
        bd = [_lane_blocks(z, c) for z in pw]
        if step < n_sq - 1:
            both = [_mm(jnp.concatenate([pw[q], inv4[q]], axis=0), bd[q]) for q in quads]
            pw = [z[0:c] for z in both]
            inv4 = [inv4[q] + both[q][c:2 * c] for q in quads]
        else:
            inv4 = [inv4[q] + _mm(inv4[q], bd[q]) for q in quads]
    inv = [inv4[u // 2][:, (u % 2) * 2 * c:(u % 2 + 1) * 2 * c] for u in units]
    t4 = [_mm(inv[u], jnp.concatenate([_lane_halves(w1[u], R_HEAD),
                                       _lane_halves(ah_p[u], R_HEAD)], axis=1)) for u in units]
    wu = [z[:, 0:LANES] for z in t4]
    ah2 = [z[:, LANES:] for z in t4]
    zeros_blk = jnp.zeros((2 * c, LANES), F32)
    t5 = [_mm(jnp.concatenate([a_rb[u], a_rk[u]], axis=1),
              jnp.concatenate([
                  jnp.concatenate([_lane_halves(wu[u], R_HEAD), _lane_halves(ah2[u], R_HEAD)],
                                  axis=1),
                  jnp.concatenate([v_blk[u], zeros_blk], axis=1)], axis=0)) for u in units]
    t6 = [_mm(jnp.concatenate([bb_p[u], kb_p[u]], axis=0),
              jnp.concatenate([jnp.concatenate([wu[u], ah2[u]], axis=1),
                               jnp.concatenate([v_p[u], zeros_c], axis=1)], axis=0), _dot_tn)
          for u in units]
    st = [st_ref[bi, p] for bi, p in st_idx]
    ys = [_mm(rh_p[u] + t5[u][:, LANES:], st[u]) + t5[u][:, 0:LANES] for u in units]
    for u in units:
        h_new = jnp.where(bd_mask, t6[u][:, 0:LANES], 0.0)
        g_x = jnp.where(bd_mask, t6[u][:, LANES:], 0.0)
        p_col = jnp.sum(jnp.where(eye_l, pe_p[u], 0.0), axis=1, keepdims=True)
        st_ref[st_idx[u]] = p_col * st[u] + _mm(g_x, st[u]) + h_new

    for bi in range(nb):
        y = jnp.concatenate(ys[bi * R_PAIRS:(bi + 1) * R_PAIRS], axis=1)
        bonus, g = fin[bi]
        y_ref[bi, chunk_rows, :] = _rwkv_finish(y, bonus, g, gnw_ref[...], gnb_ref[...], ones)


def _rwkv_prompt(rc3, params, chunk, nb):
    b, t, _ = rc3.shape
    rows = _pick_tile(t, RWKV_CHUNKS_PER_STEP * chunk)
    const = lambda bi, j: (0, 0)
    specs = [pl.BlockSpec(p.shape, const) for p in params]
    return pl.pallas_call(
        functools.partial(_rwkv_chunk_kernel, chunk=chunk, nb=nb),
        grid=(b // nb, t // rows),
        in_specs=[pl.BlockSpec((nb, rows, R_SHIFT_W), lambda bi, j: (bi, j, 0))] + specs,
        out_specs=[
            pl.BlockSpec((nb, rows, R_WIDTH), lambda bi, j: (bi, j, 0)),
            pl.BlockSpec((nb, R_PAIRS, LANES, LANES), lambda bi, j: (bi, 0, 0, 0)),
            pl.BlockSpec((nb, 1, R_SHIFT_W), lambda bi, j: (bi, 0, 0)),
        ],
        out_shape=(
            jax.ShapeDtypeStruct((b, t, R_WIDTH), BF16),
            jax.ShapeDtypeStruct((b, R_PAIRS, LANES, LANES), F32),
            jax.ShapeDtypeStruct((b, 1, R_SHIFT_W), F32),
        ),
        scratch_shapes=[pltpu.VMEM((nb, 1, R_SHIFT_W), F32)],
        compiler_params=pltpu.CompilerParams(
            dimension_semantics=("arbitrary", "arbitrary"), vmem_limit_bytes=V7X_VMEM_LIMIT),
        name="rwkv_chunk",
    )(rc3, *params)


def _rwkv_step_kernel(x_ref, prev_ref, st_in_ref, mu_ref, w0_ref, w2e_ref, a0_ref, a2e_ref,
                      g2_ref, kk_ref, ka_ref, rk_ref, gnw_ref, gnb_ref, ones_ref,
                      y_ref, st_ref, yrow_ref, *, rows):
    ones = ones_ref[...]
    r, lw, k, v, kk, a, g, bonus = _rwkv_prep(
        x_ref[...], prev_ref[...], mu_ref[...], w0_ref[...], w2e_ref[...], a0_ref[...],
        a2e_ref[...], g2_ref[...], kk_ref[...], ka_ref[...], rk_ref[...], ones)
    vecs = (jnp.exp(lw), kk * a, k, v, kk, r)
    row8 = _iota((SUBLANES, R_HEAD), 0)

    def rows8(*row_vecs):
        out = jnp.zeros((SUBLANES, R_HEAD), F32)
        for n, vec in enumerate(row_vecs):
            out = jnp.where(row8 == n, vec, out)
        return out

    def hi_lo(x):
        hi = x.astype(BF16).astype(F32)
        return hi, x - hi

    def vec_mat(vec_row, s):
        return _mm(jnp.broadcast_to(vec_row, (SUBLANES, R_HEAD)), s, _dot_nt)[0:1]

    group = math.gcd(rows, 4)
    for g0 in range(0, rows, group):
        units = [(bi, h) for bi in range(g0, g0 + group) for h in range(R_HEADS)]
        w_r, b_r, k_r, v_r, kk_r, r_r = (
            [z[bi:bi + 1, h * R_HEAD:(h + 1) * R_HEAD] for bi, h in units] for z in vecs)
        ids = range(len(units))
        s_old = [st_in_ref[bi, h] for bi, h in units]
        sa = [-vec_mat(kk_r[u], s_old[u]) for u in ids]
        upd = []
        for u in ids:
            (sa_h, sa_l), (b_h, b_l) = hi_lo(sa[u]), hi_lo(b_r[u])
            (v_h, v_l), (k_h, k_l) = hi_lo(v_r[u]), hi_lo(k_r[u])
            upd.append(_mm(rows8(sa_h, sa_h, sa_l, v_h, v_h, v_l),
                           rows8(b_h, b_l, b_h, k_h, k_l, k_h), _dot_tn))
        s_new = [s_old[u] * w_r[u] + upd[u] for u in ids]
        for u, (bi, h) in enumerate(units):
            st_ref[bi, h] = s_new[u]
        y_r = [vec_mat(r_r[u], s_new[u]) for u in ids]
        for n, bi in enumerate(range(g0, g0 + group)):
            yrow_ref[bi:bi + 1, :] = jnp.concatenate(y_r[n * R_HEADS:(n + 1) * R_HEADS], axis=1)
    y_ref[...] = _rwkv_finish(yrow_ref[...], bonus, g, gnw_ref[...], gnb_ref[...], ones)


def _rwkv_step(rc, shift_prev, state, params, rows):
    db = rc.shape[0]
    const = lambda i: (0, 0)
    specs = [pl.BlockSpec(p.shape, const) for p in params]
    state_spec = pl.BlockSpec((rows, R_HEADS, R_HEAD, R_HEAD), lambda i: (i, 0, 0, 0))
    return pl.pallas_call(
        functools.partial(_rwkv_step_kernel, rows=rows),
        grid=(db // rows,),
        in_specs=[
            pl.BlockSpec((rows, R_SHIFT_W), lambda i: (i, 0)),
            pl.BlockSpec((rows, R_SHIFT_W), lambda i: (i, 0)),
            state_spec,
        ] + specs,
        out_specs=[pl.BlockSpec((rows, R_WIDTH), lambda i: (i, 0)), state_spec],
        out_shape=(
            jax.ShapeDtypeStruct((db, R_WIDTH), BF16),
            jax.ShapeDtypeStruct((db, R_HEADS, R_HEAD, R_HEAD), F32),
        ),
        scratch_shapes=[pltpu.VMEM((rows, R_WIDTH), F32)],
        compiler_params=pltpu.CompilerParams(
            dimension_semantics=("arbitrary",), vmem_limit_bytes=V7X_VMEM_LIMIT),
        name="rwkv_step",
    )(rc, shift_prev, state, *params)


def _pairs_to_state(st):
    b = st.shape[0]
    blocks = st.reshape(b, R_PAIRS, 2, R_HEAD, 2, R_HEAD)
    diag = jnp.stack([blocks[:, :, 0, :, 0, :], blocks[:, :, 1, :, 1, :]], axis=2)
    return jnp.swapaxes(diag.reshape(b, R_HEADS, R_HEAD, R_HEAD), -1, -2)


def _merge_ffn_kernel(x_ref, o_ref, rw_ref, g_ref, wpa_ref, wpb_ref, wout_ref, nf_ref,
                      wg_ref, wu_ref, wd_ref, y_ref, *, d_ff):
    ya = _dot(o_ref[...].astype(BF16), wpa_ref[...])
    yb = _dot(rw_ref[...], wpb_ref[...])
    merged = (g_ref[:, 0:D_MODEL] * ya + g_ref[:, D_MODEL:] * yb).astype(BF16)
    x1 = x_ref[...] + _dot(merged, wout_ref[...])
    hf = _rms_rows(x1, nf_ref[...]).astype(BF16)
    acc = x1
    step = 512
    for c in range(0, d_ff, step):
        n = min(step, d_ff - c)
        gate = _dot(hf, wg_ref[:, c:c + n])
        up = _dot(hf, wu_ref[:, c:c + n])
        act = (gate * _sigmoid(gate) * up).astype(BF16)
        acc = acc + _dot(act, wd_ref[c:c + n, :])
    y_ref[...] = acc


def _merge_ffn(x2, o, rw, gates, wpa, wpb, wout, nf, wg, wu, wd, tm):
    m = x2.shape[0]
    d_ff = wg.shape[1]
    row = lambda i: (i, 0)
    const = lambda i: (0, 0)
    return pl.pallas_call(
        functools.partial(_merge_ffn_kernel, d_ff=d_ff),
        grid=(m // tm,),
        in_specs=[
            pl.BlockSpec((tm, D_MODEL), row),
            pl.BlockSpec((tm, A_WIDTH), row),
            pl.BlockSpec((tm, R_WIDTH), row),
            pl.BlockSpec((tm, 2 * D_MODEL), row),
            pl.BlockSpec(wpa.shape, const),
            pl.BlockSpec(wpb.shape, const),
            pl.BlockSpec(wout.shape, const),
            pl.BlockSpec((1, D_MODEL), const),
            pl.BlockSpec(wg.shape, const),
            pl.BlockSpec(wu.shape, const),
            pl.BlockSpec(wd.shape, const),
        ],
        out_specs=pl.BlockSpec((tm, D_MODEL), row),
        out_shape=jax.ShapeDtypeStruct((m, D_MODEL), F32),
        compiler_params=pltpu.CompilerParams(
            dimension_semantics=("arbitrary",), vmem_limit_bytes=V7X_VMEM_LIMIT),
        name="merge_ffn",
    )(x2, o, rw, gates, wpa, wpb, wout, nf, wg, wu, wd)


def _row(v):
    return v.reshape(1, -1).astype(F32)


def _pick_tile(m, target):
    t = min(m, target)
    while m % t:
        t //= 2
    return t


def kernel(x_prompt, x_sample, cache_k, cache_v, page_table, state_wkv, state_shift, norm_mix, w_in, q_gain, k_gain, lambda_q1, lambda_k1, lambda_q2, lambda_k2, attn_out_gain, w_pa, shift_mu, w0, w2, a0, a2, g2, k_k, k_a, r_k, gn_w, gn_b, w_pb, w_out, norm_ffn, w_gate, w_up, w_down):
    b, t, _ = x_prompt.shape
    db, ds, _ = x_sample.shape
    assert ds == 1, "sample group carries one new token per sequence"
    depth = w_in.shape[0]
    n_pages = page_table.shape[1]
    xp = x_prompt.reshape(b * t, D_MODEL)
    xs = x_sample.reshape(db, D_MODEL)
    head_ones = (jnp.arange(A_WIDTH)[:, None] // A_HEAD_DIM
                 == jnp.arange(A_WIDTH)[None, :] // A_HEAD_DIM).astype(BF16)
    outs = [[] for _ in range(8)]
    tm_p = _pick_tile(b * t, 512)
    tm_s = _pick_tile(db, 512)
    tq = _pick_tile(t, 1024)
    chunk = _pick_tile(t, RWKV_CHUNK)
    chunk_nb = _pick_tile(b, 4)
    pps = _pick_tile(n_pages, 32)
    step_rows = _pick_tile(db, 16)
    for l in range(depth):
        lambda_init = 0.8 - 0.6 * math.exp(-0.3 * l)
        w_in_l = w_in[l]
        qg = _row(jnp.tile(q_gain[l], A_WIDTH // A_HEAD_DIM))
        kg = _row(jnp.tile(k_gain[l], A_WIDTH // A_HEAD_DIM))
        nm = _row(norm_mix[l])
        lam_vecs = (_row(lambda_q1[l]), _row(lambda_k1[l]), _row(lambda_q2[l]), _row(lambda_k2[l]))
        og = _row(attn_out_gain[l])
        zeros_lora = jnp.zeros((R_LORA_W, R_WIDTH), F32)
        rwkv_params = (
            _row(shift_mu[l]), _row(w0[l]),
            jnp.concatenate([w2[l], zeros_lora], axis=0).astype(BF16),
            _row(a0[l]),
            jnp.concatenate([zeros_lora, a2[l]], axis=0).astype(BF16),
            g2[l].astype(BF16), _row(k_k[l]), _row(k_a[l]), _row(r_k[l]),
            _row(gn_w[l]), _row(gn_b[l]), head_ones)
        ffn_w = (w_pa[l].astype(BF16), w_pb[l].astype(BF16), w_out[l].astype(BF16),
                 _row(norm_ffn[l]), w_gate[l].astype(BF16), w_up[l].astype(BF16),
                 w_down[l].astype(BF16))

        qb, k32, kb, v32, vb, rc, gates = _proj(xp, nm, w_in_l, qg, kg, head_ones, tm_p)
        o = _prompt_attn(qb, kb, vb, *lam_vecs, og, b, t, tq, lambda_init)
        rw, st_p, sh_p = _rwkv_prompt(rc.reshape(b, t, R_SHIFT_W), rwkv_params, chunk, chunk_nb)
        xp = _merge_ffn(xp, o, rw.reshape(b * t, R_WIDTH), gates, *ffn_w, tm_p)
        outs[0].append(k32.reshape(b, t, A_HEADS, A_QK_DIM))
        outs[1].append(v32.reshape(b, t, A_HEADS, A_V_DIM))
        outs[2].append(_pairs_to_state(st_p))
        outs[3].append(sh_p.reshape(b, R_SHIFT_W))

        qb, k32, kb, v32, vb, rc, gates = _proj(xs, nm, w_in_l, qg, kg, head_ones, tm_s)
        o = _decode_attn(page_table, qb, kb, vb, cache_k, cache_v, l, *lam_vecs, og,
                         pps, lambda_init)
        rw, st_s = _rwkv_step(rc, state_shift[l], state_wkv[l], rwkv_params,
                              step_rows)
        xs = _merge_ffn(xs, o, rw, gates, *ffn_w, tm_s)
        outs[4].append(k32.reshape(db, 1, A_HEADS, A_QK_DIM))
        outs[5].append(v32.reshape(db, 1, A_HEADS, A_V_DIM))
        outs[6].append(st_s)
        outs[7].append(rc)
    return (xp.reshape(b, t, D_MODEL), xs.reshape(db, 1, D_MODEL),
            *(jnp.stack(o) for o in outs))
```

```python
import functools
import math

import jax
import jax.numpy as jnp
from jax import lax
from jax.experimental import pallas as pl
from jax.experimental.pallas import tpu as pltpu

F32 = jnp.float32
BF16 = jnp.bfloat16

D_MODEL = 1024
PAGE_SIZE = 128
A_HEADS = 4
A_HEAD_DIM = 64
A_QK_DIM = 2 * A_HEAD_DIM
A_V_DIM = 2 * A_HEAD_DIM
A_WIDTH = A_HEADS * A_V_DIM
R_HEAD = 64
R_HEADS = 8
R_WIDTH = R_HEADS * R_HEAD
R_LORA_W = 64
R_LORA_A = 64
R_LORA_G = 128
R_SHIFT_W = 3 * R_WIDTH + R_LORA_W + R_LORA_A + R_LORA_G
OFF_Q = 0
OFF_K = OFF_Q + A_HEADS * A_QK_DIM
OFF_V = OFF_K + A_HEADS * A_QK_DIM
OFF_R = OFF_V + A_WIDTH
OFF_G = OFF_R + R_SHIFT_W
IN_WIDTH = OFF_G + 2 * D_MODEL
NORM_EPS = 1e-6
GN_EPS = 64e-5
NEG_BIG = -1e30

V7X_VMEM_LIMIT = 56 * 1024 * 1024
LANES = 128
SUBLANES = 8
R_PAIRS = R_WIDTH // LANES
RWKV_CHUNK = 64
RWKV_CHUNKS_PER_STEP = 4


def _dot(a, b):
    return jnp.dot(a, b, preferred_element_type=F32)


def _dot_nt(a, b):
    return lax.dot_general(a, b, (((1,), (1,)), ((), ())), preferred_element_type=F32)


def _dot_tn(a, b):
    return lax.dot_general(a, b, (((0,), (0,)), ((), ())), preferred_element_type=F32)


def _mm(a, b, dot=_dot):
    return dot(a.astype(BF16), b.astype(BF16))


def _sigmoid(x):
    return 1.0 / (1.0 + jnp.exp(-x))


def _iota(shape, dim):
    return lax.broadcasted_iota(jnp.int32, shape, dim)


def _group_sum(z, ones_bf):
    return _dot(z.astype(BF16), ones_bf)


def _rms_rows(x, gain_row):
    ms = jnp.mean(x * x, axis=-1, keepdims=True)
    return x * lax.rsqrt(ms + NORM_EPS) * gain_row


def _proj_kernel(x_ref, nm_ref, w_ref, qg_ref, kg_ref, ones_ref,
                 q_ref, k_ref, kb_ref, v_ref, vb_ref, r_ref, g_ref):
    h = _rms_rows(x_ref[...], nm_ref[...]).astype(BF16)
    ones = ones_ref[...]

    def head_norm(z, gain_row):
        ss = _group_sum(z * z, ones)
        return z * lax.rsqrt(ss * (1.0 / A_HEAD_DIM) + NORM_EPS) * gain_row

    def w(lo, hi):
        return w_ref[:, lo:hi].astype(BF16)

    q = head_norm(_dot(h, w(OFF_Q, OFF_K)), qg_ref[...])
    q_ref[...] = (q * (A_HEAD_DIM ** -0.5)).astype(BF16)
    tm = x_ref.shape[0]
    k = head_norm(_dot(h, w(OFF_K, OFF_V)), kg_ref[...])
    kb_ref[...] = k.astype(BF16)
    v = _dot(h, w(OFF_V, OFF_R))
    vb_ref[...] = v.astype(BF16)
    for hd in range(A_HEADS):
        sl = slice(hd * A_QK_DIM, (hd + 1) * A_QK_DIM)
        k_ref[pl.ds(hd, tm, stride=A_HEADS), :] = k[:, sl]
        v_ref[pl.ds(hd, tm, stride=A_HEADS), :] = v[:, sl]
    step = 512
    for c in range(0, R_SHIFT_W, step):
        n = min(step, R_SHIFT_W - c)
        r_ref[:, c:c + n] = _dot(h, w(OFF_R + c, OFF_R + c + n))
    for c in range(0, 2 * D_MODEL, step):
        g_ref[:, c:c + step] = _sigmoid(_dot(h, w(OFF_G + c, OFF_G + c + step)))


def _proj(x2, nm, w_in_l, qg, kg, ones_bf, tm):
    m = x2.shape[0]
    row = lambda i: (i, 0)
    const = lambda i: (0, 0)
    out_shape = (
        jax.ShapeDtypeStruct((m, A_WIDTH), BF16),
        jax.ShapeDtypeStruct((m * A_HEADS, A_QK_DIM), F32),
        jax.ShapeDtypeStruct((m, A_WIDTH), BF16),
        jax.ShapeDtypeStruct((m * A_HEADS, A_V_DIM), F32),
        jax.ShapeDtypeStruct((m, A_WIDTH), BF16),
        jax.ShapeDtypeStruct((m, R_SHIFT_W), F32),
        jax.ShapeDtypeStruct((m, 2 * D_MODEL), F32),
    )
    return pl.pallas_call(
        _proj_kernel,
        grid=(m // tm,),
        in_specs=[
            pl.BlockSpec((tm, D_MODEL), row),
            pl.BlockSpec((1, D_MODEL), const),
            pl.BlockSpec((D_MODEL, IN_WIDTH), const),
            pl.BlockSpec((1, A_WIDTH), const),
            pl.BlockSpec((1, A_WIDTH), const),
            pl.BlockSpec((A_WIDTH, A_WIDTH), const),
        ],
        out_specs=[
            pl.BlockSpec((tm, A_WIDTH), row),
            pl.BlockSpec((tm * A_HEADS, A_QK_DIM), row),
            pl.BlockSpec((tm, A_WIDTH), row),
            pl.BlockSpec((tm * A_HEADS, A_V_DIM), row),
            pl.BlockSpec((tm, A_WIDTH), row),
            pl.BlockSpec((tm, R_SHIFT_W), row),
            pl.BlockSpec((tm, 2 * D_MODEL), row),
        ],
        out_shape=out_shape,
        compiler_params=pltpu.CompilerParams(
            dimension_semantics=("arbitrary",), vmem_limit_bytes=V7X_VMEM_LIMIT),
        name="proj",
    )(x2, nm, w_in_l, qg, kg, ones_bf)


def _lambda_full(lq1, lk1, lq2, lk2, lambda_init):
    s1 = jnp.sum(lq1 * lk1, axis=-1, keepdims=True)
    s2 = jnp.sum(lq2 * lk2, axis=-1, keepdims=True)
    return jnp.exp(s1) - jnp.exp(s2) + lambda_init


ATTN_ACC_ROWS = A_V_DIM + 16
ATTN_HEADS_PER_STEP = 1


def _alibi_lanes(pos, slope, keys):
    l4 = _iota(pos.shape, 1) & (A_HEAD_DIM - 1)
    lo = slope * (pos & 255).astype(F32)
    hi = slope * ((pos >> 8) << 8).astype(F32)
    if keys:
        val = jnp.where(l4 < 2, 1.0, jnp.where(l4 == 2, lo, jnp.where(l4 == 3, hi, 0.0)))
    else:
        val = jnp.where(l4 == 0, -lo, jnp.where(l4 == 1, -hi, jnp.where(l4 < 4, 1.0, 0.0)))
    return val.astype(BF16)


def _prompt_attn_kernel(q_ref, k_ref, v_ref, lq1_ref, lk1_ref, lq2_ref, lk2_ref, gain_ref,
                        o_ref, kc_ref, vt_ref, acc_ref, *, tq, lambda_init):
    hg = pl.program_id(1)
    t = k_ref.shape[0]
    hp = ATTN_HEADS_PER_STEP
    n_st = 2 * hp
    slopes = [jnp.exp2(jnp.full((1, 1), -8.0 / A_HEADS, F32) * (hg * hp + hh + 1).astype(F32))
              for hh in range(hp)]
    head_lanes = [slice(hh * A_QK_DIM, (hh + 1) * A_QK_DIM) for hh in range(hp)]
    pos = _iota((tq, A_QK_DIM), 0)
    low = _iota((tq, A_QK_DIM), 1) < A_HEAD_DIM

    ones_rows = (_iota((ATTN_ACC_ROWS - A_V_DIM, tq), 0) == 0).astype(BF16)
    for hh in range(hp):
        ek = _alibi_lanes(pos, slopes[hh], keys=True)
        for jc in range(t // tq):
            rows = slice(jc * tq, (jc + 1) * tq)
            kt = k_ref[rows, head_lanes[hh]]
            kc_ref[2 * hh, rows, :] = jnp.where(low, kt, ek)
            kc_ref[2 * hh + 1, rows, :] = jnp.where(low, ek, kt)
            vt_ref[hh, jc, 0:A_V_DIM, :] = v_ref[rows, head_lanes[hh]].astype(F32).T.astype(BF16)
            vt_ref[hh, jc, A_V_DIM:, :] = ones_rows

    eqs = [_alibi_lanes(pos, slopes[hh], keys=False) for hh in range(hp)]
    lam = _lambda_full(lq1_ref[...], lk1_ref[...], lq2_ref[...], lk2_ref[...], lambda_init)
    half = tq // 2

    def q_tile(i, carry):
        i = jnp.asarray(i, jnp.int32)
        q_rows = pl.ds(pl.multiple_of(i * tq, tq), tq)
        qc = []
        for hh in range(hp):
            q = q_ref[q_rows, head_lanes[hh]]
            qc += [jnp.where(low, q, eqs[hh]), jnp.where(low, eqs[hh], q)]

        def scores(j, st, masked):
            base = pl.multiple_of(jnp.asarray(j * tq, jnp.int32), tq)
            if not masked:
                return [(tq, _dot_nt(kc_ref[st, pl.ds(base, tq), :], qc[st]))]
            lo = _dot_nt(kc_ref[st, pl.ds(base, half), :], qc[st][0:half])
            hi = _dot_nt(kc_ref[st, pl.ds(base, tq), :], qc[st][half:tq])
            lo = jnp.where(_iota((half, half), 0) <= _iota((half, half), 1), lo, NEG_BIG)
            hi = jnp.where(_iota((tq, half), 0) <= _iota((tq, half), 1) + half, hi, NEG_BIG)
            return [(half, lo), (tq, hi)]

        def tiles(group, ms):
            s = [[scores(j, st, masked) for st in range(n_st)] for j, masked in group]
            ms = list(ms)
            acc = [acc_ref[st] for st in range(n_st)]
            for (j, _), sj in zip(group, s):
                for st in range(n_st):
                    off = slopes[st // 2] * jnp.asarray((j - i) * tq, F32)
                    m_out, acc_out, c0 = [], [], 0
                    for n_kv, blk in sj[st]:
                        cols = slice(c0, c0 + blk.shape[1])
                        m_rel = ms[st][:, cols] - off
                        m_new = jnp.maximum(m_rel, jnp.max(blk, axis=0, keepdims=True))
                        p = jnp.exp(blk - m_new).astype(BF16)
                        acc_out.append(jnp.exp(m_rel - m_new) * acc[st][:, cols]
                                       + _dot(vt_ref[st // 2, j, :, 0:n_kv], p))
                        m_out.append(m_new + off)
                        c0 += blk.shape[1]
                    ms[st] = m_out[0] if len(m_out) == 1 else jnp.concatenate(m_out, axis=1)
                    acc[st] = acc_out[0] if len(acc_out) == 1 else jnp.concatenate(acc_out, axis=1)
            for st in range(n_st):
                acc_ref[st] = acc[st]
            return tuple(ms)

        acc_ref[...] = jnp.zeros_like(acc_ref)
        ms = lax.fori_loop(0, i // 2,
                           lambda pr, ms: tiles([(2 * pr, False), (2 * pr + 1, False)], ms),
                           tuple(jnp.full((1, tq), NEG_BIG, F32) for _ in range(n_st)))
        lax.cond((i & 1) == 1,
                 lambda ms: tiles([(i - 1, False), (i, True)], ms),
                 lambda ms: tiles([(i, True)], ms), ms)
        for hh in range(hp):
            a0, a1 = acc_ref[2 * hh], acc_ref[2 * hh + 1]
            o_t = (a0[0:A_V_DIM] / a0[A_V_DIM:A_V_DIM + 1]
                   - lam * (a1[0:A_V_DIM] / a1[A_V_DIM:A_V_DIM + 1]))
            ms_o = jnp.mean(o_t * o_t, axis=0, keepdims=True)
            o_t = o_t * lax.rsqrt(ms_o + NORM_EPS) * gain_ref[...] * (1.0 - lambda_init)
            o_ref[q_rows, head_lanes[hh]] = o_t.T.astype(BF16)
        return carry

    lax.fori_loop(0, t // tq, q_tile, 0)


def _prompt_attn(qb, kb, vb, lq1, lk1, lq2, lk2, gain, b, t, tq, lambda_init):
    nq = t // tq
    hp = ATTN_HEADS_PER_STEP
    vec = lambda bi, h: (0, 0)
    seq_head = lambda bi, h: (bi, h)
    return pl.pallas_call(
        functools.partial(_prompt_attn_kernel, tq=tq, lambda_init=lambda_init),
        grid=(b, A_HEADS // hp),
        in_specs=[
            pl.BlockSpec((t, hp * A_QK_DIM), seq_head),
            pl.BlockSpec((t, hp * A_QK_DIM), seq_head),
            pl.BlockSpec((t, hp * A_V_DIM), seq_head),
            pl.BlockSpec((1, A_HEAD_DIM), vec),
            pl.BlockSpec((1, A_HEAD_DIM), vec),
            pl.BlockSpec((1, A_HEAD_DIM), vec),
            pl.BlockSpec((1, A_HEAD_DIM), vec),
            pl.BlockSpec((A_V_DIM, 1), vec),
        ],
        out_specs=pl.BlockSpec((t, hp * A_V_DIM), seq_head),
        out_shape=jax.ShapeDtypeStruct((b * t, A_WIDTH), BF16),
        scratch_shapes=[pltpu.VMEM((2 * hp, t, A_QK_DIM), BF16),
                        pltpu.VMEM((hp, nq, ATTN_ACC_ROWS, tq), BF16),
                        pltpu.VMEM((2 * hp, ATTN_ACC_ROWS, tq), F32)],
        compiler_params=pltpu.CompilerParams(
            dimension_semantics=("arbitrary", "arbitrary"),
            vmem_limit_bytes=V7X_VMEM_LIMIT),
        name="prompt_attn",
    )(qb, kb, vb, lq1, lk1, lq2, lk2, gain.reshape(A_V_DIM, 1))


def _decode_attn_kernel(pt_ref, q_ref, kn_ref, vn_ref, lq1_ref, lk1_ref, lq2_ref, lk2_ref,
                        gain_ref, *refs, pps, n_pages, lambda_init):
    k_refs = refs[:pps]
    v_refs = refs[pps:2 * pps]
    o_ref = refs[2 * pps]
    m_ref, l_ref, acc_ref = refs[2 * pps + 1:]
    j = pl.program_id(1)
    rows = 2 * A_HEADS
    page_rows = PAGE_SIZE * A_HEADS
    row_q = _iota((rows, A_QK_DIM), 0)
    lane_q = _iota((rows, A_QK_DIM), 1)
    qm = jnp.where((lane_q >> 6) == (row_q & 1), q_ref[0], 0.0).astype(BF16)
    slope = jnp.exp2((-8.0 / A_HEADS) * ((_iota((rows, 1), 0) >> 1) + 1).astype(F32))
    col = _iota((rows, pps * page_rows), 1)
    own_head = (col & (A_HEADS - 1)) == (_iota((rows, pps * page_rows), 0) >> 1)
    dist = (n_pages - j * pps) * PAGE_SIZE - (col >> 2)

    @pl.when(j == 0)
    def _():
        m_ref[...] = jnp.full((rows, 1), NEG_BIG, F32)
        l_ref[...] = jnp.zeros((rows, 1), F32)
        acc_ref[...] = jnp.zeros((rows, A_V_DIM), F32)

    s = jnp.concatenate([_dot_nt(qm, k_refs[u][...].astype(BF16)) for u in range(pps)], axis=1)
    s = jnp.where(own_head, s - slope * dist.astype(F32), NEG_BIG)
    m = m_ref[...]
    m_new = jnp.maximum(m, jnp.max(s, axis=-1, keepdims=True))
    alpha = jnp.exp(m - m_new)
    p = jnp.exp(s - m_new)
    pb = p.astype(BF16)
    pv = _dot(pb[:, 0:page_rows], v_refs[0][...].astype(BF16))
    for u in range(1, pps):
        pv = pv + _dot(pb[:, u * page_rows:(u + 1) * page_rows], v_refs[u][...].astype(BF16))
    m = m_new
    l = alpha * l_ref[...] + jnp.sum(p, axis=-1, keepdims=True)
    acc = alpha * acc_ref[...] + pv
    m_ref[...], l_ref[...], acc_ref[...] = m, l, acc

    @pl.when(j == pl.num_programs(1) - 1)
    def _():
        s_new = jnp.sum(qm.astype(F32) * kn_ref[0], axis=-1, keepdims=True)
        m_new = jnp.maximum(m, s_new)
        alpha = jnp.exp(m - m_new)
        p_new = jnp.exp(s_new - m_new)
        l_fin = alpha * l + p_new
        acc_fin = alpha * acc + p_new.astype(BF16).astype(F32) * vn_ref[0]
        lam = _lambda_full(lq1_ref[...], lk1_ref[...], lq2_ref[...], lk2_ref[...], lambda_init)
        comp = _iota((rows, 1), 0) & 1
        x = acc_fin * (jnp.where(comp == 0, 1.0, -lam) / l_fin)
        o = x + pltpu.roll(x, shift=rows - 1, axis=0)
        o_ref[0] = _rms_rows(o, gain_ref[...]) * (1.0 - lambda_init)


def _decode_attn(page_table, qb, kb, vb, cache_k, cache_v, layer, lq1, lk1, lq2, lk2, gain,
                 pps, lambda_init):
    db, n_pages = page_table.shape
    n_pool = cache_k.shape[1]
    page_rows = PAGE_SIZE * A_HEADS
    ck = cache_k.reshape(-1, A_QK_DIM)
    cv = cache_v.reshape(-1, A_V_DIM)
    pt = page_table.reshape(db * n_pages)
    rows = 2 * A_HEADS
    tok3 = lambda b, j, pt_ref: (b, 0, 0)
    vec = lambda b, j, pt_ref: (0, 0)

    def page_spec(u):
        return pl.BlockSpec(
            (page_rows, A_QK_DIM),
            lambda b, j, pt_ref: (layer * n_pool + pt_ref[b * n_pages + j * pps + u], 0))

    def per_row(z):
        return jnp.repeat(z.astype(F32).reshape(db, A_HEADS, A_QK_DIM), 2, axis=1)

    grid_spec = pltpu.PrefetchScalarGridSpec(
        num_scalar_prefetch=1,
        grid=(db, n_pages // pps),
        in_specs=[
            pl.BlockSpec((1, rows, A_QK_DIM), tok3),
            pl.BlockSpec((1, rows, A_QK_DIM), tok3),
            pl.BlockSpec((1, rows, A_V_DIM), tok3),
            pl.BlockSpec((1, A_HEAD_DIM), vec),
            pl.BlockSpec((1, A_HEAD_DIM), vec),
            pl.BlockSpec((1, A_HEAD_DIM), vec),
            pl.BlockSpec((1, A_HEAD_DIM), vec),
            pl.BlockSpec((1, A_V_DIM), vec),
        ] + [page_spec(u) for u in range(pps)] + [page_spec(u) for u in range(pps)],
        out_specs=pl.BlockSpec((1, rows, A_V_DIM), tok3),
        scratch_shapes=[pltpu.VMEM((rows, 1), F32), pltpu.VMEM((rows, 1), F32),
                        pltpu.VMEM((rows, A_V_DIM), F32)],
    )
    out = pl.pallas_call(
        functools.partial(_decode_attn_kernel, pps=pps, n_pages=n_pages, lambda_init=lambda_init),
        grid_spec=grid_spec,
        out_shape=jax.ShapeDtypeStruct((db, rows, A_V_DIM), F32),
        compiler_params=pltpu.CompilerParams(
            dimension_semantics=("arbitrary", "arbitrary"), vmem_limit_bytes=V7X_VMEM_LIMIT),
        name="decode_attn",
    )(pt, per_row(qb), per_row(kb), per_row(vb),
      lq1, lk1, lq2, lk2, gain, *([ck] * pps), *([cv] * pps))
    return out[:, 0::2, :].reshape(db, A_WIDTH)


def _rwkv_prep(x, prev, mu, w0, w2e, a0, a2e, g2, k_k, k_a, r_k, ones):
    m = x + (prev - x) * mu
    r = m[:, 0:R_WIDTH]
    k = m[:, R_WIDTH:2 * R_WIDTH]
    v = m[:, 2 * R_WIDTH:3 * R_WIDTH]
    xwa = m[:, 3 * R_WIDTH:3 * R_WIDTH + R_LORA_W + R_LORA_A]
    xg = m[:, 3 * R_WIDTH + R_LORA_W + R_LORA_A:]
    z = w0 + _dot(jnp.tanh(xwa).astype(BF16), w2e)
    lw = (-math.exp(-0.5)) * _sigmoid(z)
    a = _sigmoid(a0 + _dot(xwa.astype(BF16), a2e))
    g = _dot(_sigmoid(xg).astype(BF16), g2)
    kk = k * k_k
    kk = kk * lax.rsqrt(jnp.maximum(_group_sum(kk * kk, ones), 1e-24))
    k = k * (1.0 + (a - 1.0) * k_a)
    bonus = _group_sum(r * k * r_k, ones) * v
    return r, lw, k, v, kk, a, g, bonus


def _rwkv_finish(y, bonus, g, gn_w, gn_b, ones):
    mu = _group_sum(y, ones) * (1.0 / R_HEAD)
    d = y - mu
    var = _group_sum(d * d, ones) * (1.0 / R_HEAD)
    yn = d * lax.rsqrt(var + GN_EPS) * gn_w + gn_b
    return ((yn + bonus) * g).astype(BF16)


def _lane_halves(x, width):
    lane = _iota(x.shape, 1)
    zero = jnp.zeros_like(x)
    return jnp.concatenate([jnp.where(lane < width, x, zero), jnp.where(lane >= width, x, zero)],
                           axis=0)


def _lane_blocks(x, width):
    block = _iota(x.shape, 1) >> int(math.log2(width))
    zero = jnp.zeros_like(x)
    return jnp.concatenate([jnp.where(block == b, x, zero) for b in range(x.shape[1] // width)],
                           axis=0)


def _rwkv_chunk_kernel(x_ref, *refs, chunk, nb):
    st_ref, prev_ref = refs[-3], refs[-1]

    @pl.when(pl.program_id(1) == 0)
    def _():
        prev_ref[...] = jnp.zeros_like(prev_ref)
        st_ref[...] = jnp.zeros_like(st_ref)

    def body(k, carry):
        _rwkv_one_chunk(jnp.asarray(k, jnp.int32), x_ref, *refs, chunk=chunk, nb=nb)
        return carry

    lax.fori_loop(0, x_ref.shape[1] // chunk, body, 0)


def _rwkv_one_chunk(k, x_ref, mu_ref, w0_ref, w2e_ref, a0_ref, a2e_ref, g2_ref, kk_ref, ka_ref,
                    rk_ref, gnw_ref, gnb_ref, ones_ref,
                    y_ref, st_ref, sh_ref, prev_ref, *, chunk, nb):
    c = chunk
    chunk_rows = pl.ds(pl.multiple_of(k * c, c), c)
    ones = ones_ref[...]
    rows = _iota((c, R_SHIFT_W), 0)
    tri = (_iota((c, c), 0) >= _iota((c, c), 1)).astype(BF16)
    sls = [slice(p * LANES, (p + 1) * LANES) for p in range(R_PAIRS)]
    ah_p, rh_p, v_p, bt_p, kt_p, bb_p, kb_p, pe_p, st_idx, fin = ([] for _ in range(10))
    for bi in range(nb):
        x = x_ref[bi, chunk_rows, :]
        prev = jnp.where(rows == 0, prev_ref[bi], pltpu.roll(x, shift=1, axis=0))
        prev_ref[bi] = x[c - 1:c, :]
        sh_ref[bi] = x[c - 1:c, :]
        r, lw, k, v, kk, a, g, bonus = _rwkv_prep(
            x, prev, mu_ref[...], w0_ref[...], w2e_ref[...], a0_ref[...], a2e_ref[...],
            g2_ref[...], kk_ref[...], ka_ref[...], rk_ref[...], ones)
        fin.append((bonus, g))
        l_hi = lw.astype(BF16)
        l_r1 = lw - l_hi.astype(F32)
        l_mid = l_r1.astype(BF16)
        l_lo = (l_r1 - l_mid.astype(F32)).astype(BF16)
        cum = _dot(tri, l_hi) + _dot(tri, l_mid) + _dot(tri, l_lo)
        cum_end = cum[c - 1:c, :]
        e_neg = jnp.exp(-cum)
        e_end = jnp.exp(cum_end - cum)
        kka = kk * a
        ah = -kk * jnp.exp(cum - lw)
        rh = r * jnp.exp(cum)
        bt = kka * e_neg
        kt = k * e_neg
        bb = kka * e_end
        kb = k * e_end
        p_end = jnp.exp(cum_end)
        for p, sl in enumerate(sls):
            for lst, z in ((ah_p, ah), (rh_p, rh), (v_p, v), (bt_p, bt), (kt_p, kt),
                           (bb_p, bb), (kb_p, kb), (pe_p, p_end)):
                lst.append(z[:, sl])
            st_idx.append((bi, p))

    t_idx = _iota((c, 2 * c), 0)
    s_idx = _iota((c, 2 * c), 1) & (c - 1)
    strict = t_idx > s_idx
    incl = t_idx >= s_idx
    eye_pack = (t_idx == s_idx).astype(F32)
    bd_mask = (_iota((LANES, LANES), 0) >> 6) == (_iota((LANES, LANES), 1) >> 6)
    eye_l = _iota((LANES, LANES), 0) == _iota((LANES, LANES), 1)
    zeros_c = jnp.zeros((c, LANES), F32)

    units = range(len(st_idx))
    v_blk = [_lane_halves(z, R_HEAD) for z in v_p]
    a_all = [_mm(jnp.concatenate([ah_p[u], rh_p[u]], axis=0),
                 jnp.concatenate([_lane_halves(bt_p[u], R_HEAD), _lane_halves(kt_p[u], R_HEAD)],
                                 axis=0), _dot_nt) for u in units]
    a_ab = [jnp.where(strict, z[0:c, 0:2 * c], 0.0) for z in a_all]
    a_ak = [jnp.where(strict, z[0:c, 2 * c:4 * c], 0.0) for z in a_all]
    a_rb = [jnp.where(incl, z[c:2 * c, 0:2 * c], 0.0) for z in a_all]
    a_rk = [jnp.where(incl, z[c:2 * c, 2 * c:4 * c], 0.0) for z in a_all]
    w1 = [_mm(a_ak[u], v_blk[u]) for u in units]
    quads = range(len(st_idx) // 2)
    a4 = [jnp.concatenate([a_ab[2 * q], a_ab[2 * q + 1]], axis=1) for q in quads]
    eye4 = jnp.concatenate([eye_pack, eye_pack], axis=1)
    inv4 = [eye4 + z for z in a4]
    bd = [_lane_blocks(z, c) for z in a4]
    pw = [_mm(a4[q], bd[q]) for q in quads]
    n_sq = int(math.log2(c)) - 1
    for step in range(n_sq):
        bd = [_lane_blocks(z, c) for z in pw]
        if step < n_sq - 1:
            both = [_mm(jnp.concatenate([pw[q], inv4[q]], axis=0), bd[q]) for q in quads]
            pw = [z[0:c] for z in both]
            inv4 = [inv4[q] + both[q][c:2 * c] for q in quads]
        else:
            inv4 = [inv4[q] + _mm(inv4[q], bd[q]) for q in quads]
    inv = [inv4[u // 2][:, (u % 2) * 2 * c:(u % 2 + 1) * 2 * c] for u in units]
    t4 = [_mm(inv[u], jnp.concatenate([_lane_halves(w1[u], R_HEAD),
                                       _lane_halves(ah_p[u], R_HEAD)], axis=1)) for u in units]
    wu = [z[:, 0:LANES] for z in t4]
    ah2 = [z[:, LANES:] for z in t4]
    zeros_blk = jnp.zeros((2 * c, LANES), F32)
    t5 = [_mm(jnp.concatenate([a_rb[u], a_rk[u]], axis=1),
              jnp.concatenate([
                  jnp.concatenate([_lane_halves(wu[u], R_HEAD), _lane_halves(ah2[u], R_HEAD)],
                                  axis=1),
                  jnp.concatenate([v_blk[u], zeros_blk], axis=1)], axis=0)) for u in units]
    t6 = [_mm(jnp.concatenate([bb_p[u], kb_p[u]], axis=0),
              jnp.concatenate([jnp.concatenate([wu[u], ah2[u]], axis=1),
                               jnp.concatenate([v_p[u], zeros_c], axis=1)], axis=0), _dot_tn)
          for u in units]
    st = [st_ref[bi, p] for bi, p in st_idx]
    ys = [_mm(rh_p[u] + t5[u][:, LANES:], st[u]) + t5[u][:, 0:LANES] for u in units]
    for u in units:
        h_new = jnp.where(bd_mask, t6[u][:, 0:LANES], 0.0)
        g_x = jnp.where(bd_mask, t6[u][:, LANES:], 0.0)
        p_col = jnp.sum(jnp.where(eye_l, pe_p[u], 0.0), axis=1, keepdims=True)
        st_ref[st_idx[u]] = p_col * st[u] + _mm(g_x, st[u]) + h_new

    for bi in range(nb):
        y = jnp.concatenate(ys[bi * R_PAIRS:(bi + 1) * R_PAIRS], axis=1)
        bonus, g = fin[bi]
        y_ref[bi, chunk_rows, :] = _rwkv_finish(y, bonus, g, gnw_ref[...], gnb_ref[...], ones)


def _rwkv_prompt(rc3, params, chunk, nb):
    b, t, _ = rc3.shape
    rows = _pick_tile(t, RWKV_CHUNKS_PER_STEP * chunk)
    const = lambda bi, j: (0, 0)
    specs = [pl.BlockSpec(p.shape, const) for p in params]
    return pl.pallas_call(
        functools.partial(_rwkv_chunk_kernel, chunk=chunk, nb=nb),
        grid=(b // nb, t // rows),
        in_specs=[pl.BlockSpec((nb, rows, R_SHIFT_W), lambda bi, j: (bi, j, 0))] + specs,
        out_specs=[
            pl.BlockSpec((nb, rows, R_WIDTH), lambda bi, j: (bi, j, 0)),
            pl.BlockSpec((nb, R_PAIRS, LANES, LANES), lambda bi, j: (bi, 0, 0, 0)),
            pl.BlockSpec((nb, 1, R_SHIFT_W), lambda bi, j: (bi, 0, 0)),
        ],
        out_shape=(
            jax.ShapeDtypeStruct((b, t, R_WIDTH), BF16),
            jax.ShapeDtypeStruct((b, R_PAIRS, LANES, LANES), F32),
            jax.ShapeDtypeStruct((b, 1, R_SHIFT_W), F32),
        ),
        scratch_shapes=[pltpu.VMEM((nb, 1, R_SHIFT_W), F32)],
        compiler_params=pltpu.CompilerParams(
            dimension_semantics=("arbitrary", "arbitrary"), vmem_limit_bytes=V7X_VMEM_LIMIT),
        name="rwkv_chunk",
    )(rc3, *params)


def _rwkv_step_kernel(x_ref, prev_ref, st_in_ref, mu_ref, w0_ref, w2e_ref, a0_ref, a2e_ref,
                      g2_ref, kk_ref, ka_ref, rk_ref, gnw_ref, gnb_ref, ones_ref,
                      y_ref, st_ref, yrow_ref, *, rows):
    ones = ones_ref[...]
    r, lw, k, v, kk, a, g, bonus = _rwkv_prep(
        x_ref[...], prev_ref[...], mu_ref[...], w0_ref[...], w2e_ref[...], a0_ref[...],
        a2e_ref[...], g2_ref[...], kk_ref[...], ka_ref[...], rk_ref[...], ones)
    vecs = (jnp.exp(lw), kk * a, k, v, kk, r)
    row8 = _iota((SUBLANES, R_HEAD), 0)

    def rows8(*row_vecs):
        out = jnp.zeros((SUBLANES, R_HEAD), F32)
        for n, vec in enumerate(row_vecs):
            out = jnp.where(row8 == n, vec, out)
        return out

    def hi_lo(x):
        hi = x.astype(BF16).astype(F32)
        return hi, x - hi

    def vec_mat(vec_row, s):
        return _mm(jnp.broadcast_to(vec_row, (SUBLANES, R_HEAD)), s, _dot_nt)[0:1]

    group = math.gcd(rows, 4)
    for g0 in range(0, rows, group):
        units = [(bi, h) for bi in range(g0, g0 + group) for h in range(R_HEADS)]
        w_r, b_r, k_r, v_r, kk_r, r_r = (
            [z[bi:bi + 1, h * R_HEAD:(h + 1) * R_HEAD] for bi, h in units] for z in vecs)
        ids = range(len(units))
        s_old = [st_in_ref[bi, h] for bi, h in units]
        sa = [-vec_mat(kk_r[u], s_old[u]) for u in ids]
        upd = []
        for u in ids:
            (sa_h, sa_l), (b_h, b_l) = hi_lo(sa[u]), hi_lo(b_r[u])
            (v_h, v_l), (k_h, k_l) = hi_lo(v_r[u]), hi_lo(k_r[u])
            upd.append(_mm(rows8(sa_h, sa_h, sa_l, v_h, v_h, v_l),
                           rows8(b_h, b_l, b_h, k_h, k_l, k_h), _dot_tn))
        s_new = [s_old[u] * w_r[u] + upd[u] for u in ids]
        for u, (bi, h) in enumerate(units):
            st_ref[bi, h] = s_new[u]
        y_r = [vec_mat(r_r[u], s_new[u]) for u in ids]
        for n, bi in enumerate(range(g0, g0 + group)):
            yrow_ref[bi:bi + 1, :] = jnp.concatenate(y_r[n * R_HEADS:(n + 1) * R_HEADS], axis=1)
    y_ref[...] = _rwkv_finish(yrow_ref[...], bonus, g, gnw_ref[...], gnb_ref[...], ones)


def _rwkv_step(rc, shift_prev, state, params, rows):
    db = rc.shape[0]
    const = lambda i: (0, 0)
    specs = [pl.BlockSpec(p.shape, const) for p in params]
    state_spec = pl.BlockSpec((rows, R_HEADS, R_HEAD, R_HEAD), lambda i: (i, 0, 0, 0))
    return pl.pallas_call(
        functools.partial(_rwkv_step_kernel, rows=rows),
        grid=(db // rows,),
        in_specs=[
            pl.BlockSpec((rows, R_SHIFT_W), lambda i: (i, 0)),
            pl.BlockSpec((rows, R_SHIFT_W), lambda i: (i, 0)),
            state_spec,
        ] + specs,
        out_specs=[pl.BlockSpec((rows, R_WIDTH), lambda i: (i, 0)), state_spec],
        out_shape=(
            jax.ShapeDtypeStruct((db, R_WIDTH), BF16),
            jax.ShapeDtypeStruct((db, R_HEADS, R_HEAD, R_HEAD), F32),
        ),
        scratch_shapes=[pltpu.VMEM((rows, R_WIDTH), F32)],
        compiler_params=pltpu.CompilerParams(
            dimension_semantics=("arbitrary",), vmem_limit_bytes=V7X_VMEM_LIMIT),
        name="rwkv_step",
    )(rc, shift_prev, state, *params)


def _pairs_to_state(st):
    b = st.shape[0]
    blocks = st.reshape(b, R_PAIRS, 2, R_HEAD, 2, R_HEAD)
    diag = jnp.stack([blocks[:, :, 0, :, 0, :], blocks[:, :, 1, :, 1, :]], axis=2)
    return jnp.swapaxes(diag.reshape(b, R_HEADS, R_HEAD, R_HEAD), -1, -2)


def _merge_ffn_kernel(x_ref, o_ref, rw_ref, g_ref, wpa_ref, wpb_ref, wout_ref, nf_ref,
                      wg_ref, wu_ref, wd_ref, y_ref, *, d_ff):
    ya = _dot(o_ref[...].astype(BF16), wpa_ref[...])
    yb = _dot(rw_ref[...], wpb_ref[...])
    merged = (g_ref[:, 0:D_MODEL] * ya + g_ref[:, D_MODEL:] * yb).astype(BF16)
    x1 = x_ref[...] + _dot(merged, wout_ref[...])
    hf = _rms_rows(x1, nf_ref[...]).astype(BF16)
    acc = x1
    step = 512
    for c in range(0, d_ff, step):
        n = min(step, d_ff - c)
        gate = _dot(hf, wg_ref[:, c:c + n])
        up = _dot(hf, wu_ref[:, c:c + n])
        act = (gate * _sigmoid(gate) * up).astype(BF16)
        acc = acc + _dot(act, wd_ref[c:c + n, :])
    y_ref[...] = acc


def _merge_ffn(x2, o, rw, gates, wpa, wpb, wout, nf, wg, wu, wd, tm):
    m = x2.shape[0]
    d_ff = wg.shape[1]
    row = lambda i: (i, 0)
    const = lambda i: (0, 0)
    return pl.pallas_call(
        functools.partial(_merge_ffn_kernel, d_ff=d_ff),
        grid=(m // tm,),
        in_specs=[
            pl.BlockSpec((tm, D_MODEL), row),
            pl.BlockSpec((tm, A_WIDTH), row),
            pl.BlockSpec((tm, R_WIDTH), row),
            pl.BlockSpec((tm, 2 * D_MODEL), row),
            pl.BlockSpec(wpa.shape, const),
            pl.BlockSpec(wpb.shape, const),
            pl.BlockSpec(wout.shape, const),
            pl.BlockSpec((1, D_MODEL), const),
            pl.BlockSpec(wg.shape, const),
            pl.BlockSpec(wu.shape, const),
            pl.BlockSpec(wd.shape, const),
        ],
        out_specs=pl.BlockSpec((tm, D_MODEL), row),
        out_shape=jax.ShapeDtypeStruct((m, D_MODEL), F32),
        compiler_params=pltpu.CompilerParams(
            dimension_semantics=("arbitrary",), vmem_limit_bytes=V7X_VMEM_LIMIT),
        name="merge_ffn",
    )(x2, o, rw, gates, wpa, wpb, wout, nf, wg, wu, wd)


def _row(v):
    return v.reshape(1, -1).astype(F32)


def _pick_tile(m, target):
    t = min(m, target)
    while m % t:
        t //= 2
    return t


def kernel(x_prompt, x_sample, cache_k, cache_v, page_table, state_wkv, state_shift, norm_mix, w_in, q_gain, k_gain, lambda_q1, lambda_k1, lambda_q2, lambda_k2, attn_out_gain, w_pa, shift_mu, w0, w2, a0, a2, g2, k_k, k_a, r_k, gn_w, gn_b, w_pb, w_out, norm_ffn, w_gate, w_up, w_down):
    b, t, _ = x_prompt.shape
    db, ds, _ = x_sample.shape
    assert ds == 1, "sample group carries one new token per sequence"
    depth = w_in.shape[0]
    n_pages = page_table.shape[1]
    xp = x_prompt.reshape(b * t, D_MODEL)
    xs = x_sample.reshape(db, D_MODEL)
    head_ones = (jnp.arange(A_WIDTH)[:, None] // A_HEAD_DIM
                 == jnp.arange(A_WIDTH)[None, :] // A_HEAD_DIM).astype(BF16)
    outs = [[] for _ in range(8)]
    tm_p = _pick_tile(b * t, 512)
    tm_s = _pick_tile(db, 512)
    tq = _pick_tile(t, 1024)
    chunk = _pick_tile(t, RWKV_CHUNK)
    chunk_nb = _pick_tile(b, 4)
    pps = _pick_tile(n_pages, 32)
    step_rows = _pick_tile(db, 16)
    for l in range(depth):
        lambda_init = 0.8 - 0.6 * math.exp(-0.3 * l)
        w_in_l = w_in[l]
        qg = _row(jnp.tile(q_gain[l], A_WIDTH // A_HEAD_DIM))
        kg = _row(jnp.tile(k_gain[l], A_WIDTH // A_HEAD_DIM))
        nm = _row(norm_mix[l])
        lam_vecs = (_row(lambda_q1[l]), _row(lambda_k1[l]), _row(lambda_q2[l]), _row(lambda_k2[l]))
        og = _row(attn_out_gain[l])
        zeros_lora = jnp.zeros((R_LORA_W, R_WIDTH), F32)
        rwkv_params = (
            _row(shift_mu[l]), _row(w0[l]),
            jnp.concatenate([w2[l], zeros_lora], axis=0).astype(BF16),
            _row(a0[l]),
            jnp.concatenate([zeros_lora, a2[l]], axis=0).astype(BF16),
            g2[l].astype(BF16), _row(k_k[l]), _row(k_a[l]), _row(r_k[l]),
            _row(gn_w[l]), _row(gn_b[l]), head_ones)
        ffn_w = (w_pa[l].astype(BF16), w_pb[l].astype(BF16), w_out[l].astype(BF16),
                 _row(norm_ffn[l]), w_gate[l].astype(BF16), w_up[l].astype(BF16),
                 w_down[l].astype(BF16))

        qb, k32, kb, v32, vb, rc, gates = _proj(xp, nm, w_in_l, qg, kg, head_ones, tm_p)
        o = _prompt_attn(qb, kb, vb, *lam_vecs, og, b, t, tq, lambda_init)
        rw, st_p, sh_p = _rwkv_prompt(rc.reshape(b, t, R_SHIFT_W), rwkv_params, chunk, chunk_nb)
        xp = _merge_ffn(xp, o, rw.reshape(b * t, R_WIDTH), gates, *ffn_w, tm_p)
        outs[0].append(k32.reshape(b, t, A_HEADS, A_QK_DIM))
        outs[1].append(v32.reshape(b, t, A_HEADS, A_V_DIM))
        outs[2].append(_pairs_to_state(st_p))
        outs[3].append(sh_p.reshape(b, R_SHIFT_W))

        qb, k32, kb, v32, vb, rc, gates = _proj(xs, nm, w_in_l, qg, kg, head_ones, tm_s)
        o = _decode_attn(page_table, qb, kb, vb, cache_k, cache_v, l, *lam_vecs, og,
                         pps, lambda_init)
        rw, st_s = _rwkv_step(rc, state_shift[l], state_wkv[l], rwkv_params,
                              step_rows)
        xs = _merge_ffn(xs, o, rw, gates, *ffn_w, tm_s)
        outs[4].append(k32.reshape(db, 1, A_HEADS, A_QK_DIM))
        outs[5].append(v32.reshape(db, 1, A_HEADS, A_V_DIM))
        outs[6].append(st_s)
        outs[7].append(rc)
    return (xp.reshape(b, t, D_MODEL), xs.reshape(db, 1, D_MODEL),
            *(jnp.stack(o) for o in outs))
```
